```python
import jax, jax.numpy as jnp
from jax import lax
import numpy as np

D_MODEL = 2048
BATCH = 4
SEQ = 4096
DEPTH = 2

N_MIXERS = 2
NORM_EPS = 1e-6

ATT_HEADS = 16
ATT_HEAD_DIM = D_MODEL // ATT_HEADS
ATT_WIDTH = ATT_HEADS * ATT_HEAD_DIM
MOBA_BLOCK = 256
MOBA_TOPK = 3
MOBA_QCHUNK = 16
ROPE_THETA = 10000.0

MLSTM_HEADS = 8
MLSTM_V_WIDTH = 2 * D_MODEL
MLSTM_V_DIM = MLSTM_V_WIDTH // MLSTM_HEADS
MLSTM_QK_DIM = MLSTM_V_DIM // 2
MLSTM_QK_WIDTH = MLSTM_HEADS * MLSTM_QK_DIM
MLSTM_CONV = 4
MLSTM_CHUNK = 64
MLSTM_IN_WIDTH = 2 * MLSTM_QK_WIDTH + 3 * MLSTM_V_WIDTH + 2 * MLSTM_HEADS

kernel_name = "hybrid_moba_mlstm_adaln"


def rms_norm(x, w):
    xf = x.astype(jnp.float32)
    y = xf * lax.rsqrt(jnp.mean(xf * xf, axis=-1, keepdims=True) + NORM_EPS)
    return (y * w.astype(jnp.float32)).astype(x.dtype)


def rope(x):
    S, D = x.shape[1], x.shape[-1]
    half = D // 2
    inv = ROPE_THETA ** (-jnp.arange(half, dtype=jnp.float32) / half)
    ang = jnp.arange(S, dtype=jnp.float32)[:, None] * inv[None, :]
    cos = jnp.cos(ang)[None, :, None, :]
    sin = jnp.sin(ang)[None, :, None, :]
    xf = x.astype(jnp.float32)
    x1, x2 = xf[..., :half], xf[..., half:]
    return jnp.concatenate([x1 * cos - x2 * sin, x2 * cos + x1 * sin], axis=-1).astype(x.dtype)


def moba_attention(q, k, v):
    B, H, S, D = q.shape
    BS, QC = MOBA_BLOCK, MOBA_QCHUNK
    nb = -(-S // BS)
    pad = nb * BS - S
    kb = jnp.pad(k, ((0, 0), (0, 0), (0, pad), (0, 0))).reshape(B, H, nb, BS, D)
    vb = jnp.pad(v, ((0, 0), (0, 0), (0, pad), (0, 0))).reshape(B, H, nb, BS, D)
    k_mean = jnp.mean(kb.astype(jnp.float32), axis=3)
    q_blk = jnp.arange(S) // BS
    gate = jnp.einsum('bhsd,bhnd->bhsn', q.astype(jnp.float32), k_mean)
    past = jnp.arange(nb)[None, :] < q_blk[:, None]
    gate = jnp.where(past, gate, -jnp.inf)
    topk = min(MOBA_TOPK, nb)
    _, sel = lax.top_k(gate, topk)
    sel_ok = sel < q_blk[:, None]
    nq = S // QC
    scale = D ** -0.5

    def to_chunks(a):
        a = a.reshape(B, H, nq, QC, *a.shape[3:])
        return jnp.moveaxis(a, 2, 0)

    def chunk_fn(args):
        q_c, sel_c, ok_c, c_idx = args
        t0 = c_idx * QC
        blk = t0 // BS
        k_own = lax.dynamic_index_in_dim(kb, blk, axis=2, keepdims=False)
        v_own = lax.dynamic_index_in_dim(vb, blk, axis=2, keepdims=False)
        s_own = jnp.einsum('bhqd,bhkd->bhqk', q_c, k_own).astype(jnp.float32) * scale
        q_pos = t0 + jnp.arange(QC)
        k_pos = blk * BS + jnp.arange(BS)
        s_own = jnp.where(k_pos[None, :] <= q_pos[:, None], s_own, -jnp.inf)
        flat = sel_c.reshape(B, H, QC * topk)[:, :, :, None, None]
        k_sel = jnp.take_along_axis(kb, flat, axis=2).reshape(B, H, QC, topk, BS, D)
        v_sel = jnp.take_along_axis(vb, flat, axis=2).reshape(B, H, QC, topk, BS, D)
        s_sel = jnp.einsum('bhqd,bhqrkd->bhqrk', q_c, k_sel).astype(jnp.float32) * scale
        s_sel = jnp.where(ok_c[..., None], s_sel, -jnp.inf).reshape(B, H, QC, topk * BS)
        p = jax.nn.softmax(jnp.concatenate([s_own, s_sel], axis=-1), axis=-1)
        p_own = p[..., :BS].astype(v.dtype)
        p_sel = p[..., BS:].reshape(B, H, QC, topk, BS).astype(v.dtype)
        return (jnp.einsum('bhqk,bhkd->bhqd', p_own, v_own)
                + jnp.einsum('bhqrk,bhqrkd->bhqd', p_sel, v_sel))

    out = lax.map(chunk_fn, (to_chunks(q), to_chunks(sel), to_chunks(sel_ok), jnp.arange(nq)))
    return jnp.moveaxis(out, 0, 2).reshape(B, H, S, D)


def moba_branch(h, w_in, q_norm_w, k_norm_w, w_out):
    B, S, _ = h.shape
    proj = h @ w_in
    q, k, v, z = jnp.split(proj, 4, axis=-1)
    shp = (B, S, ATT_HEADS, ATT_HEAD_DIM)
    q = rope(rms_norm(q.reshape(shp), q_norm_w))
    k = rope(rms_norm(k.reshape(shp), k_norm_w))
    v = v.reshape(shp)
    o = moba_attention(q.transpose(0, 2, 1, 3), k.transpose(0, 2, 1, 3), v.transpose(0, 2, 1, 3))
    o = o.transpose(0, 2, 1, 3).reshape(B, S, ATT_WIDTH)
    return (o * jax.nn.silu(z)) @ w_out


def causal_depthwise_conv(x, w):
    K, C = w.shape
    return lax.conv_general_dilated(
        x, w[:, None, :].astype(x.dtype), window_strides=(1,), padding=[(K - 1, 0)],
        dimension_numbers=('NWC', 'WIO', 'NWC'), feature_group_count=C)


def mlstm_chunkwise(q, k, v, i_pre, f_pre):
    B, H, S, dk = q.shape
    dv = v.shape[-1]
    L = MLSTM_CHUNK
    nc = S // L
    q = q.astype(jnp.float32)
    k = k.astype(jnp.float32) * (dk ** -0.5)
    v = v.astype(jnp.float32)
    log_f = jax.nn.log_sigmoid(f_pre.astype(jnp.float32))
    i_pre = i_pre.astype(jnp.float32)

    def chunks(a):
        return jnp.moveaxis(a.reshape(B, H, nc, L, *a.shape[3:]), 2, 0)

    causal = jnp.tril(jnp.ones((L, L), dtype=bool))

    def step(carry, xs):
        C, n, m = carry
        qc, kc, vc, ic, lfc = xs
        b = jnp.cumsum(lfc, axis=-1)
        log_inter = b + m[..., None]
        log_D = b[..., :, None] - b[..., None, :] + ic[..., None, :]
        log_D = jnp.where(causal, log_D, -jnp.inf)
        m_t = jnp.maximum(log_inter, jnp.max(log_D, axis=-1))
        w_inter = jnp.exp(log_inter - m_t)
        s_qk = jnp.einsum('bhtd,bhsd->bhts', qc, kc) * jnp.exp(log_D - m_t[..., None])
        num = (w_inter[..., None] * jnp.einsum('bhtd,bhde->bhte', qc, C)
               + jnp.einsum('bhts,bhse->bhte', s_qk, vc))
        den = w_inter * jnp.einsum('bhtd,bhd->bht', qc, n) + jnp.sum(s_qk, axis=-1)
        h = num / jnp.maximum(jnp.abs(den), jnp.exp(-m_t))[..., None]
        b_L = b[..., -1]
        log_s = b_L[..., None] - b + ic
        m_new = jnp.maximum(b_L + m, jnp.max(log_s, axis=-1))
        decay = jnp.exp(b_L + m - m_new)
        ws = jnp.exp(log_s - m_new[..., None])
        C_new = decay[..., None, None] * C + jnp.einsum('bhs,bhsd,bhse->bhde', ws, kc, vc)
        n_new = decay[..., None] * n + jnp.einsum('bhs,bhsd->bhd', ws, kc)
        return (C_new, n_new, m_new), h

    init = (jnp.zeros((B, H, dk, dv), jnp.float32), jnp.zeros((B, H, dk), jnp.float32),
            jnp.zeros((B, H), jnp.float32))
    _, hs = lax.scan(step, init, (chunks(q), chunks(k), chunks(v), chunks(i_pre), chunks(log_f)))
    return jnp.moveaxis(hs, 0, 2).reshape(B, H, S, dv)


def mlstm_branch(h, w_in, gate_b, conv_w, norm_w, w_out):
    B, S, _ = h.shape
    H, dk, dv = MLSTM_HEADS, MLSTM_QK_DIM, MLSTM_V_DIM
    proj = h @ w_in
    o1 = 2 * MLSTM_QK_WIDTH
    o2 = o1 + MLSTM_V_WIDTH
    o3 = o2 + MLSTM_V_WIDTH
    o4 = o3 + MLSTM_V_WIDTH
    qk = jax.nn.silu(causal_depthwise_conv(proj[..., :o1], conv_w))
    q = qk[..., :MLSTM_QK_WIDTH].reshape(B, S, H, dk).transpose(0, 2, 1, 3)
    k = qk[..., MLSTM_QK_WIDTH:].reshape(B, S, H, dk).transpose(0, 2, 1, 3)
    v = proj[..., o1:o2].reshape(B, S, H, dv).transpose(0, 2, 1, 3)
    o_gate = jax.nn.sigmoid(proj[..., o2:o3])
    z = proj[..., o3:o4]
    gates = proj[..., o4:].astype(jnp.float32) + gate_b.astype(jnp.float32)
    i_pre = gates[..., :H].transpose(0, 2, 1)
    f_pre = gates[..., H:].transpose(0, 2, 1)
    hc = mlstm_chunkwise(q, k, v, i_pre, f_pre)
    hc = hc.transpose(0, 2, 1, 3).astype(h.dtype)
    hn = rms_norm(hc, norm_w.reshape(H, dv)).reshape(B, S, MLSTM_V_WIDTH)
    return (hn * o_gate * jax.nn.silu(z)) @ w_out


def setup_inputs(seed: int = 0) -> dict:
    key = jax.random.key(seed)
    ks = jax.random.split(key, 16)
    d = D_MODEL
    n_att = (DEPTH + 1) // 2
    n_ml = DEPTH // 2
    nrm = jax.random.normal
    x = nrm(ks[0], (BATCH, SEQ, d), jnp.float32)
    c = nrm(ks[1], (BATCH, d), jnp.float32)
    norm_w = 1.0 + 0.02 * nrm(ks[2], (DEPTH, d), jnp.float32)
    ada_w = nrm(ks[3], (DEPTH, d, 3 * d), jnp.float32) * d ** -0.5
    ada_b = 0.02 * nrm(ks[4], (DEPTH, 3 * d), jnp.float32)
    att_w_in = nrm(ks[5], (n_att, d, 4 * ATT_WIDTH), jnp.float32) * d ** -0.5
    att_q_norm = 1.0 + 0.02 * nrm(ks[6], (n_att, ATT_HEAD_DIM), jnp.float32)
    att_k_norm = 1.0 + 0.02 * nrm(ks[7], (n_att, ATT_HEAD_DIM), jnp.float32)
    att_w_out = nrm(ks[8], (n_att, ATT_WIDTH, d), jnp.float32) * ATT_WIDTH ** -0.5
    mlstm_w_in = nrm(ks[9], (n_ml, d, MLSTM_IN_WIDTH), jnp.float32) * d ** -0.5
    i_bias = 0.1 * nrm(ks[10], (n_ml, MLSTM_HEADS), jnp.float32)
    f_bias = 3.0 + 3.0 * jax.random.uniform(ks[11], (n_ml, MLSTM_HEADS), jnp.float32)
    mlstm_gate_b = jnp.concatenate([i_bias, f_bias], axis=-1)
    mlstm_conv_w = nrm(ks[12], (n_ml, MLSTM_CONV, 2 * MLSTM_QK_WIDTH), jnp.float32) * MLSTM_CONV ** -0.5
    mlstm_norm_w = 1.0 + 0.02 * nrm(ks[13], (n_ml, MLSTM_V_WIDTH), jnp.float32)
    mlstm_w_out = nrm(ks[14], (n_ml, MLSTM_V_WIDTH, d), jnp.float32) * MLSTM_V_WIDTH ** -0.5
    return {"x": x, "c": c, "norm_w": norm_w, "ada_w": ada_w, "ada_b": ada_b,
            "att_w_in": att_w_in, "att_q_norm": att_q_norm, "att_k_norm": att_k_norm,
            "att_w_out": att_w_out, "mlstm_w_in": mlstm_w_in, "mlstm_gate_b": mlstm_gate_b,
            "mlstm_conv_w": mlstm_conv_w, "mlstm_norm_w": mlstm_norm_w, "mlstm_w_out": mlstm_w_out}


def reference(x, c, norm_w, ada_w, ada_b, att_w_in, att_q_norm, att_k_norm, att_w_out,
              mlstm_w_in, mlstm_gate_b, mlstm_conv_w, mlstm_norm_w, mlstm_w_out):
    cond = jax.nn.silu(c)
    for layer in range(DEPTH):
        mod = cond @ ada_w[layer] + ada_b[layer]
        shift, scale, gate = jnp.split(mod, 3, axis=-1)
        h = rms_norm(x, norm_w[layer]) * (1.0 + scale[:, None, :]) + shift[:, None, :]
        j = layer // N_MIXERS
        if layer % N_MIXERS == 0:
            y = moba_branch(h, att_w_in[j], att_q_norm[j], att_k_norm[j], att_w_out[j])
        else:
            y = mlstm_branch(h, mlstm_w_in[j], mlstm_gate_b[j], mlstm_conv_w[j],
                             mlstm_norm_w[j], mlstm_w_out[j])
        x = x + gate[:, None, :] * y
    return x
```

```python
import functools

import jax
import jax.numpy as jnp
from jax import lax
from jax.experimental import pallas as pl
from jax.experimental.pallas import tpu as pltpu

F32 = jnp.float32
BF16 = jnp.bfloat16
HIGHEST = lax.Precision.HIGHEST

NORM_EPS = 1e-6
MOBA_BLOCK = 256
MOBA_TOPK = 3
ROPE_THETA = 10000.0
MLSTM_CHUNK = 256
MASK_BIAS = -1e30
LANES = 128

_VMEM_LIMIT = 48 * 1024 * 1024


def _sigmoid(x):
    return 1.0 / (1.0 + jnp.exp(-x))


def _cparams(*sem):
    return pltpu.CompilerParams(dimension_semantics=sem, vmem_limit_bytes=_VMEM_LIMIT)


def _adaln_kernel(c_ref, w_ref, b_ref, o_ref):
    c = c_ref[...]
    cond = c * _sigmoid(c)
    o_ref[...] = jnp.dot(cond, w_ref[...], preferred_element_type=F32, precision=HIGHEST) + b_ref[...]


def _adaln(c, ada_w, ada_b, tn=768):
    depth, d, n3 = ada_w.shape
    b = c.shape[0]
    rows = 8
    c_pad = jnp.zeros((rows, d), F32).at[:b].set(c)
    out = pl.pallas_call(
        _adaln_kernel,
        out_shape=jax.ShapeDtypeStruct((depth, rows, n3), F32),
        grid=(depth, n3 // tn),
        in_specs=[
            pl.BlockSpec((rows, d), lambda l, n: (0, 0)),
            pl.BlockSpec((None, d, tn), lambda l, n: (l, 0, n)),
            pl.BlockSpec((None, 1, tn), lambda l, n: (l, 0, n)),
        ],
        out_specs=pl.BlockSpec((None, rows, tn), lambda l, n: (l, 0, n)),
        compiler_params=_cparams("parallel", "parallel"),
        name="adaln_mod",
    )(c_pad, ada_w, ada_b.reshape(depth, 1, n3))
    return out[:, :b]


def _norm_mod_kernel(x_ref, nw_ref, sc_ref, sh_ref, o_ref):
    x = x_ref[...]
    ms = jnp.mean(x * x, axis=-1, keepdims=True)
    y = x * lax.rsqrt(ms + NORM_EPS) * nw_ref[...]
    o_ref[...] = (y * (1.0 + sc_ref[...]) + sh_ref[...]).astype(BF16)


def _norm_mod(xf, nw, scale, shift, seq, ts=512):
    m, d = xf.shape
    b = m // seq
    tps = seq // ts
    return pl.pallas_call(
        _norm_mod_kernel,
        out_shape=jax.ShapeDtypeStruct((m, d), BF16),
        grid=(m // ts,),
        in_specs=[
            pl.BlockSpec((ts, d), lambda i: (i, 0)),
            pl.BlockSpec((1, d), lambda i: (0, 0)),
            pl.BlockSpec((None, 1, d), lambda i: (i // tps, 0, 0)),
            pl.BlockSpec((None, 1, d), lambda i: (i // tps, 0, 0)),
        ],
        out_specs=pl.BlockSpec((ts, d), lambda i: (i, 0)),
        compiler_params=_cparams("parallel"),
        name="norm_mod",
    )(xf, nw.reshape(1, d), scale.reshape(b, 1, d), shift.reshape(b, 1, d))


def _matmul_acc(a_ref, w_ref, wb_ref):
    @pl.when(pl.program_id(1) == 0)
    def _():
        wb_ref[...] = w_ref[...].astype(BF16)

    return jnp.dot(a_ref[...], wb_ref[...], preferred_element_type=F32)


def _attn_inproj_kernel(h_ref, w_ref, cos_ref, sin_ref, nw_ref, o_ref, wb_ref, *,
                        n_qk_blocks, n_q_blocks, head_dim, q_scale):
    n = pl.program_id(0)
    acc = _matmul_acc(h_ref, w_ref, wb_ref)
    tn = acc.shape[1]

    @pl.when(n < n_qk_blocks)
    def _():
        cos = cos_ref[...]
        sin = sin_ref[...]
        scale = jnp.where(n < n_q_blocks, q_scale, 1.0).astype(F32)
        for hh in range(tn // head_dim):
            sl = slice(hh * head_dim, (hh + 1) * head_dim)
            xh = acc[:, sl]
            ms = jnp.mean(xh * xh, axis=-1, keepdims=True)
            y = xh * lax.rsqrt(ms + NORM_EPS) * nw_ref[:, sl]
            r = y * cos + pltpu.roll(y, head_dim // 2, axis=1) * sin
            o_ref[:, sl] = (r * scale).astype(BF16)

    @pl.when(n >= n_qk_blocks)
    def _():
        o_ref[...] = acc.astype(BF16)


def _attn_inproj(h, w_in, q_norm, k_norm, seq, tm=512, tn=1024):
    m, kdim = h.shape
    n_total = w_in.shape[1]
    width = n_total // 4
    hd = q_norm.shape[0]
    heads = width // hd
    tps = seq // tm
    half = hd // 2
    inv = ROPE_THETA ** (-jnp.arange(half, dtype=F32) / half)
    ang = jnp.arange(seq, dtype=F32)[:, None] * inv[None, :]
    cos = jnp.cos(ang)
    sin = jnp.sin(ang)
    cos_t = jnp.concatenate([cos, cos], axis=-1)
    sin_t = jnp.concatenate([-sin, sin], axis=-1)
    nw = jnp.concatenate([jnp.tile(q_norm, heads), jnp.tile(k_norm, heads)]).reshape(1, 2 * width)
    n_qk_blocks = 2 * width // tn
    kern = functools.partial(_attn_inproj_kernel, n_qk_blocks=n_qk_blocks, n_q_blocks=width // tn,
                             head_dim=hd, q_scale=float(hd) ** -0.5)
    return pl.pallas_call(
        kern,
        out_shape=jax.ShapeDtypeStruct((m, n_total), BF16),
        grid=(n_total // tn, m // tm),
        in_specs=[
            pl.BlockSpec((tm, kdim), lambda n, i: (i, 0)),
            pl.BlockSpec((kdim, tn), lambda n, i: (0, n)),
            pl.BlockSpec((tm, hd), lambda n, i: (i % tps, 0)),
            pl.BlockSpec((tm, hd), lambda n, i: (i % tps, 0)),
            pl.BlockSpec((1, tn), lambda n, i: (0, jnp.minimum(n, n_qk_blocks - 1))),
        ],
        out_specs=pl.BlockSpec((tm, tn), lambda n, i: (i, n)),
        scratch_shapes=[pltpu.VMEM((kdim, tn), BF16)],
        compiler_params=_cparams("parallel", "arbitrary"),
        name="attn_inproj",
    )(h, w_in, cos_t, sin_t, nw)


def _moba_kernel(q_ref, k_ref, v_ref, z_ref, e_ref, u_ref, kmean_ref, *, bs, nb, topk, hd):
    i = pl.program_id(2)

    @pl.when(i == 0)
    def _():
        kf = k_ref[...].astype(F32)
        kmean_ref[...] = jnp.sum(kf.reshape(nb, bs, hd), axis=1) * (1.0 / bs)

    q = q_ref[...]
    g_t = lax.dot_general(kmean_ref[...], q.astype(F32), (((1,), (1,)), ((), ())),
                          preferred_element_type=F32, precision=HIGHEST)
    jj = lax.broadcasted_iota(jnp.int32, (nb, bs), 0)
    rank = jnp.zeros((nb, bs), jnp.int32)
    for jp in range(nb):
        row = g_t[jp:jp + 1, :]
        beats = (row > g_t) | ((row == g_t) & (jp < jj))
        rank = rank + jnp.where(beats, 1, 0) * (jp < i).astype(jnp.int32)
    allowed = ((jj < i) & (rank < topk)) | (jj == i)
    bias_t = jnp.where(allowed, 0.0, MASK_BIAS).astype(F32)
    bias_t = jnp.concatenate([bias_t, jnp.zeros((LANES - nb, bs), F32)], axis=0)
    q_aug = jnp.concatenate([q, bias_t.T.astype(BF16)], axis=1)

    def scores(j):
        start = pl.multiple_of(j * bs, bs)
        k_aug = jnp.concatenate([k_ref[pl.ds(start, bs), :], e_ref[pl.ds(start, bs), :]], axis=1)
        return lax.dot_general(q_aug, k_aug, (((1,), (1,)), ((), ())), preferred_element_type=F32)

    def pv(p, j):
        start = pl.multiple_of(j * bs, bs)
        return jnp.dot(p.astype(BF16), v_ref[pl.ds(start, bs), :], preferred_element_type=F32)

    s = scores(i)
    row_id = lax.broadcasted_iota(jnp.int32, (bs, bs), 0)
    col_id = lax.broadcasted_iota(jnp.int32, (bs, bs), 1)
    s = jnp.where(col_id <= row_id, s, -jnp.inf)
    m0 = jnp.max(s, axis=1, keepdims=True)
    p = jnp.exp(s - m0)
    l0 = jnp.sum(p, axis=1, keepdims=True)
    acc0 = pv(p, i)

    def body(j, carry):
        m_run, l_run, acc = carry
        s = scores(j)
        m_new = jnp.maximum(m_run, jnp.max(s, axis=1, keepdims=True))
        alpha = jnp.exp(m_run - m_new)
        p = jnp.exp(s - m_new)
        l_new = alpha * l_run + jnp.sum(p, axis=1, keepdims=True)
        return m_new, l_new, alpha * acc + pv(p, j)

    _, l_fin, acc = lax.fori_loop(0, i, body, (m0, l0, acc0))
    o = acc / l_fin
    z = z_ref[...].astype(F32)
    u_ref[...] = (o * (z * _sigmoid(z))).astype(BF16)


def _moba_attention(qkvz, batch, seq, heads, hd):
    m = qkvz.shape[0]
    bs = MOBA_BLOCK
    nb = seq // bs
    width = heads * hd
    e_tab = (jnp.arange(seq)[:, None] // bs == jnp.arange(LANES)[None, :]).astype(BF16)
    kern = functools.partial(_moba_kernel, bs=bs, nb=nb, topk=min(MOBA_TOPK, nb), hd=hd)
    return pl.pallas_call(
        kern,
        out_shape=jax.ShapeDtypeStruct((m, width), BF16),
        grid=(batch, heads, nb),
        in_specs=[
            pl.BlockSpec((bs, hd), lambda b, h, i: (b * nb + i, h)),
            pl.BlockSpec((seq, hd), lambda b, h, i: (b, heads + h)),
            pl.BlockSpec((seq, hd), lambda b, h, i: (b, 2 * heads + h)),
            pl.BlockSpec((bs, hd), lambda b, h, i: (b * nb + i, 3 * heads + h)),
            pl.BlockSpec((seq, LANES), lambda b, h, i: (0, 0)),
        ],
        out_specs=pl.BlockSpec((bs, hd), lambda b, h, i: (b * nb + i, h)),
        scratch_shapes=[pltpu.VMEM((nb, hd), F32)],
        compiler_params=_cparams("parallel", "parallel", "arbitrary"),
        name="moba_attn",
    )(qkvz, qkvz, qkvz, qkvz, e_tab)


def _outproj_kernel(u_ref, w_ref, x_ref, g_ref, o_ref, wb_ref):
    acc = _matmul_acc(u_ref, w_ref, wb_ref)
    o_ref[...] = x_ref[...] + g_ref[...] * acc


def _outproj(u, w_out, xf, gate, seq, tm=512, tn=512):
    m, kdim = u.shape
    d = w_out.shape[1]
    b = m // seq
    tps = seq // tm
    return pl.pallas_call(
        _outproj_kernel,
        out_shape=jax.ShapeDtypeStruct((m, d), F32),
        grid=(d // tn, m // tm),
        in_specs=[
            pl.BlockSpec((tm, kdim), lambda n, i: (i, 0)),
            pl.BlockSpec((kdim, tn), lambda n, i: (0, n)),
            pl.BlockSpec((tm, tn), lambda n, i: (i, n)),
            pl.BlockSpec((None, 1, tn), lambda n, i: (i // tps, 0, n)),
        ],
        out_specs=pl.BlockSpec((tm, tn), lambda n, i: (i, n)),
        scratch_shapes=[pltpu.VMEM((kdim, tn), BF16)],
        compiler_params=_cparams("parallel", "arbitrary"),
        name="outproj",
    )(u, w_out, xf, gate.reshape(b, 1, d))


def _mlstm_inproj_kernel(h_ref, w_ref, cw_ref, o_ref, wb_ref, ext_ref, *,
                         n_conv_blocks, n_q_blocks, tiles_per_seq, conv_k, k_scale):
    n = pl.program_id(0)
    i = pl.program_id(1)
    acc = _matmul_acc(h_ref, w_ref, wb_ref)
    tm = acc.shape[0]

    @pl.when(n < n_conv_blocks)
    def _():
        @pl.when(i % tiles_per_seq == 0)
        def _():
            ext_ref[0:8, :] = jnp.zeros((8, acc.shape[1]), F32)

        @pl.when(i % tiles_per_seq != 0)
        def _():
            ext_ref[0:8, :] = ext_ref[tm:tm + 8, :]

        ext_ref[8:tm + 8, :] = acc
        y = cw_ref[conv_k - 1:conv_k, :] * acc
        for back in range(1, conv_k):
            y = y + cw_ref[conv_k - 1 - back:conv_k - back, :] * ext_ref[8 - back:8 - back + tm, :]
        y = y * _sigmoid(y)
        scale = jnp.where(n >= n_q_blocks, k_scale, 1.0).astype(F32)
        o_ref[...] = (y * scale).astype(BF16)

    @pl.when(n >= n_conv_blocks)
    def _():
        o_ref[...] = acc.astype(BF16)


def _mlstm_inproj(h, w_in, conv_w, n_main, qk_width, dk, seq, tm=512, tn=1024):
    m, kdim = h.shape
    conv_k = conv_w.shape[0]
    tps = seq // tm
    n_conv_blocks = 2 * qk_width // tn
    kern = functools.partial(_mlstm_inproj_kernel, n_conv_blocks=n_conv_blocks, n_q_blocks=qk_width // tn,
                             tiles_per_seq=tps, conv_k=conv_k, k_scale=float(dk) ** -0.5)
    return pl.pallas_call(
        kern,
        out_shape=jax.ShapeDtypeStruct((m, n_main), BF16),
        grid=(n_main // tn, m // tm),
        in_specs=[
            pl.BlockSpec((tm, kdim), lambda n, i: (i, 0)),
            pl.BlockSpec((kdim, tn), lambda n, i: (0, n)),
            pl.BlockSpec((conv_k, tn), lambda n, i: (0, jnp.minimum(n, n_conv_blocks - 1))),
        ],
        out_specs=pl.BlockSpec((tm, tn), lambda n, i: (i, n)),
        scratch_shapes=[pltpu.VMEM((kdim, tn), BF16), pltpu.VMEM((tm + 8, tn), F32)],
        compiler_params=_cparams("arbitrary", "arbitrary"),
        name="mlstm_inproj",
    )(h, w_in, conv_w)


def _gate_proj_kernel(h_ref, w_ref, b_ref, o_ref):
    o_ref[...] = jnp.dot(h_ref[...], w_ref[...].astype(BF16), preferred_element_type=F32) + b_ref[...]


def _gate_proj(h, w_gate, gate_b, tm=1024):
    m, kdim = h.shape
    ng = w_gate.shape[1]
    w_pad = jnp.zeros((kdim, LANES), F32).at[:, :ng].set(w_gate)
    b_pad = jnp.zeros((1, LANES), F32).at[0, :ng].set(gate_b)
    out = pl.pallas_call(
        _gate_proj_kernel,
        out_shape=jax.ShapeDtypeStruct((m, LANES), F32),
        grid=(m // tm,),
        in_specs=[
            pl.BlockSpec((tm, kdim), lambda i: (i, 0)),
            pl.BlockSpec((kdim, LANES), lambda i: (0, 0)),
            pl.BlockSpec((1, LANES), lambda i: (0, 0)),
        ],
        out_specs=pl.BlockSpec((tm, LANES), lambda i: (i, 0)),
        compiler_params=_cparams("parallel"),
        name="mlstm_gate_proj",
    )(h, w_pad, b_pad)
    return out[:, :ng]


def _mlstm_kernel(q_ref, k_ref, v_ref, o_ref, z_ref, g_ref, nw_ref, u_ref, c_ref, m_ref, *,
                  chunk, heads, dv):
    h = pl.program_id(1)
    c = pl.program_id(2)
    L = chunk

    @pl.when(c == 0)
    def _():
        c_ref[...] = jnp.zeros(c_ref.shape, F32)
        m_ref[...] = jnp.zeros(m_ref.shape, F32)

    q = q_ref[...]
    k = k_ref[...]
    v = v_ref[...]
    i_row = g_ref[pl.ds(h, 1), :]
    f_row = g_ref[pl.ds(heads + h, 1), :]
    lf_row = jnp.minimum(f_row, 0.0) - jnp.log1p(jnp.exp(-jnp.abs(f_row)))

    t_id = lax.broadcasted_iota(jnp.int32, (L, L), 0)
    s_id = lax.broadcasted_iota(jnp.int32, (L, L), 1)
    causal = s_id <= t_id
    tri = jnp.where(causal, 1.0, 0.0).astype(F32)
    eye = jnp.where(s_id == t_id, 1.0, 0.0).astype(F32)
    rid = lax.broadcasted_iota(jnp.int32, (8, L), 0)
    rows8 = jnp.where(rid == 0, i_row, jnp.where(rid == 1, lf_row, 0.0))
    nt = (((1,), (1,)), ((), ()))
    b_row = lax.dot_general(rows8, tri, nt, preferred_element_type=F32, precision=HIGHEST)[1:2, :]
    cols_c = lax.dot_general(tri, rows8, nt, preferred_element_type=F32, precision=HIGHEST)
    cols_i = lax.dot_general(eye, rows8, nt, preferred_element_type=F32, precision=HIGHEST)
    b_col = cols_c[:, 1:2]
    i_col = cols_i[:, 0:1]

    m_prev = m_ref[...]
    log_d = jnp.where(causal, b_col - b_row + i_row, -jnp.inf)
    log_inter = b_col + m_prev
    m_t = jnp.maximum(log_inter, jnp.max(log_d, axis=1, keepdims=True))
    d_mat = jnp.exp(log_d - m_t)
    w_inter = jnp.exp(log_inter - m_t)

    s_qk = lax.dot_general(q, k, nt, preferred_element_type=F32) * d_mat
    lane = lax.broadcasted_iota(jnp.int32, (L, LANES), 1)
    ones_blk = jnp.where(lane == 0, 1.0, 0.0).astype(BF16)
    v_aug = jnp.concatenate([v, ones_blk], axis=1)
    c_state = c_ref[...]
    num = (w_inter * jnp.dot(q, c_state.astype(BF16), preferred_element_type=F32)
           + jnp.dot(s_qk.astype(BF16), v_aug, preferred_element_type=F32))
    den = num[:, dv:dv + 1]
    hh = num[:, :dv] / jnp.maximum(jnp.abs(den), jnp.exp(-m_t))
    ms = jnp.mean(hh * hh, axis=-1, keepdims=True)
    hn = hh * lax.rsqrt(ms + NORM_EPS) * nw_ref[...]
    og = o_ref[...].astype(F32)
    z = z_ref[...].astype(F32)
    u_ref[...] = (hn * _sigmoid(og) * (z * _sigmoid(z))).astype(BF16)

    b_last = b_row[:, L - 1:L]
    log_s_row = b_last - b_row + i_row
    m_new = jnp.maximum(b_last + m_prev, jnp.max(log_s_row, axis=1, keepdims=True))
    decay = jnp.exp(b_last + m_prev - m_new)
    ws_col = jnp.exp(b_last - b_col + i_col - m_new)
    wv = (ws_col * v_aug.astype(F32)).astype(BF16)
    upd = lax.dot_general(k, wv, (((0,), (0,)), ((), ())), preferred_element_type=F32)
    c_ref[...] = decay * c_state + upd
    m_ref[...] = m_new


def _mlstm_core(proj, gates_t, norm_w, batch, seq, heads, dk, dv):
    m = proj.shape[0]
    L = MLSTM_CHUNK
    nc = seq // L
    v_width = heads * dv
    qk_blocks = heads
    v_off = 2 * heads * dk // dv
    o_off = v_off + heads
    z_off = o_off + heads
    kern = functools.partial(_mlstm_kernel, chunk=L, heads=heads, dv=dv)
    return pl.pallas_call(
        kern,
        out_shape=jax.ShapeDtypeStruct((m, v_width), BF16),
        grid=(batch, heads, nc),
        in_specs=[
            pl.BlockSpec((L, dk), lambda b, h, c: (b * nc + c, h)),
            pl.BlockSpec((L, dk), lambda b, h, c: (b * nc + c, qk_blocks + h)),
            pl.BlockSpec((L, dv), lambda b, h, c: (b * nc + c, v_off + h)),
            pl.BlockSpec((L, dv), lambda b, h, c: (b * nc + c, o_off + h)),
            pl.BlockSpec((L, dv), lambda b, h, c: (b * nc + c, z_off + h)),
            pl.BlockSpec((None, 2 * heads, L), lambda b, h, c: (b, 0, c)),
            pl.BlockSpec((1, dv), lambda b, h, c: (0, h)),
        ],
        out_specs=pl.BlockSpec((L, dv), lambda b, h, c: (b * nc + c, h)),
        scratch_shapes=[pltpu.VMEM((dk, dv + LANES), F32), pltpu.VMEM((1, 1), F32)],
        compiler_params=_cparams("parallel", "parallel", "arbitrary"),
        name="mlstm_core",
    )(proj, proj, proj, proj, proj, gates_t, norm_w.reshape(1, v_width))


def _moba_layer(xf, h, gate, w_in, q_norm, k_norm, w_out, batch, seq):
    hd = q_norm.shape[0]
    heads = w_in.shape[1] // 4 // hd
    qkvz = _attn_inproj(h, w_in, q_norm, k_norm, seq)
    u = _moba_attention(qkvz, batch, seq, heads, hd)
    return _outproj(u, w_out, xf, gate, seq)


def _mlstm_layer(xf, h, gate, w_in, gate_b, conv_w, norm_w, w_out, batch, seq):
    heads = gate_b.shape[0] // 2
    v_width = w_out.shape[0]
    dv = v_width // heads
    qk_width = conv_w.shape[1] // 2
    dk = qk_width // heads
    n_main = 2 * qk_width + 3 * v_width
    proj = _mlstm_inproj(h, w_in, conv_w, n_main, qk_width, dk, seq)
    gates = _gate_proj(h, w_in[:, n_main:], gate_b)
    gates_t = gates.reshape(batch, seq, 2 * heads).transpose(0, 2, 1)
    u = _mlstm_core(proj, gates_t, norm_w, batch, seq, heads, dk, dv)
    return _outproj(u, w_out, xf, gate, seq)


def kernel(x, c, norm_w, ada_w, ada_b, att_w_in, att_q_norm, att_k_norm, att_w_out, mlstm_w_in,
           mlstm_gate_b, mlstm_conv_w, mlstm_norm_w, mlstm_w_out):
    batch, seq, d = x.shape
    depth = norm_w.shape[0]
    mod = _adaln(c, ada_w, ada_b)
    xf = x.reshape(batch * seq, d)
    for layer in range(depth):
        shift, scale, gate = mod[layer, :, :d], mod[layer, :, d:2 * d], mod[layer, :, 2 * d:]
        h = _norm_mod(xf, norm_w[layer], scale, shift, seq)
        j = layer // 2
        if layer % 2 == 0:
            xf = _moba_layer(xf, h, gate, att_w_in[j], att_q_norm[j], att_k_norm[j], att_w_out[j], batch, seq)
        else:
            xf = _mlstm_layer(xf, h, gate, mlstm_w_in[j], mlstm_gate_b[j], mlstm_conv_w[j],
                              mlstm_norm_w[j], mlstm_w_out[j], batch, seq)
    return xf.reshape(batch, seq, d)
```

```python
import functools

import jax
import jax.numpy as jnp
from jax import lax
from jax.experimental import pallas as pl
from jax.experimental.pallas import tpu as pltpu

F32 = jnp.float32
BF16 = jnp.bfloat16
HIGHEST = lax.Precision.HIGHEST

NORM_EPS = 1e-6
MOBA_BLOCK = 256
MOBA_TOPK = 3
ROPE_THETA = 10000.0
MLSTM_CHUNK = 256
MASK_BIAS = -1e30
LANES = 128
LOG2_E = 1.4426950408889634

_VMEM_LIMIT = 48 * 1024 * 1024


def _sigmoid(x):
    return 1.0 / (1.0 + jnp.exp(-x))


def _cparams(*sem):
    return pltpu.CompilerParams(dimension_semantics=sem, vmem_limit_bytes=_VMEM_LIMIT)


def _adaln_kernel(c_ref, w_ref, b_ref, o_ref):
    c = c_ref[...]
    cond = c * _sigmoid(c)
    o_ref[...] = jnp.dot(cond, w_ref[...], preferred_element_type=F32, precision=HIGHEST) + b_ref[...]


def _adaln(c, ada_w, ada_b, tn=768):
    depth, d, n3 = ada_w.shape
    b = c.shape[0]
    rows = 8
    c_pad = jnp.zeros((rows, d), F32).at[:b].set(c)
    out = pl.pallas_call(
        _adaln_kernel,
        out_shape=jax.ShapeDtypeStruct((depth, rows, n3), F32),
        grid=(depth, n3 // tn),
        in_specs=[
            pl.BlockSpec((rows, d), lambda l, n: (0, 0)),
            pl.BlockSpec((None, d, tn), lambda l, n: (l, 0, n)),
            pl.BlockSpec((None, 1, tn), lambda l, n: (l, 0, n)),
        ],
        out_specs=pl.BlockSpec((None, rows, tn), lambda l, n: (l, 0, n)),
        compiler_params=_cparams("parallel", "parallel"),
        name="adaln_mod",
    )(c_pad, ada_w, ada_b.reshape(depth, 1, n3))
    return out[:, :b]


def _norm_mod_kernel(x_ref, nw_ref, sc_ref, sh_ref, o_ref):
    x = x_ref[...]
    ms = jnp.mean(x * x, axis=-1, keepdims=True)
    y = x * lax.rsqrt(ms + NORM_EPS) * nw_ref[...]
    o_ref[...] = (y * (1.0 + sc_ref[...]) + sh_ref[...]).astype(BF16)


def _norm_mod(xf, nw, scale, shift, seq, ts=512):
    m, d = xf.shape
    b = m // seq
    tps = seq // ts
    return pl.pallas_call(
        _norm_mod_kernel,
        out_shape=jax.ShapeDtypeStruct((m, d), BF16),
        grid=(m // ts,),
        in_specs=[
            pl.BlockSpec((ts, d), lambda i: (i, 0)),
            pl.BlockSpec((1, d), lambda i: (0, 0)),
            pl.BlockSpec((None, 1, d), lambda i: (i // tps, 0, 0)),
            pl.BlockSpec((None, 1, d), lambda i: (i // tps, 0, 0)),
        ],
        out_specs=pl.BlockSpec((ts, d), lambda i: (i, 0)),
        compiler_params=_cparams("parallel"),
        name="norm_mod",
    )(xf, nw.reshape(1, d), scale.reshape(b, 1, d), shift.reshape(b, 1, d))


def _matmul_acc(a_ref, w_ref, wb_ref):
    @pl.when(pl.program_id(1) == 0)
    def _():
        wb_ref[...] = w_ref[...].astype(BF16)

    return jnp.dot(a_ref[...], wb_ref[...], preferred_element_type=F32)


def _attn_inproj_kernel(h_ref, w_ref, cos_ref, sin_ref, nw_ref, o_ref, wb_ref, *,
                        n_qk_blocks, n_q_blocks, head_dim, q_scale):
    n = pl.program_id(0)
    acc = _matmul_acc(h_ref, w_ref, wb_ref)
    tn = acc.shape[1]

    @pl.when(n < n_qk_blocks)
    def _():
        cos = cos_ref[...]
        sin = sin_ref[...]
        scale = jnp.where(n < n_q_blocks, q_scale, 1.0).astype(F32)
        for hh in range(tn // head_dim):
            sl = slice(hh * head_dim, (hh + 1) * head_dim)
            xh = acc[:, sl]
            ms = jnp.mean(xh * xh, axis=-1, keepdims=True)
            y = xh * lax.rsqrt(ms + NORM_EPS) * nw_ref[:, sl]
            r = y * cos + pltpu.roll(y, head_dim // 2, axis=1) * sin
            o_ref[:, sl] = (r * scale).astype(BF16)

    @pl.when(n >= n_qk_blocks)
    def _():
        o_ref[...] = acc.astype(BF16)


def _attn_inproj(h, w_in, q_norm, k_norm, seq, tm=512, tn=1024):
    m, kdim = h.shape
    n_total = w_in.shape[1]
    width = n_total // 4
    hd = q_norm.shape[0]
    heads = width // hd
    tps = seq // tm
    half = hd // 2
    inv = ROPE_THETA ** (-jnp.arange(half, dtype=F32) / half)
    ang = jnp.arange(seq, dtype=F32)[:, None] * inv[None, :]
    cos = jnp.cos(ang)
    sin = jnp.sin(ang)
    cos_t = jnp.concatenate([cos, cos], axis=-1)
    sin_t = jnp.concatenate([-sin, sin], axis=-1)
    nw = jnp.concatenate([jnp.tile(q_norm, heads), jnp.tile(k_norm, heads)]).reshape(1, 2 * width)
    n_qk_blocks = 2 * width // tn
    kern = functools.partial(_attn_inproj_kernel, n_qk_blocks=n_qk_blocks, n_q_blocks=width // tn,
                             head_dim=hd, q_scale=float(hd) ** -0.5 * LOG2_E)
    return pl.pallas_call(
        kern,
        out_shape=jax.ShapeDtypeStruct((m, n_total), BF16),
        grid=(n_total // tn, m // tm),
        in_specs=[
            pl.BlockSpec((tm, kdim), lambda n, i: (i, 0)),
            pl.BlockSpec((kdim, tn), lambda n, i: (0, n)),
            pl.BlockSpec((tm, hd), lambda n, i: (i % tps, 0)),
            pl.BlockSpec((tm, hd), lambda n, i: (i % tps, 0)),
            pl.BlockSpec((1, tn), lambda n, i: (0, jnp.minimum(n, n_qk_blocks - 1))),
        ],
        out_specs=pl.BlockSpec((tm, tn), lambda n, i: (i, n)),
        scratch_shapes=[pltpu.VMEM((kdim, tn), BF16)],
        compiler_params=_cparams("parallel", "arbitrary"),
        name="attn_inproj",
    )(h, w_in, cos_t, sin_t, nw)


def _moba_kernel(pairs_ref, q_ref, k_ref, v_ref, z_ref, e_ref, u_ref,
                 qaugt_ref, kaug_ref, vt_ref, m_ref, l_ref, acc_ref, *, bs, nb, topk, hd, hp, pw, n_pairs):
    seq = nb * bs
    nt = (((1,), (1,)), ((), ()))

    for h in range(hp):
        hs = slice(h * hd, (h + 1) * hd)
        kf = k_ref[:, hs].astype(F32)
        kmean = jnp.sum(kf.reshape(nb, bs, hd), axis=1) * (1.0 / bs)
        g_t = lax.dot_general(kmean, q_ref[:, hs].astype(F32), nt, preferred_element_type=F32,
                              precision=HIGHEST)
        jj = lax.broadcasted_iota(jnp.int32, (nb, seq), 0)
        qb = lax.broadcasted_iota(jnp.int32, (nb, seq), 1) // bs
        rank = jnp.zeros((nb, seq), jnp.int32)
        for jp in range(nb):
            row = g_t[jp:jp + 1, :]
            beats = (row > g_t) | ((row == g_t) & (jp < jj))
            rank = rank + jnp.where(beats & (jp < qb), 1, 0)
        allowed = ((jj < qb) & (rank < topk)) | (jj == qb)
        bias_t = jnp.where(allowed, 0.0, MASK_BIAS).astype(BF16)
        qaugt_ref[h, hd:hd + nb, :] = bias_t
        qaugt_ref[h, hd + nb:hd + LANES, :] = jnp.zeros((LANES - nb, seq), BF16)
        kaug_ref[h, :, 0:hd] = k_ref[:, hs]
        kaug_ref[h, :, hd:hd + LANES] = e_ref[...]
        for jb in range(nb):
            rs = slice(jb * bs, (jb + 1) * bs)
            qaugt_ref[h, 0:hd, rs] = q_ref[rs, hs].astype(F32).T.astype(BF16)
            vt_ref[h, :, rs] = v_ref[rs, hs].astype(F32).T.astype(BF16)
    m_ref[...] = jnp.full(m_ref.shape, -jnp.inf, F32)
    l_ref[...] = jnp.zeros(l_ref.shape, F32)
    acc_ref[...] = jnp.zeros(acc_ref.shape, F32)

    key_id = lax.broadcasted_iota(jnp.int32, (bs, bs), 0)
    qry_id = lax.broadcasted_iota(jnp.int32, (bs, bs), 1)
    causal = key_id <= qry_id

    def units(pair, t):
        return [(h,) + pair(t * pw + u) for u in range(pw) for h in range(hp)]

    def stage_scores(us):
        out = []
        for h, i, j in us:
            q0 = pl.multiple_of(i * bs, bs)
            j0 = pl.multiple_of(j * bs, bs)
            out.append(jnp.dot(kaug_ref[h, pl.ds(j0, bs), :], qaugt_ref[h, :, pl.ds(q0, bs)],
                               preferred_element_type=F32))
        return tuple(out)

    def stage_softmax(us, ss, diag):
        out = []
        for (h, i, _), s in zip(us, ss):
            if diag:
                s = jnp.where(causal, s, -jnp.inf)
            m_old = m_ref[h, i]
            m_new = jnp.maximum(m_old, jnp.max(s, axis=0, keepdims=True))
            alpha = jnp.exp2(m_old - m_new)
            p = jnp.exp2(s - m_new)
            l_ref[h, i] = alpha * l_ref[h, i] + jnp.sum(p, axis=0, keepdims=True)
            m_ref[h, i] = m_new
            out += [p.astype(BF16), alpha]
        return tuple(out)

    def stage_pv(us, pa):
        for n, (h, i, j) in enumerate(us):
            j0 = pl.multiple_of(j * bs, bs)
            pv = jnp.dot(vt_ref[h, :, pl.ds(j0, bs)], pa[2 * n], preferred_element_type=F32)
            acc_ref[h, i] = pa[2 * n + 1] * acc_ref[h, i] + pv

    def run_pairs(pair, count, diag):
        steps = count // pw
        nu = pw * hp
        idle = []
        for _ in range(nu):
            idle += [jnp.zeros((bs, bs), BF16), jnp.ones((1, bs), F32)]

        def body(t, carry):
            ss, pa = carry[:nu], carry[nu:]
            ss_next = stage_scores(units(pair, t))
            pa_next = stage_softmax(units(pair, t - 1), ss, diag)
            stage_pv(units(pair, jnp.maximum(t - 2, 0)), pa)
            return tuple(ss_next) + tuple(pa_next)

        carry = lax.fori_loop(1, steps, body, stage_scores(units(pair, 0)) + tuple(idle))
        ss, pa = carry[:nu], carry[nu:]
        pa_last = stage_softmax(units(pair, steps - 1), ss, diag)
        stage_pv(units(pair, max(steps - 2, 0)), pa)
        stage_pv(units(pair, steps - 1), pa_last)

    run_pairs(lambda t: (t, t), nb, True)
    if n_pairs > 0:
        run_pairs(lambda t: (pairs_ref[0, t], pairs_ref[1, t]), n_pairs, False)

    def finalize(i, _):
        q0 = pl.multiple_of(i * bs, bs)
        for h in range(hp):
            hs = slice(h * hd, (h + 1) * hd)
            o = (acc_ref[h, i] / l_ref[h, i]).T
            z = z_ref[pl.ds(q0, bs), hs].astype(F32)
            u_ref[pl.ds(q0, bs), hs] = (o * (z * _sigmoid(z))).astype(BF16)
        return 0

    lax.fori_loop(0, nb, finalize, 0)


def _moba_attention(qkvz, batch, seq, heads, hd, hp=1):
    m = qkvz.shape[0]
    bs = MOBA_BLOCK
    nb = seq // bs
    width = heads * hd
    groups = heads // hp
    e_tab = (jnp.arange(seq)[:, None] // bs == jnp.arange(LANES)[None, :]).astype(BF16)
    pair_list = [(i, j) for j in range(nb) for i in range(j + 1, nb)]
    n_pairs = len(pair_list)
    pairs = jnp.asarray(pair_list or [(0, 0)], jnp.int32).T
    pw = next(w for w in (4, 2, 1) if nb % w == 0 and n_pairs % w == 0)
    kern = functools.partial(_moba_kernel, bs=bs, nb=nb, topk=min(MOBA_TOPK, nb), hd=hd, hp=hp, pw=pw,
                             n_pairs=n_pairs)
    blk = (seq, hp * hd)
    grid_spec = pltpu.PrefetchScalarGridSpec(
        num_scalar_prefetch=1,
        grid=(batch, groups),
        in_specs=[
            pl.BlockSpec(blk, lambda b, g, pr: (b, g)),
            pl.BlockSpec(blk, lambda b, g, pr: (b, groups + g)),
            pl.BlockSpec(blk, lambda b, g, pr: (b, 2 * groups + g)),
            pl.BlockSpec(blk, lambda b, g, pr: (b, 3 * groups + g)),
            pl.BlockSpec((seq, LANES), lambda b, g, pr: (0, 0)),
        ],
        out_specs=pl.BlockSpec(blk, lambda b, g, pr: (b, g)),
        scratch_shapes=[
            pltpu.VMEM((hp, hd + LANES, seq), BF16),
            pltpu.VMEM((hp, seq, hd + LANES), BF16),
            pltpu.VMEM((hp, hd, seq), BF16),
            pltpu.VMEM((hp, nb, 1, bs), F32),
            pltpu.VMEM((hp, nb, 1, bs), F32),
            pltpu.VMEM((hp, nb, hd, bs), F32),
        ],
    )
    return pl.pallas_call(
        kern,
        out_shape=jax.ShapeDtypeStruct((m, width), BF16),
        grid_spec=grid_spec,
        compiler_params=_cparams("parallel", "parallel"),
        name="moba_attn",
    )(pairs, qkvz, qkvz, qkvz, qkvz, e_tab)


def _outproj_kernel(u_ref, w_ref, x_ref, g_ref, o_ref, wb_ref):
    acc = _matmul_acc(u_ref, w_ref, wb_ref)
    o_ref[...] = x_ref[...] + g_ref[...] * acc


def _outproj(u, w_out, xf, gate, seq, tm=512, tn=512):
    m, kdim = u.shape
    d = w_out.shape[1]
    b = m // seq
    tps = seq // tm
    return pl.pallas_call(
        _outproj_kernel,
        out_shape=jax.ShapeDtypeStruct((m, d), F32),
        grid=(d // tn, m // tm),
        in_specs=[
            pl.BlockSpec((tm, kdim), lambda n, i: (i, 0)),
            pl.BlockSpec((kdim, tn), lambda n, i: (0, n)),
            pl.BlockSpec((tm, tn), lambda n, i: (i, n)),
            pl.BlockSpec((None, 1, tn), lambda n, i: (i // tps, 0, n)),
        ],
        out_specs=pl.BlockSpec((tm, tn), lambda n, i: (i, n)),
        scratch_shapes=[pltpu.VMEM((kdim, tn), BF16)],
        compiler_params=_cparams("parallel", "arbitrary"),
        name="outproj",
    )(u, w_out, xf, gate.reshape(b, 1, d))


def _mlstm_inproj_kernel(h_ref, w_ref, cw_ref, o_ref, wb_ref, ext_ref, *,
                         n_conv_blocks, n_q_blocks, tiles_per_seq, conv_k, k_scale):
    n = pl.program_id(0)
    i = pl.program_id(1)
    acc = _matmul_acc(h_ref, w_ref, wb_ref)
    tm = acc.shape[0]

    @pl.when(n < n_conv_blocks)
    def _():
        @pl.when(i % tiles_per_seq == 0)
        def _():
            ext_ref[0:8, :] = jnp.zeros((8, acc.shape[1]), F32)

        @pl.when(i % tiles_per_seq != 0)
        def _():
            ext_ref[0:8, :] = ext_ref[tm:tm + 8, :]

        ext_ref[8:tm + 8, :] = acc
        y = cw_ref[conv_k - 1:conv_k, :] * acc
        for back in range(1, conv_k):
            y = y + cw_ref[conv_k - 1 - back:conv_k - back, :] * ext_ref[8 - back:8 - back + tm, :]
        y = y * _sigmoid(y)
        scale = jnp.where(n >= n_q_blocks, k_scale, 1.0).astype(F32)
        o_ref[...] = (y * scale).astype(BF16)

    @pl.when(n >= n_conv_blocks)
    def _():
        o_ref[...] = acc.astype(BF16)


def _mlstm_inproj(h, w_in, conv_w, n_main, qk_width, dk, seq, tm=512, tn=1024):
    m, kdim = h.shape
    conv_k = conv_w.shape[0]
    tps = seq // tm
    n_conv_blocks = 2 * qk_width // tn
    kern = functools.partial(_mlstm_inproj_kernel, n_conv_blocks=n_conv_blocks, n_q_blocks=qk_width // tn,
                             tiles_per_seq=tps, conv_k=conv_k, k_scale=float(dk) ** -0.5)
    return pl.pallas_call(
        kern,
        out_shape=jax.ShapeDtypeStruct((m, n_main), BF16),
        grid=(n_main // tn, m // tm),
        in_specs=[
            pl.BlockSpec((tm, kdim), lambda n, i: (i, 0)),
            pl.BlockSpec((kdim, tn), lambda n, i: (0, n)),
            pl.BlockSpec((conv_k, tn), lambda n, i: (0, jnp.minimum(n, n_conv_blocks - 1))),
        ],
        out_specs=pl.BlockSpec((tm, tn), lambda n, i: (i, n)),
        scratch_shapes=[pltpu.VMEM((kdim, tn), BF16), pltpu.VMEM((tm + 8, tn), F32)],
        compiler_params=_cparams("arbitrary", "arbitrary"),
        name="mlstm_inproj",
    )(h, w_in, conv_w)


def _gate_proj_kernel(h_ref, w_ref, b_ref, o_ref):
    o_ref[...] = jnp.dot(h_ref[...], w_ref[...].astype(BF16), preferred_element_type=F32) + b_ref[...]


def _gate_proj(h, w_gate, gate_b, tm=1024):
    m, kdim = h.shape
    ng = w_gate.shape[1]
    w_pad = jnp.zeros((kdim, LANES), F32).at[:, :ng].set(w_gate)
    b_pad = jnp.zeros((1, LANES), F32).at[0, :ng].set(gate_b)
    out = pl.pallas_call(
        _gate_proj_kernel,
        out_shape=jax.ShapeDtypeStruct((m, LANES), F32),
        grid=(m // tm,),
        in_specs=[
            pl.BlockSpec((tm, kdim), lambda i: (i, 0)),
            pl.BlockSpec((kdim, LANES), lambda i: (0, 0)),
            pl.BlockSpec((1, LANES), lambda i: (0, 0)),
        ],
        out_specs=pl.BlockSpec((tm, LANES), lambda i: (i, 0)),
        compiler_params=_cparams("parallel"),
        name="mlstm_gate_proj",
    )(h, w_pad, b_pad)
    return out[:, :ng]


def _mlstm_kernel(q_ref, k_ref, v_ref, o_ref, z_ref, g_ref, nw_ref, u_ref, c_ref, m_ref, *,
                  chunk, heads, dv):
    h = pl.program_id(1)
    c = pl.program_id(2)
    L = chunk

    @pl.when(c == 0)
    def _():
        c_ref[...] = jnp.zeros(c_ref.shape, F32)
        m_ref[...] = jnp.zeros(m_ref.shape, F32)

    q = q_ref[...]
    k = k_ref[...]
    v = v_ref[...]
    i_row = g_ref[pl.ds(h, 1), :]
    f_row = g_ref[pl.ds(heads + h, 1), :]
    lf_row = jnp.minimum(f_row, 0.0) - jnp.log1p(jnp.exp(-jnp.abs(f_row)))

    t_id = lax.broadcasted_iota(jnp.int32, (L, L), 0)
    s_id = lax.broadcasted_iota(jnp.int32, (L, L), 1)
    causal = s_id <= t_id
    tri = jnp.where(causal, 1.0, 0.0).astype(F32)
    eye = jnp.where(s_id == t_id, 1.0, 0.0).astype(F32)
    rid = lax.broadcasted_iota(jnp.int32, (8, L), 0)
    rows8 = jnp.where(rid == 0, i_row, jnp.where(rid == 1, lf_row, 0.0))
    nt = (((1,), (1,)), ((), ()))
    b_row = lax.dot_general(rows8, tri, nt, preferred_element_type=F32, precision=HIGHEST)[1:2, :]
    cols_c = lax.dot_general(tri, rows8, nt, preferred_element_type=F32, precision=HIGHEST)
    cols_i = lax.dot_general(eye, rows8, nt, preferred_element_type=F32, precision=HIGHEST)
    b_col = cols_c[:, 1:2]
    i_col = cols_i[:, 0:1]

    m_prev = m_ref[...]
    log_d = jnp.where(causal, b_col - b_row + i_row, -jnp.inf)
    log_inter = b_col + m_prev
    m_t = jnp.maximum(log_inter, jnp.max(log_d, axis=1, keepdims=True))
    d_mat = jnp.exp(log_d - m_t)
    w_inter = jnp.exp(log_inter - m_t)

    s_qk = lax.dot_general(q, k, nt, preferred_element_type=F32) * d_mat
    lane = lax.broadcasted_iota(jnp.int32, (L, LANES), 1)
    ones_blk = jnp.where(lane == 0, 1.0, 0.0).astype(BF16)
    v_aug = jnp.concatenate([v, ones_blk], axis=1)
    c_state = c_ref[...]
    num = (w_inter * jnp.dot(q, c_state.astype(BF16), preferred_element_type=F32)
           + jnp.dot(s_qk.astype(BF16), v_aug, preferred_element_type=F32))
    den = num[:, dv:dv + 1]
    hh = num[:, :dv] / jnp.maximum(jnp.abs(den), jnp.exp(-m_t))
    ms = jnp.mean(hh * hh, axis=-1, keepdims=True)
    hn = hh * lax.rsqrt(ms + NORM_EPS) * nw_ref[...]
    og = o_ref[...].astype(F32)
    z = z_ref[...].astype(F32)
    u_ref[...] = (hn * _sigmoid(og) * (z * _sigmoid(z))).astype(BF16)

    b_last = b_row[:, L - 1:L]
    log_s_row = b_last - b_row + i_row
    m_new = jnp.maximum(b_last + m_prev, jnp.max(log_s_row, axis=1, keepdims=True))
    decay = jnp.exp(b_last + m_prev - m_new)
    ws_col = jnp.exp(b_last - b_col + i_col - m_new)
    wv = (ws_col * v_aug.astype(F32)).astype(BF16)
    upd = lax.dot_general(k, wv, (((0,), (0,)), ((), ())), preferred_element_type=F32)
    c_ref[...] = decay * c_state + upd
    m_ref[...] = m_new


def _mlstm_core(proj, gates_t, norm_w, batch, seq, heads, dk, dv):
    m = proj.shape[0]
    L = MLSTM_CHUNK
    nc = seq // L
    v_width = heads * dv
    qk_blocks = heads
    v_off = 2 * heads * dk // dv
    o_off = v_off + heads
    z_off = o_off + heads
    kern = functools.partial(_mlstm_kernel, chunk=L, heads=heads, dv=dv)
    return pl.pallas_call(
        kern,
        out_shape=jax.ShapeDtypeStruct((m, v_width), BF16),
        grid=(batch, heads, nc),
        in_specs=[
            pl.BlockSpec((L, dk), lambda b, h, c: (b * nc + c, h)),
            pl.BlockSpec((L, dk), lambda b, h, c: (b * nc + c, qk_blocks + h)),
            pl.BlockSpec((L, dv), lambda b, h, c: (b * nc + c, v_off + h)),
            pl.BlockSpec((L, dv), lambda b, h, c: (b * nc + c, o_off + h)),
            pl.BlockSpec((L, dv), lambda b, h, c: (b * nc + c, z_off + h)),
            pl.BlockSpec((None, 2 * heads, L), lambda b, h, c: (b, 0, c)),
            pl.BlockSpec((1, dv), lambda b, h, c: (0, h)),
        ],
        out_specs=pl.BlockSpec((L, dv), lambda b, h, c: (b * nc + c, h)),
        scratch_shapes=[pltpu.VMEM((dk, dv + LANES), F32), pltpu.VMEM((1, 1), F32)],
        compiler_params=_cparams("parallel", "parallel", "arbitrary"),
        name="mlstm_core",
    )(proj, proj, proj, proj, proj, gates_t, norm_w.reshape(1, v_width))


def _moba_layer(xf, h, gate, w_in, q_norm, k_norm, w_out, batch, seq):
    hd = q_norm.shape[0]
    heads = w_in.shape[1] // 4 // hd
    qkvz = _attn_inproj(h, w_in, q_norm, k_norm, seq)
    u = _moba_attention(qkvz, batch, seq, heads, hd)
    return _outproj(u, w_out, xf, gate, seq)


def _mlstm_layer(xf, h, gate, w_in, gate_b, conv_w, norm_w, w_out, batch, seq):
    heads = gate_b.shape[0] // 2
    v_width = w_out.shape[0]
    dv = v_width // heads
    qk_width = conv_w.shape[1] // 2
    dk = qk_width // heads
    n_main = 2 * qk_width + 3 * v_width
    proj = _mlstm_inproj(h, w_in, conv_w, n_main, qk_width, dk, seq)
    gates = _gate_proj(h, w_in[:, n_main:], gate_b)
    gates_t = gates.reshape(batch, seq, 2 * heads).transpose(0, 2, 1)
    u = _mlstm_core(proj, gates_t, norm_w, batch, seq, heads, dk, dv)
    return _outproj(u, w_out, xf, gate, seq)


def kernel(x, c, norm_w, ada_w, ada_b, att_w_in, att_q_norm, att_k_norm, att_w_out, mlstm_w_in,
           mlstm_gate_b, mlstm_conv_w, mlstm_norm_w, mlstm_w_out):
    batch, seq, d = x.shape
    depth = norm_w.shape[0]
    mod = _adaln(c, ada_w, ada_b)
    xf = x.reshape(batch * seq, d)
    for layer in range(depth):
        shift, scale, gate = mod[layer, :, :d], mod[layer, :, d:2 * d], mod[layer, :, 2 * d:]
        h = _norm_mod(xf, norm_w[layer], scale, shift, seq)
        j = layer // 2
        if layer % 2 == 0:
            xf = _moba_layer(xf, h, gate, att_w_in[j], att_q_norm[j], att_k_norm[j], att_w_out[j], batch, seq)
        else:
            xf = _mlstm_layer(xf, h, gate, mlstm_w_in[j], mlstm_gate_b[j], mlstm_conv_w[j],
                              mlstm_norm_w[j], mlstm_w_out[j], batch, seq)
    return xf.reshape(batch, seq, d)
```

```python
import functools

import jax
import jax.numpy as jnp
from jax import lax
from jax.experimental import pallas as pl
from jax.experimental.pallas import tpu as pltpu

F32 = jnp.float32
BF16 = jnp.bfloat16
HIGHEST = lax.Precision.HIGHEST

NORM_EPS = 1e-6
MOBA_BLOCK = 256
MOBA_TOPK = 3
ROPE_THETA = 10000.0
MLSTM_CHUNK = 256
MASK_BIAS = -1e30
LANES = 128
LOG2_E = 1.4426950408889634

_VMEM_LIMIT = 48 * 1024 * 1024


def _sigmoid(x):
    return 1.0 / (1.0 + jnp.exp(-x))


def _cparams(*sem):
    return pltpu.CompilerParams(dimension_semantics=sem, vmem_limit_bytes=_VMEM_LIMIT)


def _adaln_kernel(c_ref, w_ref, b_ref, o_ref):
    c = c_ref[...]
    cond = c * _sigmoid(c)
    o_ref[...] = jnp.dot(cond, w_ref[...], preferred_element_type=F32, precision=HIGHEST) + b_ref[...]


def _adaln(c, ada_w, ada_b, tn=768):
    depth, d, n3 = ada_w.shape
    b = c.shape[0]
    rows = 8
    c_pad = jnp.zeros((rows, d), F32).at[:b].set(c)
    out = pl.pallas_call(
        _adaln_kernel,
        out_shape=jax.ShapeDtypeStruct((depth, rows, n3), F32),
        grid=(depth, n3 // tn),
        in_specs=[
            pl.BlockSpec((rows, d), lambda l, n: (0, 0)),
            pl.BlockSpec((None, d, tn), lambda l, n: (l, 0, n)),
            pl.BlockSpec((None, 1, tn), lambda l, n: (l, 0, n)),
        ],
        out_specs=pl.BlockSpec((None, rows, tn), lambda l, n: (l, 0, n)),
        compiler_params=_cparams("parallel", "parallel"),
        name="adaln_mod",
    )(c_pad, ada_w, ada_b.reshape(depth, 1, n3))
    return out[:, :b]


def _norm_mod_kernel(x_ref, nw_ref, sc_ref, sh_ref, o_ref):
    x = x_ref[...]
    ms = jnp.mean(x * x, axis=-1, keepdims=True)
    y = x * lax.rsqrt(ms + NORM_EPS) * nw_ref[...]
    o_ref[...] = (y * (1.0 + sc_ref[...]) + sh_ref[...]).astype(BF16)


def _norm_mod(xf, nw, scale, shift, seq, ts=512):
    m, d = xf.shape
    b = m // seq
    tps = seq // ts
    return pl.pallas_call(
        _norm_mod_kernel,
        out_shape=jax.ShapeDtypeStruct((m, d), BF16),
        grid=(m // ts,),
        in_specs=[
            pl.BlockSpec((ts, d), lambda i: (i, 0)),
            pl.BlockSpec((1, d), lambda i: (0, 0)),
            pl.BlockSpec((None, 1, d), lambda i: (i // tps, 0, 0)),
            pl.BlockSpec((None, 1, d), lambda i: (i // tps, 0, 0)),
        ],
        out_specs=pl.BlockSpec((ts, d), lambda i: (i, 0)),
        compiler_params=_cparams("parallel"),
        name="norm_mod",
    )(xf, nw.reshape(1, d), scale.reshape(b, 1, d), shift.reshape(b, 1, d))


def _matmul_acc(a_ref, w_ref, wb_ref):
    @pl.when(pl.program_id(1) == 0)
    def _():
        wb_ref[...] = w_ref[...].astype(BF16)

    return jnp.dot(a_ref[...], wb_ref[...], preferred_element_type=F32)


def _attn_inproj_kernel(h_ref, w_ref, cos_ref, sin_ref, nw_ref, o_ref, wb_ref, *,
                        n_qk_blocks, n_q_blocks, head_dim, q_scale):
    n = pl.program_id(0)
    acc = _matmul_acc(h_ref, w_ref, wb_ref)
    tn = acc.shape[1]

    @pl.when(n < n_qk_blocks)
    def _():
        cos = cos_ref[...]
        sin = sin_ref[...]
        scale = jnp.where(n < n_q_blocks, q_scale, 1.0).astype(F32)
        for hh in range(tn // head_dim):
            sl = slice(hh * head_dim, (hh + 1) * head_dim)
            xh = acc[:, sl]
            ms = jnp.mean(xh * xh, axis=-1, keepdims=True)
            y = xh * lax.rsqrt(ms + NORM_EPS) * nw_ref[:, sl]
            r = y * cos + pltpu.roll(y, head_dim // 2, axis=1) * sin
            o_ref[:, sl] = (r * scale).astype(BF16)

    @pl.when(n >= n_qk_blocks)
    def _():
        o_ref[...] = acc.astype(BF16)


def _attn_inproj(h, w_in, q_norm, k_norm, seq, tm=512, tn=1024):
    m, kdim = h.shape
    n_total = w_in.shape[1]
    width = n_total // 4
    hd = q_norm.shape[0]
    heads = width // hd
    tps = seq // tm
    half = hd // 2
    inv = ROPE_THETA ** (-jnp.arange(half, dtype=F32) / half)
    ang = jnp.arange(seq, dtype=F32)[:, None] * inv[None, :]
    cos = jnp.cos(ang)
    sin = jnp.sin(ang)
    cos_t = jnp.concatenate([cos, cos], axis=-1)
    sin_t = jnp.concatenate([-sin, sin], axis=-1)
    nw = jnp.concatenate([jnp.tile(q_norm, heads), jnp.tile(k_norm, heads)]).reshape(1, 2 * width)
    n_qk_blocks = 2 * width // tn
    kern = functools.partial(_attn_inproj_kernel, n_qk_blocks=n_qk_blocks, n_q_blocks=width // tn,
                             head_dim=hd, q_scale=float(hd) ** -0.5 * LOG2_E)
    return pl.pallas_call(
        kern,
        out_shape=jax.ShapeDtypeStruct((m, n_total), BF16),
        grid=(n_total // tn, m // tm),
        in_specs=[
            pl.BlockSpec((tm, kdim), lambda n, i: (i, 0)),
            pl.BlockSpec((kdim, tn), lambda n, i: (0, n)),
            pl.BlockSpec((tm, hd), lambda n, i: (i % tps, 0)),
            pl.BlockSpec((tm, hd), lambda n, i: (i % tps, 0)),
            pl.BlockSpec((1, tn), lambda n, i: (0, jnp.minimum(n, n_qk_blocks - 1))),
        ],
        out_specs=pl.BlockSpec((tm, tn), lambda n, i: (i, n)),
        scratch_shapes=[pltpu.VMEM((kdim, tn), BF16)],
        compiler_params=_cparams("parallel", "arbitrary"),
        name="attn_inproj",
    )(h, w_in, cos_t, sin_t, nw)


def _moba_kernel(pairs_ref, q_ref, k_ref, v_ref, z_ref, e_ref, u_ref,
                 qaugt_ref, kaug_ref, vt_ref, m_ref, l_ref, acc_ref, *, bs, nb, topk, hd, hp, pw, n_pairs):
    seq = nb * bs
    nt = (((1,), (1,)), ((), ()))

    for h in range(hp):
        hs = slice(h * hd, (h + 1) * hd)
        kf = k_ref[:, hs].astype(F32)
        kmean = jnp.sum(kf.reshape(nb, bs, hd), axis=1) * (1.0 / bs)
        g_t = lax.dot_general(kmean, q_ref[:, hs].astype(F32), nt, preferred_element_type=F32,
                              precision=HIGHEST)
        jj = lax.broadcasted_iota(jnp.int32, (nb, seq), 0)
        qb = lax.broadcasted_iota(jnp.int32, (nb, seq), 1) // bs
        rank = jnp.zeros((nb, seq), jnp.int32)
        for jp in range(nb):
            row = g_t[jp:jp + 1, :]
            beats = (row > g_t) | ((row == g_t) & (jp < jj))
            rank = rank + jnp.where(beats & (jp < qb), 1, 0)
        allowed = ((jj < qb) & (rank < topk)) | (jj == qb)
        bias_t = jnp.where(allowed, 0.0, MASK_BIAS).astype(BF16)
        qaugt_ref[h, hd:hd + nb, :] = bias_t
        qaugt_ref[h, hd + nb:hd + LANES, :] = jnp.zeros((LANES - nb, seq), BF16)
        kaug_ref[h, :, 0:hd] = k_ref[:, hs]
        kaug_ref[h, :, hd:hd + LANES] = e_ref[...]
        for jb in range(nb):
            rs = slice(jb * bs, (jb + 1) * bs)
            qaugt_ref[h, 0:hd, rs] = q_ref[rs, hs].astype(F32).T.astype(BF16)
            vt_ref[h, :, rs] = v_ref[rs, hs].astype(F32).T.astype(BF16)
    m_ref[...] = jnp.full(m_ref.shape, -jnp.inf, F32)
    l_ref[...] = jnp.zeros(l_ref.shape, F32)
    acc_ref[...] = jnp.zeros(acc_ref.shape, F32)

    key_id = lax.broadcasted_iota(jnp.int32, (bs, bs), 0)
    qry_id = lax.broadcasted_iota(jnp.int32, (bs, bs), 1)
    causal = key_id <= qry_id

    def units(pair, t):
        return [(h,) + pair(t * pw + u) for u in range(pw) for h in range(hp)]

    def stage_scores(us):
        out = []
        for h, i, j in us:
            q0 = pl.multiple_of(i * bs, bs)
            j0 = pl.multiple_of(j * bs, bs)
            out.append(jnp.dot(kaug_ref[h, pl.ds(j0, bs), :], qaugt_ref[h, :, pl.ds(q0, bs)],
                               preferred_element_type=F32))
        return tuple(out)

    def stage_softmax(us, ss, diag):
        out = []
        for (h, i, _), s in zip(us, ss):
            if diag:
                s = jnp.where(causal, s, -jnp.inf)
            m_old = m_ref[h, i]
            m_new = jnp.maximum(m_old, jnp.max(s, axis=0, keepdims=True))
            alpha = jnp.exp2(m_old - m_new)
            p = jnp.exp2(s - m_new)
            l_ref[h, i] = alpha * l_ref[h, i] + jnp.sum(p, axis=0, keepdims=True)
            m_ref[h, i] = m_new
            out += [p.astype(BF16), alpha]
        return tuple(out)

    def stage_pv(us, pa):
        for n, (h, i, j) in enumerate(us):
            j0 = pl.multiple_of(j * bs, bs)
            pv = jnp.dot(vt_ref[h, :, pl.ds(j0, bs)], pa[2 * n], preferred_element_type=F32)
            acc_ref[h, i] = pa[2 * n + 1] * acc_ref[h, i] + pv

    def run_pairs(pair, count, diag):
        steps = count // pw
        nu = pw * hp
        idle = []
        for _ in range(nu):
            idle += [jnp.zeros((bs, bs), BF16), jnp.ones((1, bs), F32)]

        def body(t, carry):
            ss, pa = carry[:nu], carry[nu:]
            ss_next = stage_scores(units(pair, t))
            pa_next = stage_softmax(units(pair, t - 1), ss, diag)
            stage_pv(units(pair, jnp.maximum(t - 2, 0)), pa)
            return tuple(ss_next) + tuple(pa_next)

        carry = lax.fori_loop(1, steps, body, stage_scores(units(pair, 0)) + tuple(idle))
        ss, pa = carry[:nu], carry[nu:]
        pa_last = stage_softmax(units(pair, steps - 1), ss, diag)
        stage_pv(units(pair, max(steps - 2, 0)), pa)
        stage_pv(units(pair, steps - 1), pa_last)

    run_pairs(lambda t: (t, t), nb, True)
    if n_pairs > 0:
        run_pairs(lambda t: (pairs_ref[0, t], pairs_ref[1, t]), n_pairs, False)

    def finalize(i, _):
        q0 = pl.multiple_of(i * bs, bs)
        for h in range(hp):
            hs = slice(h * hd, (h + 1) * hd)
            o = (acc_ref[h, i] / l_ref[h, i]).T
            z = z_ref[pl.ds(q0, bs), hs].astype(F32)
            u_ref[pl.ds(q0, bs), hs] = (o * (z * _sigmoid(z))).astype(BF16)
        return 0

    lax.fori_loop(0, nb, finalize, 0)


def _moba_attention(qkvz, batch, seq, heads, hd, hp=1):
    m = qkvz.shape[0]
    bs = MOBA_BLOCK
    nb = seq // bs
    width = heads * hd
    groups = heads // hp
    e_tab = (jnp.arange(seq)[:, None] // bs == jnp.arange(LANES)[None, :]).astype(BF16)
    pair_list = [(i, j) for j in range(nb) for i in range(j + 1, nb)]
    n_pairs = len(pair_list)
    pairs = jnp.asarray(pair_list or [(0, 0)], jnp.int32).T
    pw = next(w for w in (4, 2, 1) if nb % w == 0 and n_pairs % w == 0)
    kern = functools.partial(_moba_kernel, bs=bs, nb=nb, topk=min(MOBA_TOPK, nb), hd=hd, hp=hp, pw=pw,
                             n_pairs=n_pairs)
    blk = (seq, hp * hd)
    grid_spec = pltpu.PrefetchScalarGridSpec(
        num_scalar_prefetch=1,
        grid=(batch, groups),
        in_specs=[
            pl.BlockSpec(blk, lambda b, g, pr: (b, g)),
            pl.BlockSpec(blk, lambda b, g, pr: (b, groups + g)),
            pl.BlockSpec(blk, lambda b, g, pr: (b, 2 * groups + g)),
            pl.BlockSpec(blk, lambda b, g, pr: (b, 3 * groups + g)),
            pl.BlockSpec((seq, LANES), lambda b, g, pr: (0, 0)),
        ],
        out_specs=pl.BlockSpec(blk, lambda b, g, pr: (b, g)),
        scratch_shapes=[
            pltpu.VMEM((hp, hd + LANES, seq), BF16),
            pltpu.VMEM((hp, seq, hd + LANES), BF16),
            pltpu.VMEM((hp, hd, seq), BF16),
            pltpu.VMEM((hp, nb, 1, bs), F32),
            pltpu.VMEM((hp, nb, 1, bs), F32),
            pltpu.VMEM((hp, nb, hd, bs), F32),
        ],
    )
    return pl.pallas_call(
        kern,
        out_shape=jax.ShapeDtypeStruct((m, width), BF16),
        grid_spec=grid_spec,
        compiler_params=_cparams("parallel", "parallel"),
        name="moba_attn",
    )(pairs, qkvz, qkvz, qkvz, qkvz, e_tab)


def _outproj_kernel(u_ref, w_ref, x_ref, g_ref, o_ref, wb_ref):
    acc = _matmul_acc(u_ref, w_ref, wb_ref)
    o_ref[...] = x_ref[...] + g_ref[...] * acc


def _outproj(u, w_out, xf, gate, seq, tm=512, tn=512):
    m, kdim = u.shape
    d = w_out.shape[1]
    b = m // seq
    tps = seq // tm
    return pl.pallas_call(
        _outproj_kernel,
        out_shape=jax.ShapeDtypeStruct((m, d), F32),
        grid=(d // tn, m // tm),
        in_specs=[
            pl.BlockSpec((tm, kdim), lambda n, i: (i, 0)),
            pl.BlockSpec((kdim, tn), lambda n, i: (0, n)),
            pl.BlockSpec((tm, tn), lambda n, i: (i, n)),
            pl.BlockSpec((None, 1, tn), lambda n, i: (i // tps, 0, n)),
        ],
        out_specs=pl.BlockSpec((tm, tn), lambda n, i: (i, n)),
        scratch_shapes=[pltpu.VMEM((kdim, tn), BF16)],
        compiler_params=_cparams("parallel", "arbitrary"),
        name="outproj",
    )(u, w_out, xf, gate.reshape(b, 1, d))


def _mlstm_inproj_kernel(h_ref, w_ref, cw_ref, o_ref, wb_ref, ext_ref, *,
                         n_conv_blocks, n_q_blocks, n_o_start, n_z_start, tiles_per_seq, conv_k, k_scale):
    n = pl.program_id(0)
    i = pl.program_id(1)
    acc = _matmul_acc(h_ref, w_ref, wb_ref)
    tm = acc.shape[0]

    @pl.when(n < n_conv_blocks)
    def _():
        @pl.when(i % tiles_per_seq == 0)
        def _():
            ext_ref[0:8, :] = jnp.zeros((8, acc.shape[1]), F32)

        @pl.when(i % tiles_per_seq != 0)
        def _():
            ext_ref[0:8, :] = ext_ref[tm:tm + 8, :]

        ext_ref[8:tm + 8, :] = acc
        y = cw_ref[conv_k - 1:conv_k, :] * acc
        for back in range(1, conv_k):
            y = y + cw_ref[conv_k - 1 - back:conv_k - back, :] * ext_ref[8 - back:8 - back + tm, :]
        y = y * _sigmoid(y)
        scale = jnp.where(n >= n_q_blocks, k_scale, 1.0).astype(F32)
        o_ref[...] = (y * scale).astype(BF16)

    @pl.when((n >= n_conv_blocks) & (n < n_o_start))
    def _():
        o_ref[...] = acc.astype(BF16)

    @pl.when((n >= n_o_start) & (n < n_z_start))
    def _():
        o_ref[...] = _sigmoid(acc).astype(BF16)

    @pl.when(n >= n_z_start)
    def _():
        o_ref[...] = (acc * _sigmoid(acc)).astype(BF16)


def _mlstm_inproj(h, w_in, conv_w, n_main, qk_width, v_width, dk, seq, tm=512, tn=1024):
    m, kdim = h.shape
    conv_k = conv_w.shape[0]
    tps = seq // tm
    n_conv_blocks = 2 * qk_width // tn
    n_o_start = n_conv_blocks + v_width // tn
    n_z_start = n_o_start + v_width // tn
    kern = functools.partial(_mlstm_inproj_kernel, n_conv_blocks=n_conv_blocks, n_q_blocks=qk_width // tn,
                             n_o_start=n_o_start, n_z_start=n_z_start,
                             tiles_per_seq=tps, conv_k=conv_k, k_scale=float(dk) ** -0.5)
    return pl.pallas_call(
        kern,
        out_shape=jax.ShapeDtypeStruct((m, n_main), BF16),
        grid=(n_main // tn, m // tm),
        in_specs=[
            pl.BlockSpec((tm, kdim), lambda n, i: (i, 0)),
            pl.BlockSpec((kdim, tn), lambda n, i: (0, n)),
            pl.BlockSpec((conv_k, tn), lambda n, i: (0, jnp.minimum(n, n_conv_blocks - 1))),
        ],
        out_specs=pl.BlockSpec((tm, tn), lambda n, i: (i, n)),
        scratch_shapes=[pltpu.VMEM((kdim, tn), BF16), pltpu.VMEM((tm + 8, tn), F32)],
        compiler_params=_cparams("arbitrary", "arbitrary"),
        name="mlstm_inproj",
    )(h, w_in, conv_w)


def _gate_proj_kernel(h_ref, w_ref, b_ref, rows_ref, cols_ref, *, heads, chunk):
    g = jnp.dot(h_ref[...], w_ref[...].astype(BF16), preferred_element_type=F32) + b_ref[...]
    tm = g.shape[0]
    g_t = g.T
    i_rows = g_t[0:heads, :]
    f_rows = g_t[heads:2 * heads, :]
    b = jnp.minimum(f_rows, 0.0) - jnp.log1p(jnp.exp(-jnp.abs(f_rows)))
    pos = lax.broadcasted_iota(jnp.int32, b.shape, 1) % chunk
    shift = 1
    while shift < chunk:
        b = b + jnp.where(pos >= shift, pltpu.roll(b, shift, axis=1), 0.0)
        shift *= 2
    rows = jnp.concatenate([i_rows, b], axis=0)
    rows_ref[...] = rows
    cols_ref[...] = jnp.concatenate([rows, jnp.zeros((LANES - 2 * heads, tm), F32)], axis=0).T


def _gate_proj(h, w_gate, gate_b, batch, seq, tm=1024):
    m, kdim = h.shape
    ng = w_gate.shape[1]
    heads = ng // 2
    tps = seq // tm
    w_pad = jnp.zeros((kdim, LANES), F32).at[:, :ng].set(w_gate)
    b_pad = jnp.zeros((1, LANES), F32).at[0, :ng].set(gate_b)
    kern = functools.partial(_gate_proj_kernel, heads=heads, chunk=MLSTM_CHUNK)
    return pl.pallas_call(
        kern,
        out_shape=(jax.ShapeDtypeStruct((batch, ng, seq), F32), jax.ShapeDtypeStruct((m, LANES), F32)),
        grid=(m // tm,),
        in_specs=[
            pl.BlockSpec((tm, kdim), lambda i: (i, 0)),
            pl.BlockSpec((kdim, LANES), lambda i: (0, 0)),
            pl.BlockSpec((1, LANES), lambda i: (0, 0)),
        ],
        out_specs=(pl.BlockSpec((None, ng, tm), lambda i: (i // tps, 0, i % tps)),
                   pl.BlockSpec((tm, LANES), lambda i: (i, 0))),
        compiler_params=_cparams("parallel"),
        name="mlstm_gate_proj",
    )(h, w_pad, b_pad)


def _mlstm_kernel(q_ref, k_ref, v_ref, o_ref, z_ref, rows_ref, cols_ref, nw_ref, u_ref, c_ref, m_ref, *,
                  chunk, heads, hp, dk, dv):
    g = pl.program_id(1)
    c = pl.program_id(2)
    L = chunk
    nt = (((1,), (1,)), ((), ()))

    @pl.when(c == 0)
    def _():
        c_ref[...] = jnp.zeros(c_ref.shape, F32)
        m_ref[...] = jnp.zeros(m_ref.shape, F32)

    t_id = lax.broadcasted_iota(jnp.int32, (L, L), 0)
    s_id = lax.broadcasted_iota(jnp.int32, (L, L), 1)
    causal = s_id <= t_id
    lane = lax.broadcasted_iota(jnp.int32, (L, LANES), 1)
    ones_blk = jnp.where(lane == 0, 1.0, 0.0).astype(BF16)
    cols = cols_ref[...]

    def col_of(idx):
        return jnp.sum(jnp.where(lane == idx, cols, 0.0), axis=1, keepdims=True)

    gate = []
    for hh in range(hp):
        h = g * hp + hh
        i_row = rows_ref[pl.ds(h, 1), :]
        b_row = rows_ref[pl.ds(heads + h, 1), :]
        i_col = col_of(h)
        b_col = col_of(heads + h)
        m_prev = m_ref[hh]
        log_d = jnp.where(causal, b_col - b_row + i_row, -jnp.inf)
        log_inter = b_col + m_prev
        m_t = jnp.maximum(log_inter, jnp.max(log_d, axis=1, keepdims=True))
        d_mat = jnp.exp(log_d - m_t)
        w_inter = jnp.exp(log_inter - m_t)
        b_last = b_row[:, L - 1:L]
        m_new = jnp.maximum(b_last + m_prev, jnp.max(b_last - b_row + i_row, axis=1, keepdims=True))
        decay = jnp.exp(b_last + m_prev - m_new)
        ws_col = jnp.exp(b_last - b_col + i_col - m_new)
        m_ref[hh] = m_new
        gate.append((m_t, d_mat, w_inter, decay, ws_col))

    s_qk = []
    for hh in range(hp):
        q = q_ref[:, hh * dk:(hh + 1) * dk]
        k = k_ref[:, hh * dk:(hh + 1) * dk]
        s_qk.append((lax.dot_general(q, k, nt, preferred_element_type=F32) * gate[hh][1]).astype(BF16))

    for hh in range(hp):
        m_t, _, w_inter, _, _ = gate[hh]
        q_scaled = (w_inter * q_ref[:, hh * dk:(hh + 1) * dk].astype(F32)).astype(BF16)
        v_aug = jnp.concatenate([v_ref[:, hh * dv:(hh + 1) * dv], ones_blk], axis=1)
        lhs = jnp.concatenate([q_scaled, s_qk[hh]], axis=1)
        rhs = jnp.concatenate([c_ref[hh].astype(BF16), v_aug], axis=0)
        num = jnp.dot(lhs, rhs, preferred_element_type=F32)
        d = jnp.maximum(jnp.abs(num[:, dv:dv + 1]), jnp.exp(-m_t))
        nh = num[:, :dv]
        r = lax.rsqrt(jnp.mean(nh * nh, axis=-1, keepdims=True) + NORM_EPS * (d * d))
        hn = nh * r * nw_ref[:, hh * dv:(hh + 1) * dv]
        gz = (o_ref[:, hh * dv:(hh + 1) * dv] * z_ref[:, hh * dv:(hh + 1) * dv]).astype(F32)
        u_ref[:, hh * dv:(hh + 1) * dv] = (hn * gz).astype(BF16)

    for hh in range(hp):
        _, _, _, decay, ws_col = gate[hh]
        k_scaled = (ws_col * k_ref[:, hh * dk:(hh + 1) * dk].astype(F32)).astype(BF16)
        v_aug = jnp.concatenate([v_ref[:, hh * dv:(hh + 1) * dv], ones_blk], axis=1)
        upd = lax.dot_general(k_scaled, v_aug, (((0,), (0,)), ((), ())), preferred_element_type=F32)
        c_ref[hh] = decay * c_ref[hh] + upd


def _mlstm_core(proj, rows, cols, norm_w, batch, seq, heads, dk, dv, hp=None):
    m = proj.shape[0]
    hp = heads if hp is None else hp
    L = MLSTM_CHUNK
    nc = seq // L
    groups = heads // hp
    v_width = heads * dv
    k_off = groups
    v_off = 2 * heads * dk // (hp * dv)
    o_off = v_off + groups
    z_off = o_off + groups
    kern = functools.partial(_mlstm_kernel, chunk=L, heads=heads, hp=hp, dk=dk, dv=dv)
    return pl.pallas_call(
        kern,
        out_shape=jax.ShapeDtypeStruct((m, v_width), BF16),
        grid=(batch, groups, nc),
        in_specs=[
            pl.BlockSpec((L, hp * dk), lambda b, g, c: (b * nc + c, g)),
            pl.BlockSpec((L, hp * dk), lambda b, g, c: (b * nc + c, k_off + g)),
            pl.BlockSpec((L, hp * dv), lambda b, g, c: (b * nc + c, v_off + g)),
            pl.BlockSpec((L, hp * dv), lambda b, g, c: (b * nc + c, o_off + g)),
            pl.BlockSpec((L, hp * dv), lambda b, g, c: (b * nc + c, z_off + g)),
            pl.BlockSpec((None, 2 * heads, L), lambda b, g, c: (b, 0, c)),
            pl.BlockSpec((L, LANES), lambda b, g, c: (b * nc + c, 0)),
            pl.BlockSpec((1, hp * dv), lambda b, g, c: (0, g)),
        ],
        out_specs=pl.BlockSpec((L, hp * dv), lambda b, g, c: (b * nc + c, g)),
        scratch_shapes=[pltpu.VMEM((hp, dk, dv + LANES), F32), pltpu.VMEM((hp, 1, 1), F32)],
        compiler_params=_cparams("parallel", "parallel", "arbitrary"),
        name="mlstm_core",
    )(proj, proj, proj, proj, proj, rows, cols, norm_w.reshape(1, v_width))


def _moba_layer(xf, h, gate, w_in, q_norm, k_norm, w_out, batch, seq):
    hd = q_norm.shape[0]
    heads = w_in.shape[1] // 4 // hd
    qkvz = _attn_inproj(h, w_in, q_norm, k_norm, seq)
    u = _moba_attention(qkvz, batch, seq, heads, hd)
    return _outproj(u, w_out, xf, gate, seq)


def _mlstm_layer(xf, h, gate, w_in, gate_b, conv_w, norm_w, w_out, batch, seq):
    heads = gate_b.shape[0] // 2
    v_width = w_out.shape[0]
    dv = v_width // heads
    qk_width = conv_w.shape[1] // 2
    dk = qk_width // heads
    n_main = 2 * qk_width + 3 * v_width
    proj = _mlstm_inproj(h, w_in, conv_w, n_main, qk_width, v_width, dk, seq)
    rows, cols = _gate_proj(h, w_in[:, n_main:], gate_b, batch, seq)
    u = _mlstm_core(proj, rows, cols, norm_w, batch, seq, heads, dk, dv)
    return _outproj(u, w_out, xf, gate, seq)


def kernel(x, c, norm_w, ada_w, ada_b, att_w_in, att_q_norm, att_k_norm, att_w_out, mlstm_w_in,
           mlstm_gate_b, mlstm_conv_w, mlstm_norm_w, mlstm_w_out):
    batch, seq, d = x.shape
    depth = norm_w.shape[0]
    mod = _adaln(c, ada_w, ada_b)
    xf = x.reshape(batch * seq, d)
    for layer in range(depth):
        shift, scale, gate = mod[layer, :, :d], mod[layer, :, d:2 * d], mod[layer, :, 2 * d:]
        h = _norm_mod(xf, norm_w[layer], scale, shift, seq)
        j = layer // 2
        if layer % 2 == 0:
            xf = _moba_layer(xf, h, gate, att_w_in[j], att_q_norm[j], att_k_norm[j], att_w_out[j], batch, seq)
        else:
            xf = _mlstm_layer(xf, h, gate, mlstm_w_in[j], mlstm_gate_b[j], mlstm_conv_w[j],
                              mlstm_norm_w[j], mlstm_w_out[j], batch, seq)
    return xf.reshape(batch, seq, d)
```

```python
import functools

import jax
import jax.numpy as jnp
from jax import lax
from jax.experimental import pallas as pl
from jax.experimental.pallas import tpu as pltpu

F32 = jnp.float32
BF16 = jnp.bfloat16
HIGHEST = lax.Precision.HIGHEST

NORM_EPS = 1e-6
MOBA_BLOCK = 256
MOBA_TOPK = 3
ROPE_THETA = 10000.0
MLSTM_CHUNK = 256
MASK_BIAS = -1e30
LANES = 128
LOG2_E = 1.4426950408889634
ROW_SUB = 512

_VMEM_LIMIT = 48 * 1024 * 1024


def _sigmoid(x):
    return 1.0 / (1.0 + jnp.exp(-x))


def _cparams(*sem):
    return pltpu.CompilerParams(dimension_semantics=sem, vmem_limit_bytes=_VMEM_LIMIT)


def _adaln_kernel(c_ref, w_ref, b_ref, o_ref):
    c = c_ref[...]
    cond = c * _sigmoid(c)
    o_ref[...] = jnp.dot(cond, w_ref[...], preferred_element_type=F32, precision=HIGHEST) + b_ref[...]


def _adaln(c, ada_w, ada_b, tn=768):
    depth, d, n3 = ada_w.shape
    b = c.shape[0]
    rows = 8
    c_pad = jnp.zeros((rows, d), F32).at[:b].set(c)
    out = pl.pallas_call(
        _adaln_kernel,
        out_shape=jax.ShapeDtypeStruct((depth, rows, n3), F32),
        grid=(depth, n3 // tn),
        in_specs=[
            pl.BlockSpec((rows, d), lambda l, n: (0, 0)),
            pl.BlockSpec((None, d, tn), lambda l, n: (l, 0, n)),
            pl.BlockSpec((None, 1, tn), lambda l, n: (l, 0, n)),
        ],
        out_specs=pl.BlockSpec((None, rows, tn), lambda l, n: (l, 0, n)),
        compiler_params=_cparams("parallel", "parallel"),
        name="adaln_mod",
    )(c_pad, ada_w, ada_b.reshape(depth, 1, n3))
    return out[:, :b]


def _norm_mod_kernel(x_ref, nw_ref, sc_ref, sh_ref, o_ref):
    x = x_ref[...]
    ms = jnp.mean(x * x, axis=-1, keepdims=True)
    y = x * lax.rsqrt(ms + NORM_EPS) * nw_ref[...]
    o_ref[...] = (y * (1.0 + sc_ref[...]) + sh_ref[...]).astype(BF16)


def _norm_mod(xf, nw, scale, shift, seq, ts=512):
    m, d = xf.shape
    b = m // seq
    tps = seq // ts
    return pl.pallas_call(
        _norm_mod_kernel,
        out_shape=jax.ShapeDtypeStruct((m, d), BF16),
        grid=(m // ts,),
        in_specs=[
            pl.BlockSpec((ts, d), lambda i: (i, 0)),
            pl.BlockSpec((1, d), lambda i: (0, 0)),
            pl.BlockSpec((None, 1, d), lambda i: (i // tps, 0, 0)),
            pl.BlockSpec((None, 1, d), lambda i: (i // tps, 0, 0)),
        ],
        out_specs=pl.BlockSpec((ts, d), lambda i: (i, 0)),
        compiler_params=_cparams("parallel"),
        name="norm_mod",
    )(xf, nw.reshape(1, d), scale.reshape(b, 1, d), shift.reshape(b, 1, d))


def _cast_weight(w_ref, wb_ref):
    @pl.when(pl.program_id(1) == 0)
    def _():
        wb_ref[...] = w_ref[...].astype(BF16)


def _row_tiles(a_ref):
    return [slice(r, r + ROW_SUB) for r in range(0, a_ref.shape[0], ROW_SUB)]


def _silu(x):
    return x * _sigmoid(x)


def _proj_act_kernel(h_ref, w_ref, o_ref, wb_ref, *, act):
    _cast_weight(w_ref, wb_ref)
    for rows in _row_tiles(h_ref):
        acc = jnp.dot(h_ref[rows, :], wb_ref[...], preferred_element_type=F32)
        o_ref[rows, :] = (acc if act is None else act(acc)).astype(BF16)


def _proj_act(h, w, col0, width, act, name, tm=1024, tn=1024):
    m, kdim = h.shape
    off = col0 // tn
    return pl.pallas_call(
        functools.partial(_proj_act_kernel, act=act),
        out_shape=jax.ShapeDtypeStruct((m, width), BF16),
        grid=(width // tn, m // tm),
        in_specs=[
            pl.BlockSpec((tm, kdim), lambda n, i: (i, 0)),
            pl.BlockSpec((kdim, tn), lambda n, i: (0, off + n)),
        ],
        out_specs=pl.BlockSpec((tm, tn), lambda n, i: (i, n)),
        scratch_shapes=[pltpu.VMEM((kdim, tn), BF16)],
        compiler_params=_cparams("parallel", "arbitrary"),
        name=name,
    )(h, w)


def _attn_qk_kernel(h_ref, w_ref, cos_ref, sin_ref, nw_ref, sc_ref, o_ref, wb_ref, *, head_dim):
    _cast_weight(w_ref, wb_ref)
    for rows in _row_tiles(h_ref):
        acc = jnp.dot(h_ref[rows, :], wb_ref[...], preferred_element_type=F32)
        cos = cos_ref[rows, :]
        sin = sin_ref[rows, :]
        for hh in range(acc.shape[1] // head_dim):
            sl = slice(hh * head_dim, (hh + 1) * head_dim)
            xh = acc[:, sl]
            ms = jnp.mean(xh * xh, axis=-1, keepdims=True)
            y = xh * lax.rsqrt(ms + NORM_EPS) * nw_ref[:, sl]
            r = y * cos + pltpu.roll(y, head_dim // 2, axis=1) * sin
            o_ref[rows, sl] = (r * sc_ref[:, sl]).astype(BF16)


def _attn_qk_proj(h, w_in, q_norm, k_norm, seq, tm=1024, tn=1024):
    m, kdim = h.shape
    width = w_in.shape[1] // 4
    hd = q_norm.shape[0]
    heads = width // hd
    tps = seq // tm
    half = hd // 2
    inv = ROPE_THETA ** (-jnp.arange(half, dtype=F32) / half)
    ang = jnp.arange(seq, dtype=F32)[:, None] * inv[None, :]
    cos = jnp.cos(ang)
    sin = jnp.sin(ang)
    cos_t = jnp.concatenate([cos, cos], axis=-1)
    sin_t = jnp.concatenate([-sin, sin], axis=-1)
    nw = jnp.concatenate([jnp.tile(q_norm, heads), jnp.tile(k_norm, heads)]).reshape(1, 2 * width)
    q_scale = float(hd) ** -0.5 * LOG2_E
    sc = jnp.concatenate([jnp.full((width,), q_scale, F32), jnp.ones((width,), F32)]).reshape(1, 2 * width)
    return pl.pallas_call(
        functools.partial(_attn_qk_kernel, head_dim=hd),
        out_shape=jax.ShapeDtypeStruct((m, 2 * width), BF16),
        grid=(2 * width // tn, m // tm),
        in_specs=[
            pl.BlockSpec((tm, kdim), lambda n, i: (i, 0)),
            pl.BlockSpec((kdim, tn), lambda n, i: (0, n)),
            pl.BlockSpec((tm, hd), lambda n, i: (i % tps, 0)),
            pl.BlockSpec((tm, hd), lambda n, i: (i % tps, 0)),
            pl.BlockSpec((1, tn), lambda n, i: (0, n)),
            pl.BlockSpec((1, tn), lambda n, i: (0, n)),
        ],
        out_specs=pl.BlockSpec((tm, tn), lambda n, i: (i, n)),
        scratch_shapes=[pltpu.VMEM((kdim, tn), BF16)],
        compiler_params=_cparams("parallel", "arbitrary"),
        name="attn_qk_proj",
    )(h, w_in, cos_t, sin_t, nw, sc)


def _moba_kernel(pairs_ref, q_ref, k_ref, v_ref, z_ref, e_ref, u_ref,
                 qaugt_ref, kaug_ref, vt_ref, m_ref, l_ref, acc_ref, *, bs, nb, topk, hd, hp, pw, n_pairs):
    seq = nb * bs
    nt = (((1,), (1,)), ((), ()))

    for h in range(hp):
        hs = slice(h * hd, (h + 1) * hd)
        kf = k_ref[:, hs].astype(F32)
        kmean = jnp.sum(kf.reshape(nb, bs, hd), axis=1) * (1.0 / bs)
        g_t = lax.dot_general(kmean, q_ref[:, hs].astype(F32), nt, preferred_element_type=F32,
                              precision=HIGHEST)
        jj = lax.broadcasted_iota(jnp.int32, (nb, seq), 0)
        qb = lax.broadcasted_iota(jnp.int32, (nb, seq), 1) // bs
        rank = jnp.zeros((nb, seq), jnp.int32)
        for jp in range(nb):
            row = g_t[jp:jp + 1, :]
            beats = (row > g_t) | ((row == g_t) & (jp < jj))
            rank = rank + jnp.where(beats & (jp < qb), 1, 0)
        allowed = ((jj < qb) & (rank < topk)) | (jj == qb)
        bias_t = jnp.where(allowed, 0.0, MASK_BIAS).astype(BF16)
        qaugt_ref[h, hd:hd + nb, :] = bias_t
        qaugt_ref[h, hd + nb:hd + LANES, :] = jnp.zeros((LANES - nb, seq), BF16)
        kaug_ref[h, :, 0:hd] = k_ref[:, hs]
        kaug_ref[h, :, hd:hd + LANES] = e_ref[...]
        for jb in range(nb):
            rs = slice(jb * bs, (jb + 1) * bs)
            qaugt_ref[h, 0:hd, rs] = q_ref[rs, hs].astype(F32).T.astype(BF16)
            vt_ref[h, :, rs] = v_ref[rs, hs].astype(F32).T.astype(BF16)
    m_ref[...] = jnp.full(m_ref.shape, -jnp.inf, F32)
    l_ref[...] = jnp.zeros(l_ref.shape, F32)
    acc_ref[...] = jnp.zeros(acc_ref.shape, F32)

    key_id = lax.broadcasted_iota(jnp.int32, (bs, bs), 0)
    qry_id = lax.broadcasted_iota(jnp.int32, (bs, bs), 1)
    causal = key_id <= qry_id

    def units(pair, t):
        return [(h,) + pair(t * pw + u) for u in range(pw) for h in range(hp)]

    def stage_scores(us):
        out = []
        for h, i, j in us:
            q0 = pl.multiple_of(i * bs, bs)
            j0 = pl.multiple_of(j * bs, bs)
            out.append(jnp.dot(kaug_ref[h, pl.ds(j0, bs), :], qaugt_ref[h, :, pl.ds(q0, bs)],
                               preferred_element_type=F32))
        return tuple(out)

    def stage_softmax(us, ss, diag):
        out = []
        for (h, i, _), s in zip(us, ss):
            if diag:
                s = jnp.where(causal, s, -jnp.inf)
            m_old = m_ref[h, i]
            m_new = jnp.maximum(m_old, jnp.max(s, axis=0, keepdims=True))
            alpha = jnp.exp2(m_old - m_new)
            p = jnp.exp2(s - m_new)
            l_ref[h, i] = alpha * l_ref[h, i] + jnp.sum(p, axis=0, keepdims=True)
            m_ref[h, i] = m_new
            out += [p.astype(BF16), alpha]
        return tuple(out)

    def stage_pv(us, pa):
        for n, (h, i, j) in enumerate(us):
            j0 = pl.multiple_of(j * bs, bs)
            pv = jnp.dot(vt_ref[h, :, pl.ds(j0, bs)], pa[2 * n], preferred_element_type=F32)
            acc_ref[h, i] = pa[2 * n + 1] * acc_ref[h, i] + pv

    def run_pairs(pair, count, diag):
        steps = count // pw
        nu = pw * hp
        idle = []
        for _ in range(nu):
            idle += [jnp.zeros((bs, bs), BF16), jnp.ones((1, bs), F32)]

        def body(t, carry):
            ss, pa = carry[:nu], carry[nu:]
            ss_next = stage_scores(units(pair, t))
            pa_next = stage_softmax(units(pair, t - 1), ss, diag)
            stage_pv(units(pair, jnp.maximum(t - 2, 0)), pa)
            return tuple(ss_next) + tuple(pa_next)

        carry = lax.fori_loop(1, steps, body, stage_scores(units(pair, 0)) + tuple(idle))
        ss, pa = carry[:nu], carry[nu:]
        pa_last = stage_softmax(units(pair, steps - 1), ss, diag)
        stage_pv(units(pair, max(steps - 2, 0)), pa)
        stage_pv(units(pair, steps - 1), pa_last)

    run_pairs(lambda t: (t, t), nb, True)
    if n_pairs > 0:
        run_pairs(lambda t: (pairs_ref[0, t], pairs_ref[1, t]), n_pairs, False)

    def finalize(i, _):
        q0 = pl.multiple_of(i * bs, bs)
        for h in range(hp):
            hs = slice(h * hd, (h + 1) * hd)
            o = (acc_ref[h, i] / l_ref[h, i]).T
            u_ref[pl.ds(q0, bs), hs] = (o * z_ref[pl.ds(q0, bs), hs].astype(F32)).astype(BF16)
        return 0

    lax.fori_loop(0, nb, finalize, 0)


def _moba_attention(qk, v, z_act, batch, seq, heads, hd, hp=1):
    m = qk.shape[0]
    bs = MOBA_BLOCK
    nb = seq // bs
    width = heads * hd
    groups = heads // hp
    e_tab = (jnp.arange(seq)[:, None] // bs == jnp.arange(LANES)[None, :]).astype(BF16)
    pair_list = [(i, j) for j in range(nb) for i in range(j + 1, nb)]
    n_pairs = len(pair_list)
    pairs = jnp.asarray(pair_list or [(0, 0)], jnp.int32).T
    pw = next(w for w in (4, 2, 1) if nb % w == 0 and n_pairs % w == 0)
    kern = functools.partial(_moba_kernel, bs=bs, nb=nb, topk=min(MOBA_TOPK, nb), hd=hd, hp=hp, pw=pw,
                             n_pairs=n_pairs)
    blk = (seq, hp * hd)
    grid_spec = pltpu.PrefetchScalarGridSpec(
        num_scalar_prefetch=1,
        grid=(batch, groups),
        in_specs=[
            pl.BlockSpec(blk, lambda b, g, pr: (b, g)),
            pl.BlockSpec(blk, lambda b, g, pr: (b, groups + g)),
            pl.BlockSpec(blk, lambda b, g, pr: (b, g)),
            pl.BlockSpec(blk, lambda b, g, pr: (b, g)),
            pl.BlockSpec((seq, LANES), lambda b, g, pr: (0, 0)),
        ],
        out_specs=pl.BlockSpec(blk, lambda b, g, pr: (b, g)),
        scratch_shapes=[
            pltpu.VMEM((hp, hd + LANES, seq), BF16),
            pltpu.VMEM((hp, seq, hd + LANES), BF16),
            pltpu.VMEM((hp, hd, seq), BF16),
            pltpu.VMEM((hp, nb, 1, bs), F32),
            pltpu.VMEM((hp, nb, 1, bs), F32),
            pltpu.VMEM((hp, nb, hd, bs), F32),
        ],
    )
    return pl.pallas_call(
        kern,
        out_shape=jax.ShapeDtypeStruct((m, width), BF16),
        grid_spec=grid_spec,
        compiler_params=_cparams("parallel", "parallel"),
        name="moba_attn",
    )(pairs, qk, qk, v, z_act, e_tab)


def _outproj_kernel(u_ref, w_ref, x_ref, g_ref, o_ref, wb_ref):
    _cast_weight(w_ref, wb_ref)
    acc = jnp.dot(u_ref[...], wb_ref[...], preferred_element_type=F32)
    o_ref[...] = x_ref[...] + g_ref[...] * acc


def _outproj(u, w_out, xf, gate, seq, tm=512, tn=512):
    m, kdim = u.shape
    d = w_out.shape[1]
    b = m // seq
    tps = seq // tm
    return pl.pallas_call(
        _outproj_kernel,
        out_shape=jax.ShapeDtypeStruct((m, d), F32),
        grid=(d // tn, m // tm),
        in_specs=[
            pl.BlockSpec((tm, kdim), lambda n, i: (i, 0)),
            pl.BlockSpec((kdim, tn), lambda n, i: (0, n)),
            pl.BlockSpec((tm, tn), lambda n, i: (i, n)),
            pl.BlockSpec((None, 1, tn), lambda n, i: (i // tps, 0, n)),
        ],
        out_specs=pl.BlockSpec((tm, tn), lambda n, i: (i, n)),
        scratch_shapes=[pltpu.VMEM((kdim, tn), BF16)],
        compiler_params=_cparams("parallel", "arbitrary"),
        name="outproj",
    )(u, w_out, xf, gate.reshape(b, 1, d))


def _mlstm_qk_kernel(h_ref, w_ref, cw_ref, sc_ref, o_ref, wb_ref, ext_ref, *, tiles_per_seq, conv_k):
    i = pl.program_id(1)
    _cast_weight(w_ref, wb_ref)
    sub = ROW_SUB

    tiles = _row_tiles(h_ref)
    slots = ext_ref.shape[0]
    last = (len(tiles) - 1) % slots

    @pl.when(i % tiles_per_seq == 0)
    def _():
        ext_ref[last, sub:sub + 8, :] = jnp.zeros((8, ext_ref.shape[2]), F32)

    for r, rows in enumerate(tiles):
        cur, prev = r % slots, (r - 1) % slots if r else last
        acc = jnp.dot(h_ref[rows, :], wb_ref[...], preferred_element_type=F32)
        ext_ref[cur, 0:8, :] = ext_ref[prev, sub:sub + 8, :]
        ext_ref[cur, 8:sub + 8, :] = acc
        y = cw_ref[conv_k - 1:conv_k, :] * acc
        for back in range(1, conv_k):
            y = y + cw_ref[conv_k - 1 - back:conv_k - back, :] * ext_ref[cur, 8 - back:8 - back + sub, :]
        o_ref[rows, :] = (_silu(y) * sc_ref[...]).astype(BF16)


def _mlstm_qk_proj(h, w_in, conv_w, qk_width, dk, seq, tm=1024, tn=1024):
    m, kdim = h.shape
    conv_k = conv_w.shape[0]
    tps = seq // tm
    sc = jnp.concatenate([jnp.ones((qk_width,), F32),
                          jnp.full((qk_width,), float(dk) ** -0.5, F32)]).reshape(1, 2 * qk_width)
    return pl.pallas_call(
        functools.partial(_mlstm_qk_kernel, tiles_per_seq=tps, conv_k=conv_k),
        out_shape=jax.ShapeDtypeStruct((m, 2 * qk_width), BF16),
        grid=(2 * qk_width // tn, m // tm),
        in_specs=[
            pl.BlockSpec((tm, kdim), lambda n, i: (i, 0)),
            pl.BlockSpec((kdim, tn), lambda n, i: (0, n)),
            pl.BlockSpec((conv_k, tn), lambda n, i: (0, n)),
            pl.BlockSpec((1, tn), lambda n, i: (0, n)),
        ],
        out_specs=pl.BlockSpec((tm, tn), lambda n, i: (i, n)),
        scratch_shapes=[pltpu.VMEM((kdim, tn), BF16),
                        pltpu.VMEM((2 if (tm // ROW_SUB) % 2 == 0 else 1, ROW_SUB + 8, tn), F32)],
        compiler_params=_cparams("arbitrary", "arbitrary"),
        name="mlstm_qk_proj",
    )(h, w_in, conv_w, sc)


def _mlstm_gz_kernel(h_ref, wo_ref, wz_ref, o_ref, wob_ref, wzb_ref):
    @pl.when(pl.program_id(1) == 0)
    def _():
        wob_ref[...] = wo_ref[...].astype(BF16)
        wzb_ref[...] = wz_ref[...].astype(BF16)

    for rows in _row_tiles(h_ref):
        a = h_ref[rows, :]
        og = jnp.dot(a, wob_ref[...], preferred_element_type=F32)
        z = jnp.dot(a, wzb_ref[...], preferred_element_type=F32)
        o_ref[rows, :] = (_sigmoid(og) * _silu(z)).astype(BF16)


def _mlstm_gz_proj(h, w_in, o_col0, z_col0, width, tm=1024, tn=512):
    m, kdim = h.shape
    o_off, z_off = o_col0 // tn, z_col0 // tn
    return pl.pallas_call(
        _mlstm_gz_kernel,
        out_shape=jax.ShapeDtypeStruct((m, width), BF16),
        grid=(width // tn, m // tm),
        in_specs=[
            pl.BlockSpec((tm, kdim), lambda n, i: (i, 0)),
            pl.BlockSpec((kdim, tn), lambda n, i: (0, o_off + n)),
            pl.BlockSpec((kdim, tn), lambda n, i: (0, z_off + n)),
        ],
        out_specs=pl.BlockSpec((tm, tn), lambda n, i: (i, n)),
        scratch_shapes=[pltpu.VMEM((kdim, tn), BF16), pltpu.VMEM((kdim, tn), BF16)],
        compiler_params=_cparams("parallel", "arbitrary"),
        name="mlstm_gz_proj",
    )(h, w_in, w_in)


def _gate_proj_kernel(h_ref, w_ref, b_ref, rows_ref, cols_ref, *, heads, chunk):
    g = jnp.dot(h_ref[...], w_ref[...].astype(BF16), preferred_element_type=F32) + b_ref[...]
    tm = g.shape[0]
    g_t = g.T
    i_rows = g_t[0:heads, :]
    f_rows = g_t[heads:2 * heads, :]
    b = jnp.minimum(f_rows, 0.0) - jnp.log1p(jnp.exp(-jnp.abs(f_rows)))
    pos = lax.broadcasted_iota(jnp.int32, b.shape, 1) % chunk
    shift = 1
    while shift < chunk:
        b = b + jnp.where(pos >= shift, pltpu.roll(b, shift, axis=1), 0.0)
        shift *= 2
    rows = jnp.concatenate([i_rows, b], axis=0)
    rows_ref[...] = rows
    cols_ref[...] = jnp.concatenate([rows, jnp.zeros((LANES - 2 * heads, tm), F32)], axis=0).T


def _gate_proj(h, w_gate, gate_b, batch, seq, tm=1024):
    m, kdim = h.shape
    ng = w_gate.shape[1]
    heads = ng // 2
    tps = seq // tm
    w_pad = jnp.zeros((kdim, LANES), F32).at[:, :ng].set(w_gate)
    b_pad = jnp.zeros((1, LANES), F32).at[0, :ng].set(gate_b)
    kern = functools.partial(_gate_proj_kernel, heads=heads, chunk=MLSTM_CHUNK)
    return pl.pallas_call(
        kern,
        out_shape=(jax.ShapeDtypeStruct((batch, ng, seq), F32), jax.ShapeDtypeStruct((m, LANES), F32)),
        grid=(m // tm,),
        in_specs=[
            pl.BlockSpec((tm, kdim), lambda i: (i, 0)),
            pl.BlockSpec((kdim, LANES), lambda i: (0, 0)),
            pl.BlockSpec((1, LANES), lambda i: (0, 0)),
        ],
        out_specs=(pl.BlockSpec((None, ng, tm), lambda i: (i // tps, 0, i % tps)),
                   pl.BlockSpec((tm, LANES), lambda i: (i, 0))),
        compiler_params=_cparams("parallel"),
        name="mlstm_gate_proj",
    )(h, w_pad, b_pad)


def _mlstm_kernel(q_ref, k_ref, v_ref, gz_ref, rows_ref, cols_ref, nw_ref, u_ref, c_ref, m_ref, *,
                  chunk, heads, hp, dk, dv):
    g = pl.program_id(1)
    c = pl.program_id(2)
    L = chunk
    nt = (((1,), (1,)), ((), ()))

    @pl.when(c == 0)
    def _():
        c_ref[...] = jnp.zeros(c_ref.shape, F32)
        m_ref[...] = jnp.zeros(m_ref.shape, F32)

    t_id = lax.broadcasted_iota(jnp.int32, (L, L), 0)
    s_id = lax.broadcasted_iota(jnp.int32, (L, L), 1)
    causal = s_id <= t_id
    lane = lax.broadcasted_iota(jnp.int32, (L, LANES), 1)
    ones_blk = jnp.where(lane == 0, 1.0, 0.0).astype(BF16)
    cols = cols_ref[...]

    def col_of(idx):
        return jnp.sum(jnp.where(lane == idx, cols, 0.0), axis=1, keepdims=True)

    gate = []
    for hh in range(hp):
        h = g * hp + hh
        i_row = rows_ref[pl.ds(h, 1), :]
        b_row = rows_ref[pl.ds(heads + h, 1), :]
        i_col = col_of(h)
        b_col = col_of(heads + h)
        m_prev = m_ref[hh]
        log_d = jnp.where(causal, b_col - b_row + i_row, -jnp.inf)
        log_inter = b_col + m_prev
        m_t = jnp.maximum(log_inter, jnp.max(log_d, axis=1, keepdims=True))
        d_mat = jnp.exp(log_d - m_t)
        w_inter = jnp.exp(log_inter - m_t)
        b_last = b_row[:, L - 1:L]
        m_new = jnp.maximum(b_last + m_prev, jnp.max(b_last - b_row + i_row, axis=1, keepdims=True))
        decay = jnp.exp(b_last + m_prev - m_new)
        ws_col = jnp.exp(b_last - b_col + i_col - m_new)
        m_ref[hh] = m_new
        gate.append((m_t, d_mat, w_inter, decay, ws_col))

    s_qk = []
    for hh in range(hp):
        q = q_ref[:, hh * dk:(hh + 1) * dk]
        k = k_ref[:, hh * dk:(hh + 1) * dk]
        s_qk.append((lax.dot_general(q, k, nt, preferred_element_type=F32) * gate[hh][1]).astype(BF16))

    for hh in range(hp):
        m_t, _, w_inter, _, _ = gate[hh]
        q_scaled = (w_inter * q_ref[:, hh * dk:(hh + 1) * dk].astype(F32)).astype(BF16)
        v_aug = jnp.concatenate([v_ref[:, hh * dv:(hh + 1) * dv], ones_blk], axis=1)
        lhs = jnp.concatenate([q_scaled, s_qk[hh]], axis=1)
        rhs = jnp.concatenate([c_ref[hh].astype(BF16), v_aug], axis=0)
        num = jnp.dot(lhs, rhs, preferred_element_type=F32)
        d = jnp.maximum(jnp.abs(num[:, dv:dv + 1]), jnp.exp(-m_t))
        nh = num[:, :dv]
        r = lax.rsqrt(jnp.mean(nh * nh, axis=-1, keepdims=True) + NORM_EPS * (d * d))
        hn = nh * r * nw_ref[:, hh * dv:(hh + 1) * dv]
        u_ref[:, hh * dv:(hh + 1) * dv] = (hn * gz_ref[:, hh * dv:(hh + 1) * dv].astype(F32)).astype(BF16)

    for hh in range(hp):
        _, _, _, decay, ws_col = gate[hh]
        k_scaled = (ws_col * k_ref[:, hh * dk:(hh + 1) * dk].astype(F32)).astype(BF16)
        v_aug = jnp.concatenate([v_ref[:, hh * dv:(hh + 1) * dv], ones_blk], axis=1)
        upd = lax.dot_general(k_scaled, v_aug, (((0,), (0,)), ((), ())), preferred_element_type=F32)
        c_ref[hh] = decay * c_ref[hh] + upd


def _mlstm_core(qk, v, gz, rows, cols, norm_w, batch, seq, heads, dk, dv, hp=None):
    m = qk.shape[0]
    hp = heads if hp is None else hp
    L = MLSTM_CHUNK
    nc = seq // L
    groups = heads // hp
    v_width = heads * dv
    k_off = groups
    kern = functools.partial(_mlstm_kernel, chunk=L, heads=heads, hp=hp, dk=dk, dv=dv)
    return pl.pallas_call(
        kern,
        out_shape=jax.ShapeDtypeStruct((m, v_width), BF16),
        grid=(batch, groups, nc),
        in_specs=[
            pl.BlockSpec((L, hp * dk), lambda b, g, c: (b * nc + c, g)),
            pl.BlockSpec((L, hp * dk), lambda b, g, c: (b * nc + c, k_off + g)),
            pl.BlockSpec((L, hp * dv), lambda b, g, c: (b * nc + c, g)),
            pl.BlockSpec((L, hp * dv), lambda b, g, c: (b * nc + c, g)),
            pl.BlockSpec((None, 2 * heads, L), lambda b, g, c: (b, 0, c)),
            pl.BlockSpec((L, LANES), lambda b, g, c: (b * nc + c, 0)),
            pl.BlockSpec((1, hp * dv), lambda b, g, c: (0, g)),
        ],
        out_specs=pl.BlockSpec((L, hp * dv), lambda b, g, c: (b * nc + c, g)),
        scratch_shapes=[pltpu.VMEM((hp, dk, dv + LANES), F32), pltpu.VMEM((hp, 1, 1), F32)],
        compiler_params=_cparams("parallel", "parallel", "arbitrary"),
        name="mlstm_core",
    )(qk, qk, v, gz, rows, cols, norm_w.reshape(1, v_width))


def _moba_layer(xf, h, gate, w_in, q_norm, k_norm, w_out, batch, seq):
    hd = q_norm.shape[0]
    heads = w_in.shape[1] // 4 // hd
    width = heads * hd
    qk = _attn_qk_proj(h, w_in, q_norm, k_norm, seq)
    v = _proj_act(h, w_in, 2 * width, width, None, "attn_v_proj")
    z_act = _proj_act(h, w_in, 3 * width, width, _silu, "attn_z_proj")
    u = _moba_attention(qk, v, z_act, batch, seq, heads, hd)
    return _outproj(u, w_out, xf, gate, seq)


def _mlstm_layer(xf, h, gate, w_in, gate_b, conv_w, norm_w, w_out, batch, seq):
    heads = gate_b.shape[0] // 2
    v_width = w_out.shape[0]
    dv = v_width // heads
    qk_width = conv_w.shape[1] // 2
    dk = qk_width // heads
    n_main = 2 * qk_width + 3 * v_width
    qk = _mlstm_qk_proj(h, w_in, conv_w, qk_width, dk, seq)
    v = _proj_act(h, w_in, 2 * qk_width, v_width, None, "mlstm_v_proj")
    gz = _mlstm_gz_proj(h, w_in, 2 * qk_width + v_width, 2 * qk_width + 2 * v_width, v_width)
    rows, cols = _gate_proj(h, w_in[:, n_main:], gate_b, batch, seq)
    u = _mlstm_core(qk, v, gz, rows, cols, norm_w, batch, seq, heads, dk, dv)
    return _outproj(u, w_out, xf, gate, seq)


def kernel(x, c, norm_w, ada_w, ada_b, att_w_in, att_q_norm, att_k_norm, att_w_out, mlstm_w_in,
           mlstm_gate_b, mlstm_conv_w, mlstm_norm_w, mlstm_w_out):
    batch, seq, d = x.shape
    depth = norm_w.shape[0]
    mod = _adaln(c, ada_w, ada_b)
    xf = x.reshape(batch * seq, d)
    for layer in range(depth):
        shift, scale, gate = mod[layer, :, :d], mod[layer, :, d:2 * d], mod[layer, :, 2 * d:]
        h = _norm_mod(xf, norm_w[layer], scale, shift, seq)
        j = layer // 2
        if layer % 2 == 0:
            xf = _moba_layer(xf, h, gate, att_w_in[j], att_q_norm[j], att_k_norm[j], att_w_out[j], batch, seq)
        else:
            xf = _mlstm_layer(xf, h, gate, mlstm_w_in[j], mlstm_gate_b[j], mlstm_conv_w[j],
                              mlstm_norm_w[j], mlstm_w_out[j], batch, seq)
    return xf.reshape(batch, seq, d)
```

```python
import functools

import jax
import jax.numpy as jnp
from jax import lax
from jax.experimental import pallas as pl
from jax.experimental.pallas import tpu as pltpu

F32 = jnp.float32
BF16 = jnp.bfloat16
HIGHEST = lax.Precision.HIGHEST

NORM_EPS = 1e-6
MOBA_BLOCK = 256
MOBA_TOPK = 3
ROPE_THETA = 10000.0
MLSTM_CHUNK = 256
MASK_BIAS = -1e30
LANES = 128
LOG2_E = 1.4426950408889634
ROW_SUB = 512

_VMEM_LIMIT = 48 * 1024 * 1024


def _sigmoid(x):
    return 1.0 / (1.0 + jnp.exp(-x))


def _cparams(*sem):
    return pltpu.CompilerParams(dimension_semantics=sem, vmem_limit_bytes=_VMEM_LIMIT)


def _adaln_kernel(c_ref, w_ref, b_ref, o_ref):
    c = c_ref[...]
    cond = c * _sigmoid(c)
    o_ref[...] = jnp.dot(cond, w_ref[...], preferred_element_type=F32, precision=HIGHEST) + b_ref[...]


def _adaln(c, ada_w, ada_b, tn=768):
    depth, d, n3 = ada_w.shape
    b = c.shape[0]
    rows = 8
    c_pad = jnp.zeros((rows, d), F32).at[:b].set(c)
    out = pl.pallas_call(
        _adaln_kernel,
        out_shape=jax.ShapeDtypeStruct((depth, rows, n3), F32),
        grid=(depth, n3 // tn),
        in_specs=[
            pl.BlockSpec((rows, d), lambda l, n: (0, 0)),
            pl.BlockSpec((None, d, tn), lambda l, n: (l, 0, n)),
            pl.BlockSpec((None, 1, tn), lambda l, n: (l, 0, n)),
        ],
        out_specs=pl.BlockSpec((None, rows, tn), lambda l, n: (l, 0, n)),
        compiler_params=_cparams("parallel", "parallel"),
        name="adaln_mod",
    )(c_pad, ada_w, ada_b.reshape(depth, 1, n3))
    return out[:, :b]


def _norm_mod_kernel(x_ref, nw_ref, sc_ref, sh_ref, o_ref):
    x = x_ref[...]
    ms = jnp.mean(x * x, axis=-1, keepdims=True)
    y = x * lax.rsqrt(ms + NORM_EPS) * nw_ref[...]
    o_ref[...] = (y * (1.0 + sc_ref[...]) + sh_ref[...]).astype(BF16)


def _norm_mod(xf, nw, scale, shift, seq, ts=512):
    m, d = xf.shape
    b = m // seq
    tps = seq // ts
    return pl.pallas_call(
        _norm_mod_kernel,
        out_shape=jax.ShapeDtypeStruct((m, d), BF16),
        grid=(m // ts,),
        in_specs=[
            pl.BlockSpec((ts, d), lambda i: (i, 0)),
            pl.BlockSpec((1, d), lambda i: (0, 0)),
            pl.BlockSpec((None, 1, d), lambda i: (i // tps, 0, 0)),
            pl.BlockSpec((None, 1, d), lambda i: (i // tps, 0, 0)),
        ],
        out_specs=pl.BlockSpec((ts, d), lambda i: (i, 0)),
        compiler_params=_cparams("parallel"),
        name="norm_mod",
    )(xf, nw.reshape(1, d), scale.reshape(b, 1, d), shift.reshape(b, 1, d))


def _row_tiles(a_ref):
    return [slice(r, r + ROW_SUB) for r in range(0, a_ref.shape[0], ROW_SUB)]


def _silu(x):
    return x * _sigmoid(x)


def _proj_act_kernel(h_ref, w_ref, o_ref, *, act):
    for rows in _row_tiles(h_ref):
        acc = jnp.dot(h_ref[rows, :], w_ref[...], preferred_element_type=F32)
        o_ref[rows, :] = (acc if act is None else act(acc)).astype(BF16)


def _proj_act(h, w, col0, width, act, name, tm=1024, tn=1024):
    m, kdim = h.shape
    off = col0 // tn
    return pl.pallas_call(
        functools.partial(_proj_act_kernel, act=act),
        out_shape=jax.ShapeDtypeStruct((m, width), BF16),
        grid=(width // tn, m // tm),
        in_specs=[
            pl.BlockSpec((tm, kdim), lambda n, i: (i, 0)),
            pl.BlockSpec((kdim, tn), lambda n, i: (0, off + n)),
        ],
        out_specs=pl.BlockSpec((tm, tn), lambda n, i: (i, n)),
        compiler_params=_cparams("parallel", "parallel"),
        name=name,
    )(h, w)


def _attn_qk_kernel(h_ref, w_ref, cos_ref, sin_ref, nw_ref, sc_ref, o_ref, *, head_dim):
    for rows in _row_tiles(h_ref):
        acc = jnp.dot(h_ref[rows, :], w_ref[...], preferred_element_type=F32)
        cos = cos_ref[rows, :]
        sin = sin_ref[rows, :]
        for hh in range(acc.shape[1] // head_dim):
            sl = slice(hh * head_dim, (hh + 1) * head_dim)
            xh = acc[:, sl]
            ms = jnp.mean(xh * xh, axis=-1, keepdims=True)
            y = xh * lax.rsqrt(ms + NORM_EPS) * nw_ref[:, sl]
            r = y * cos + pltpu.roll(y, head_dim // 2, axis=1) * sin
            o_ref[rows, sl] = (r * sc_ref[:, sl]).astype(BF16)


def _attn_qk_proj(h, w_in, q_norm, k_norm, seq, tm=1024, tn=1024):
    m, kdim = h.shape
    width = w_in.shape[1] // 4
    hd = q_norm.shape[0]
    heads = width // hd
    tps = seq // tm
    half = hd // 2
    inv = ROPE_THETA ** (-jnp.arange(half, dtype=F32) / half)
    ang = jnp.arange(seq, dtype=F32)[:, None] * inv[None, :]
    cos = jnp.cos(ang)
    sin = jnp.sin(ang)
    cos_t = jnp.concatenate([cos, cos], axis=-1)
    sin_t = jnp.concatenate([-sin, sin], axis=-1)
    nw = jnp.concatenate([jnp.tile(q_norm, heads), jnp.tile(k_norm, heads)]).reshape(1, 2 * width)
    q_scale = float(hd) ** -0.5 * LOG2_E
    sc = jnp.concatenate([jnp.full((width,), q_scale, F32), jnp.ones((width,), F32)]).reshape(1, 2 * width)
    return pl.pallas_call(
        functools.partial(_attn_qk_kernel, head_dim=hd),
        out_shape=jax.ShapeDtypeStruct((m, 2 * width), BF16),
        grid=(2 * width // tn, m // tm),
        in_specs=[
            pl.BlockSpec((tm, kdim), lambda n, i: (i, 0)),
            pl.BlockSpec((kdim, tn), lambda n, i: (0, n)),
            pl.BlockSpec((tm, hd), lambda n, i: (i % tps, 0)),
            pl.BlockSpec((tm, hd), lambda n, i: (i % tps, 0)),
            pl.BlockSpec((1, tn), lambda n, i: (0, n)),
            pl.BlockSpec((1, tn), lambda n, i: (0, n)),
        ],
        out_specs=pl.BlockSpec((tm, tn), lambda n, i: (i, n)),
        compiler_params=_cparams("parallel", "parallel"),
        name="attn_qk_proj",
    )(h, w_in, cos_t, sin_t, nw, sc)


def _moba_kernel(pairs_ref, q_ref, k_ref, v_ref, z_ref, e_ref, u_ref,
                 qaugt_ref, kaug_ref, vt_ref, m_ref, acc_ref, sbuf_ref, cbuf_ref, abuf_ref, pbuf_ref, *,
                 bs, nb, topk, hd, hp, pw, n_pairs):
    seq = nb * bs
    nt = (((1,), (1,)), ((), ()))

    for h in range(hp):
        hs = slice(h * hd, (h + 1) * hd)
        kf = k_ref[:, hs].astype(F32)
        kmean = jnp.sum(kf.reshape(nb, bs, hd), axis=1) * (1.0 / bs)
        g_t = lax.dot_general(kmean, q_ref[:, hs].astype(F32), nt, preferred_element_type=F32,
                              precision=HIGHEST)
        jj = lax.broadcasted_iota(jnp.int32, (nb, seq), 0)
        qb = lax.broadcasted_iota(jnp.int32, (nb, seq), 1) // bs
        rank = jnp.zeros((nb, seq), jnp.int32)
        for jp in range(nb):
            row = g_t[jp:jp + 1, :]
            beats = (row > g_t) | ((row == g_t) & (jp < jj))
            rank = rank + jnp.where(beats & (jp < qb), 1, 0)
        allowed = ((jj < qb) & (rank < topk)) | (jj == qb)
        bias_t = jnp.where(allowed, 0.0, MASK_BIAS).astype(BF16)
        qaugt_ref[h, hd:hd + nb, :] = bias_t
        qaugt_ref[h, hd + nb:hd + LANES, :] = jnp.zeros((LANES - nb, seq), BF16)
        kaug_ref[h, :, 0:hd] = k_ref[:, hs]
        kaug_ref[h, :, hd:hd + LANES] = e_ref[...]
        for jb in range(nb):
            rs = slice(jb * bs, (jb + 1) * bs)
            qaugt_ref[h, 0:hd, rs] = q_ref[rs, hs].astype(F32).T.astype(BF16)
            vt_ref[h, 0:hd, rs] = v_ref[rs, hs].astype(F32).T.astype(BF16)
        ones_row = lax.broadcasted_iota(jnp.int32, (8, seq), 0) == 0
        vt_ref[h, hd:hd + 8, :] = jnp.where(ones_row, 1.0, 0.0).astype(BF16)
    m_ref[...] = jnp.full(m_ref.shape, -jnp.inf, F32)
    acc_ref[...] = jnp.zeros(acc_ref.shape, F32)

    key_id = lax.broadcasted_iota(jnp.int32, (bs, bs), 0)
    qry_id = lax.broadcasted_iota(jnp.int32, (bs, bs), 1)
    causal = key_id <= qry_id
    nu = pw * hp

    def units(pair, t):
        return [(h,) + pair(t * pw + u) for u in range(pw) for h in range(hp)]

    def stage_scores(us, slot, diag):
        for n, (h, i, j) in enumerate(us):
            q0 = pl.multiple_of(i * bs, bs)
            j0 = pl.multiple_of(j * bs, bs)
            s = jnp.dot(kaug_ref[h, pl.ds(j0, bs), :], qaugt_ref[h, :, pl.ds(q0, bs)],
                        preferred_element_type=F32)
            if diag:
                s = jnp.where(causal, s, -jnp.inf)
            sbuf_ref[slot, n] = s
            cbuf_ref[slot, n] = jnp.max(s, axis=0, keepdims=True)

    def stage_softmax(us, slot):
        for n, (h, i, _) in enumerate(us):
            m_old = m_ref[h, i]
            m_new = jnp.maximum(m_old, cbuf_ref[slot, n])
            abuf_ref[slot, n] = jnp.exp2(m_old - m_new)
            m_ref[h, i] = m_new
            pbuf_ref[slot, n] = jnp.exp2(sbuf_ref[slot, n] - m_new).astype(BF16)

    def stage_pv(us, slot):
        for n, (h, i, j) in enumerate(us):
            j0 = pl.multiple_of(j * bs, bs)
            pv = jnp.dot(vt_ref[h, :, pl.ds(j0, bs)], pbuf_ref[slot, n], preferred_element_type=F32)
            acc_ref[h, i] = abuf_ref[slot, n] * acc_ref[h, i] + pv

    def run_pairs(pair, count, diag):
        steps = count // pw
        pbuf_ref[1] = jnp.zeros(pbuf_ref.shape[1:], BF16)
        abuf_ref[1] = jnp.ones(abuf_ref.shape[1:], F32)
        stage_scores(units(pair, 0), 0, diag)

        def step(t, slot):
            stage_scores(units(pair, t), slot, diag)
            stage_softmax(units(pair, t - 1), 1 - slot)
            stage_pv(units(pair, jnp.maximum(t - 2, 0)), slot)

        def two_steps(t2, _):
            step(2 * t2 + 1, 1)
            step(2 * t2 + 2, 0)
            return 0

        lax.fori_loop(0, (steps - 2) // 2, two_steps, 0)
        step(steps - 1, 1)
        stage_softmax(units(pair, steps - 1), 1)
        stage_pv(units(pair, steps - 2), 0)
        stage_pv(units(pair, steps - 1), 1)

    run_pairs(lambda t: (t, t), nb, True)
    if n_pairs > 0:
        run_pairs(lambda t: (pairs_ref[0, t], pairs_ref[1, t]), n_pairs, False)

    def finalize(i, _):
        q0 = pl.multiple_of(i * bs, bs)
        for h in range(hp):
            hs = slice(h * hd, (h + 1) * hd)
            acc = acc_ref[h, i]
            o = (acc[0:hd, :] / acc[hd:hd + 1, :]).T
            u_ref[pl.ds(q0, bs), hs] = (o * z_ref[pl.ds(q0, bs), hs].astype(F32)).astype(BF16)
        return 0

    lax.fori_loop(0, nb, finalize, 0)


def _moba_attention(qk, v, z_act, batch, seq, heads, hd, hp=1):
    m = qk.shape[0]
    bs = MOBA_BLOCK
    nb = seq // bs
    width = heads * hd
    groups = heads // hp
    e_tab = (jnp.arange(seq)[:, None] // bs == jnp.arange(LANES)[None, :]).astype(BF16)
    pair_list = [(i, j) for j in range(nb) for i in range(j + 1, nb)]
    n_pairs = len(pair_list)
    pairs = jnp.asarray(pair_list or [(0, 0)], jnp.int32).T
    pw = next(w for w in (4, 2, 1) if nb % (2 * w) == 0 and n_pairs % (2 * w) == 0)
    kern = functools.partial(_moba_kernel, bs=bs, nb=nb, topk=min(MOBA_TOPK, nb), hd=hd, hp=hp, pw=pw,
                             n_pairs=n_pairs)
    blk = (seq, hp * hd)
    grid_spec = pltpu.PrefetchScalarGridSpec(
        num_scalar_prefetch=1,
        grid=(batch, groups),
        in_specs=[
            pl.BlockSpec(blk, lambda b, g, pr: (b, g)),
            pl.BlockSpec(blk, lambda b, g, pr: (b, groups + g)),
            pl.BlockSpec(blk, lambda b, g, pr: (b, g)),
            pl.BlockSpec(blk, lambda b, g, pr: (b, g)),
            pl.BlockSpec((seq, LANES), lambda b, g, pr: (0, 0)),
        ],
        out_specs=pl.BlockSpec(blk, lambda b, g, pr: (b, g)),
        scratch_shapes=[
            pltpu.VMEM((hp, hd + LANES, seq), BF16),
            pltpu.VMEM((hp, seq, hd + LANES), BF16),
            pltpu.VMEM((hp, hd + 8, seq), BF16),
            pltpu.VMEM((hp, nb, 1, bs), F32),
            pltpu.VMEM((hp, nb, hd + 8, bs), F32),
            pltpu.VMEM((2, pw * hp, bs, bs), F32),
            pltpu.VMEM((2, pw * hp, 1, bs), F32),
            pltpu.VMEM((2, pw * hp, 1, bs), F32),
            pltpu.VMEM((2, pw * hp, bs, bs), BF16),
        ],
    )
    return pl.pallas_call(
        kern,
        out_shape=jax.ShapeDtypeStruct((m, width), BF16),
        grid_spec=grid_spec,
        compiler_params=_cparams("parallel", "parallel"),
        name="moba_attn",
    )(pairs, qk, qk, v, z_act, e_tab)


def _outproj_kernel(u_ref, w_ref, x_ref, g_ref, o_ref):
    acc = jnp.dot(u_ref[...], w_ref[...], preferred_element_type=F32)
    o_ref[...] = x_ref[...] + g_ref[...] * acc


def _outproj(u, w_out, xf, gate, seq, tm=512, tn=1024):
    m, kdim = u.shape
    d = w_out.shape[1]
    b = m // seq
    tps = seq // tm
    return pl.pallas_call(
        _outproj_kernel,
        out_shape=jax.ShapeDtypeStruct((m, d), F32),
        grid=(d // tn, m // tm),
        in_specs=[
            pl.BlockSpec((tm, kdim), lambda n, i: (i, 0)),
            pl.BlockSpec((kdim, tn), lambda n, i: (0, n)),
            pl.BlockSpec((tm, tn), lambda n, i: (i, n)),
            pl.BlockSpec((None, 1, tn), lambda n, i: (i // tps, 0, n)),
        ],
        out_specs=pl.BlockSpec((tm, tn), lambda n, i: (i, n)),
        compiler_params=_cparams("parallel", "parallel"),
        name="outproj",
    )(u, w_out, xf, gate.reshape(b, 1, d))


def _mlstm_qk_kernel(h_ref, w_ref, cw_ref, sc_ref, o_ref, ext_ref, *, tiles_per_seq, conv_k):
    i = pl.program_id(1)
    sub = ROW_SUB

    tiles = _row_tiles(h_ref)
    slots = ext_ref.shape[0]
    last = (len(tiles) - 1) % slots

    @pl.when(i % tiles_per_seq == 0)
    def _():
        ext_ref[last, sub:sub + 8, :] = jnp.zeros((8, ext_ref.shape[2]), F32)

    for r, rows in enumerate(tiles):
        cur, prev = r % slots, (r - 1) % slots if r else last
        acc = jnp.dot(h_ref[rows, :], w_ref[...], preferred_element_type=F32)
        ext_ref[cur, 0:8, :] = ext_ref[prev, sub:sub + 8, :]
        ext_ref[cur, 8:sub + 8, :] = acc
        y = cw_ref[conv_k - 1:conv_k, :] * acc
        for back in range(1, conv_k):
            y = y + cw_ref[conv_k - 1 - back:conv_k - back, :] * ext_ref[cur, 8 - back:8 - back + sub, :]
        o_ref[rows, :] = (_silu(y) * sc_ref[...]).astype(BF16)


def _mlstm_qk_proj(h, w_in, conv_w, qk_width, dk, seq, tm=1024, tn=1024):
    m, kdim = h.shape
    conv_k = conv_w.shape[0]
    tps = seq // tm
    sc = jnp.concatenate([jnp.ones((qk_width,), F32),
                          jnp.full((qk_width,), float(dk) ** -0.5, F32)]).reshape(1, 2 * qk_width)
    return pl.pallas_call(
        functools.partial(_mlstm_qk_kernel, tiles_per_seq=tps, conv_k=conv_k),
        out_shape=jax.ShapeDtypeStruct((m, 2 * qk_width), BF16),
        grid=(2 * qk_width // tn, m // tm),
        in_specs=[
            pl.BlockSpec((tm, kdim), lambda n, i: (i, 0)),
            pl.BlockSpec((kdim, tn), lambda n, i: (0, n)),
            pl.BlockSpec((conv_k, tn), lambda n, i: (0, n)),
            pl.BlockSpec((1, tn), lambda n, i: (0, n)),
        ],
        out_specs=pl.BlockSpec((tm, tn), lambda n, i: (i, n)),
        scratch_shapes=[pltpu.VMEM((2 if (tm // ROW_SUB) % 2 == 0 else 1, ROW_SUB + 8, tn), F32)],
        compiler_params=_cparams("arbitrary", "arbitrary"),
        name="mlstm_qk_proj",
    )(h, w_in, conv_w, sc)


def _mlstm_gz_kernel(h_ref, wo_ref, wz_ref, o_ref):
    for rows in _row_tiles(h_ref):
        a = h_ref[rows, :]
        og = jnp.dot(a, wo_ref[...], preferred_element_type=F32)
        z = jnp.dot(a, wz_ref[...], preferred_element_type=F32)
        o_ref[rows, :] = (_sigmoid(og) * _silu(z)).astype(BF16)


def _mlstm_gz_proj(h, w_in, o_col0, z_col0, width, tm=1024, tn=1024):
    m, kdim = h.shape
    o_off, z_off = o_col0 // tn, z_col0 // tn
    return pl.pallas_call(
        _mlstm_gz_kernel,
        out_shape=jax.ShapeDtypeStruct((m, width), BF16),
        grid=(width // tn, m // tm),
        in_specs=[
            pl.BlockSpec((tm, kdim), lambda n, i: (i, 0)),
            pl.BlockSpec((kdim, tn), lambda n, i: (0, o_off + n)),
            pl.BlockSpec((kdim, tn), lambda n, i: (0, z_off + n)),
        ],
        out_specs=pl.BlockSpec((tm, tn), lambda n, i: (i, n)),
        compiler_params=_cparams("parallel", "parallel"),
        name="mlstm_gz_proj",
    )(h, w_in, w_in)


def _gate_proj_kernel(h_ref, w_ref, b_ref, rows_ref, cols_ref, *, heads, chunk):
    g = jnp.dot(h_ref[...], w_ref[...].astype(BF16), preferred_element_type=F32) + b_ref[...]
    tm = g.shape[0]
    g_t = g.T
    i_rows = g_t[0:heads, :]
    f_rows = g_t[heads:2 * heads, :]
    b = jnp.minimum(f_rows, 0.0) - jnp.log1p(jnp.exp(-jnp.abs(f_rows)))
    pos = lax.broadcasted_iota(jnp.int32, b.shape, 1) % chunk
    shift = 1
    while shift < chunk:
        b = b + jnp.where(pos >= shift, pltpu.roll(b, shift, axis=1), 0.0)
        shift *= 2
    rows = jnp.concatenate([i_rows, b], axis=0)
    rows_ref[...] = rows
    cols_ref[...] = jnp.concatenate([rows, jnp.zeros((LANES - 2 * heads, tm), F32)], axis=0).T


def _gate_proj(h, w_gate, gate_b, batch, seq, tm=1024):
    m, kdim = h.shape
    ng = w_gate.shape[1]
    heads = ng // 2
    tps = seq // tm
    w_pad = jnp.zeros((kdim, LANES), F32).at[:, :ng].set(w_gate)
    b_pad = jnp.zeros((1, LANES), F32).at[0, :ng].set(gate_b)
    kern = functools.partial(_gate_proj_kernel, heads=heads, chunk=MLSTM_CHUNK)
    return pl.pallas_call(
        kern,
        out_shape=(jax.ShapeDtypeStruct((batch, ng, seq), F32), jax.ShapeDtypeStruct((m, LANES), F32)),
        grid=(m // tm,),
        in_specs=[
            pl.BlockSpec((tm, kdim), lambda i: (i, 0)),
            pl.BlockSpec((kdim, LANES), lambda i: (0, 0)),
            pl.BlockSpec((1, LANES), lambda i: (0, 0)),
        ],
        out_specs=(pl.BlockSpec((None, ng, tm), lambda i: (i // tps, 0, i % tps)),
                   pl.BlockSpec((tm, LANES), lambda i: (i, 0))),
        compiler_params=_cparams("parallel"),
        name="mlstm_gate_proj",
    )(h, w_pad, b_pad)


def _mlstm_kernel(q_ref, k_ref, v_ref, gz_ref, rows_ref, cols_ref, nw_ref, u_ref, c_ref, m_ref, *,
                  chunk, heads, hp, dk, dv):
    g = pl.program_id(1)
    c = pl.program_id(2)
    L = chunk
    nt = (((1,), (1,)), ((), ()))

    @pl.when(c == 0)
    def _():
        c_ref[...] = jnp.zeros(c_ref.shape, F32)
        m_ref[...] = jnp.zeros(m_ref.shape, F32)

    t_id = lax.broadcasted_iota(jnp.int32, (L, L), 0)
    s_id = lax.broadcasted_iota(jnp.int32, (L, L), 1)
    causal = s_id <= t_id
    lane = lax.broadcasted_iota(jnp.int32, (L, LANES), 1)
    ones_blk = jnp.where(lane == 0, 1.0, 0.0).astype(BF16)
    cols = cols_ref[...]

    def col_of(idx):
        return jnp.sum(jnp.where(lane == idx, cols, 0.0), axis=1, keepdims=True)

    gate = []
    for hh in range(hp):
        h = g * hp + hh
        i_row = rows_ref[pl.ds(h, 1), :]
        b_row = rows_ref[pl.ds(heads + h, 1), :]
        i_col = col_of(h)
        b_col = col_of(heads + h)
        m_prev = m_ref[hh]
        log_d = jnp.where(causal, b_col - b_row + i_row, -jnp.inf)
        log_inter = b_col + m_prev
        m_t = jnp.maximum(log_inter, jnp.max(log_d, axis=1, keepdims=True))
        d_mat = jnp.exp(log_d - m_t)
        w_inter = jnp.exp(log_inter - m_t)
        b_last = b_row[:, L - 1:L]
        m_new = jnp.maximum(b_last + m_prev, jnp.max(b_last - b_row + i_row, axis=1, keepdims=True))
        decay = jnp.exp(b_last + m_prev - m_new)
        ws_col = jnp.exp(b_last - b_col + i_col - m_new)
        m_ref[hh] = m_new
        gate.append((m_t, d_mat, w_inter, decay, ws_col))

    s_qk = []
    for hh in range(hp):
        q = q_ref[:, hh * dk:(hh + 1) * dk]
        k = k_ref[:, hh * dk:(hh + 1) * dk]
        s_qk.append((lax.dot_general(q, k, nt, preferred_element_type=F32) * gate[hh][1]).astype(BF16))

    for hh in range(hp):
        m_t, _, w_inter, _, _ = gate[hh]
        q_scaled = (w_inter * q_ref[:, hh * dk:(hh + 1) * dk].astype(F32)).astype(BF16)
        v_aug = jnp.concatenate([v_ref[:, hh * dv:(hh + 1) * dv], ones_blk], axis=1)
        lhs = jnp.concatenate([q_scaled, s_qk[hh]], axis=1)
        rhs = jnp.concatenate([c_ref[hh].astype(BF16), v_aug], axis=0)
        num = jnp.dot(lhs, rhs, preferred_element_type=F32)
        d = jnp.maximum(jnp.abs(num[:, dv:dv + 1]), jnp.exp(-m_t))
        nh = num[:, :dv]
        r = lax.rsqrt(jnp.mean(nh * nh, axis=-1, keepdims=True) + NORM_EPS * (d * d))
        hn = nh * r * nw_ref[:, hh * dv:(hh + 1) * dv]
        u_ref[:, hh * dv:(hh + 1) * dv] = (hn * gz_ref[:, hh * dv:(hh + 1) * dv].astype(F32)).astype(BF16)

    for hh in range(hp):
        _, _, _, decay, ws_col = gate[hh]
        k_scaled = (ws_col * k_ref[:, hh * dk:(hh + 1) * dk].astype(F32)).astype(BF16)
        v_aug = jnp.concatenate([v_ref[:, hh * dv:(hh + 1) * dv], ones_blk], axis=1)
        upd = lax.dot_general(k_scaled, v_aug, (((0,), (0,)), ((), ())), preferred_element_type=F32)
        c_ref[hh] = decay * c_ref[hh] + upd


def _mlstm_core(qk, v, gz, rows, cols, norm_w, batch, seq, heads, dk, dv, hp=None):
    m = qk.shape[0]
    hp = heads if hp is None else hp
    L = MLSTM_CHUNK
    nc = seq // L
    groups = heads // hp
    v_width = heads * dv
    k_off = groups
    kern = functools.partial(_mlstm_kernel, chunk=L, heads=heads, hp=hp, dk=dk, dv=dv)
    return pl.pallas_call(
        kern,
        out_shape=jax.ShapeDtypeStruct((m, v_width), BF16),
        grid=(batch, groups, nc),
        in_specs=[
            pl.BlockSpec((L, hp * dk), lambda b, g, c: (b * nc + c, g)),
            pl.BlockSpec((L, hp * dk), lambda b, g, c: (b * nc + c, k_off + g)),
            pl.BlockSpec((L, hp * dv), lambda b, g, c: (b * nc + c, g)),
            pl.BlockSpec((L, hp * dv), lambda b, g, c: (b * nc + c, g)),
            pl.BlockSpec((None, 2 * heads, L), lambda b, g, c: (b, 0, c)),
            pl.BlockSpec((L, LANES), lambda b, g, c: (b * nc + c, 0)),
            pl.BlockSpec((1, hp * dv), lambda b, g, c: (0, g)),
        ],
        out_specs=pl.BlockSpec((L, hp * dv), lambda b, g, c: (b * nc + c, g)),
        scratch_shapes=[pltpu.VMEM((hp, dk, dv + LANES), F32), pltpu.VMEM((hp, 1, 1), F32)],
        compiler_params=_cparams("parallel", "parallel", "arbitrary"),
        name="mlstm_core",
    )(qk, qk, v, gz, rows, cols, norm_w.reshape(1, v_width))


def _moba_layer(xf, h, gate, w_in, q_norm, k_norm, w_out, batch, seq):
    hd = q_norm.shape[0]
    heads = w_in.shape[1] // 4 // hd
    width = heads * hd
    w_in = w_in.astype(BF16)
    qk = _attn_qk_proj(h, w_in, q_norm, k_norm, seq)
    v = _proj_act(h, w_in, 2 * width, width, None, "attn_v_proj")
    z_act = _proj_act(h, w_in, 3 * width, width, _silu, "attn_z_proj")
    u = _moba_attention(qk, v, z_act, batch, seq, heads, hd)
    return _outproj(u, w_out.astype(BF16), xf, gate, seq)


def _mlstm_layer(xf, h, gate, w_in, gate_b, conv_w, norm_w, w_out, batch, seq):
    heads = gate_b.shape[0] // 2
    v_width = w_out.shape[0]
    dv = v_width // heads
    qk_width = conv_w.shape[1] // 2
    dk = qk_width // heads
    n_main = 2 * qk_width + 3 * v_width
    w_main = w_in[:, :n_main].astype(BF16)
    qk = _mlstm_qk_proj(h, w_main, conv_w, qk_width, dk, seq)
    v = _proj_act(h, w_main, 2 * qk_width, v_width, None, "mlstm_v_proj")
    gz = _mlstm_gz_proj(h, w_main, 2 * qk_width + v_width, 2 * qk_width + 2 * v_width, v_width)
    rows, cols = _gate_proj(h, w_in[:, n_main:], gate_b, batch, seq)
    u = _mlstm_core(qk, v, gz, rows, cols, norm_w, batch, seq, heads, dk, dv)
    return _outproj(u, w_out.astype(BF16), xf, gate, seq)


def kernel(x, c, norm_w, ada_w, ada_b, att_w_in, att_q_norm, att_k_norm, att_w_out, mlstm_w_in,
           mlstm_gate_b, mlstm_conv_w, mlstm_norm_w, mlstm_w_out):
    batch, seq, d = x.shape
    depth = norm_w.shape[0]
    mod = _adaln(c, ada_w, ada_b)
    xf = x.reshape(batch * seq, d)
    for layer in range(depth):
        shift, scale, gate = mod[layer, :, :d], mod[layer, :, d:2 * d], mod[layer, :, 2 * d:]
        h = _norm_mod(xf, norm_w[layer], scale, shift, seq)
        j = layer // 2
        if layer % 2 == 0:
            xf = _moba_layer(xf, h, gate, att_w_in[j], att_q_norm[j], att_k_norm[j], att_w_out[j], batch, seq)
        else:
            xf = _mlstm_layer(xf, h, gate, mlstm_w_in[j], mlstm_gate_b[j], mlstm_conv_w[j],
                              mlstm_norm_w[j], mlstm_w_out[j], batch, seq)
    return xf.reshape(batch, seq, d)
```

```python
import functools

import jax
import jax.numpy as jnp
from jax import lax
from jax.experimental import pallas as pl
from jax.experimental.pallas import tpu as pltpu

F32 = jnp.float32
BF16 = jnp.bfloat16
HIGHEST = lax.Precision.HIGHEST

NORM_EPS = 1e-6
MOBA_BLOCK = 256
MOBA_TOPK = 3
ROPE_THETA = 10000.0
MLSTM_CHUNK = 256
MASK_BIAS = -1e30
LANES = 128
LOG2_E = 1.4426950408889634
ROW_SUB = 512

_VMEM_LIMIT = 48 * 1024 * 1024


def _sigmoid(x):
    return 1.0 / (1.0 + jnp.exp(-x))


def _cparams(*sem):
    return pltpu.CompilerParams(dimension_semantics=sem, vmem_limit_bytes=_VMEM_LIMIT)


def _adaln_kernel(c_ref, w_ref, b_ref, o_ref):
    c = c_ref[...]
    cond = c * _sigmoid(c)
    o_ref[...] = jnp.dot(cond, w_ref[...], preferred_element_type=F32, precision=HIGHEST) + b_ref[...]


def _adaln(c, ada_w, ada_b, tn=768):
    depth, d, n3 = ada_w.shape
    b = c.shape[0]
    rows = 8
    c_pad = jnp.zeros((rows, d), F32).at[:b].set(c)
    out = pl.pallas_call(
        _adaln_kernel,
        out_shape=jax.ShapeDtypeStruct((depth, rows, n3), F32),
        grid=(depth, n3 // tn),
        in_specs=[
            pl.BlockSpec((rows, d), lambda l, n: (0, 0)),
            pl.BlockSpec((None, d, tn), lambda l, n: (l, 0, n)),
            pl.BlockSpec((None, 1, tn), lambda l, n: (l, 0, n)),
        ],
        out_specs=pl.BlockSpec((None, rows, tn), lambda l, n: (l, 0, n)),
        compiler_params=_cparams("parallel", "parallel"),
        name="adaln_mod",
    )(c_pad, ada_w, ada_b.reshape(depth, 1, n3))
    return out[:, :b]


def _norm_mod_kernel(x_ref, nw_ref, sc_ref, sh_ref, o_ref):
    x = x_ref[...]
    ms = jnp.mean(x * x, axis=-1, keepdims=True)
    y = x * lax.rsqrt(ms + NORM_EPS) * nw_ref[...]
    o_ref[...] = (y * (1.0 + sc_ref[...]) + sh_ref[...]).astype(BF16)


def _norm_mod(xf, nw, scale, shift, seq, ts=512):
    m, d = xf.shape
    b = m // seq
    tps = seq // ts
    return pl.pallas_call(
        _norm_mod_kernel,
        out_shape=jax.ShapeDtypeStruct((m, d), BF16),
        grid=(m // ts,),
        in_specs=[
            pl.BlockSpec((ts, d), lambda i: (i, 0)),
            pl.BlockSpec((1, d), lambda i: (0, 0)),
            pl.BlockSpec((None, 1, d), lambda i: (i // tps, 0, 0)),
            pl.BlockSpec((None, 1, d), lambda i: (i // tps, 0, 0)),
        ],
        out_specs=pl.BlockSpec((ts, d), lambda i: (i, 0)),
        compiler_params=_cparams("parallel"),
        name="norm_mod",
    )(xf, nw.reshape(1, d), scale.reshape(b, 1, d), shift.reshape(b, 1, d))


def _row_tiles(a_ref):
    return [slice(r, r + ROW_SUB) for r in range(0, a_ref.shape[0], ROW_SUB)]


def _silu(x):
    return x * _sigmoid(x)


def _mm(a, w, w_t):
    dims = (((1,), (1,)), ((), ())) if w_t else (((1,), (0,)), ((), ()))
    return lax.dot_general(a, w, dims, preferred_element_type=F32)


def _w_spec(kdim, tn, off, w_t):
    if w_t:
        return pl.BlockSpec((tn, kdim), lambda n, i: (off + n, 0))
    return pl.BlockSpec((kdim, tn), lambda n, i: (0, off + n))


def _proj_act_kernel(h_ref, w_ref, o_ref, *, act, w_t):
    for rows in _row_tiles(h_ref):
        acc = _mm(h_ref[rows, :], w_ref[...], w_t)
        o_ref[rows, :] = (acc if act is None else act(acc)).astype(BF16)


def _proj_act(h, w, col0, width, act, name, w_t=False, tm=1024, tn=1024):
    m, kdim = h.shape
    return pl.pallas_call(
        functools.partial(_proj_act_kernel, act=act, w_t=w_t),
        out_shape=jax.ShapeDtypeStruct((m, width), BF16),
        grid=(width // tn, m // tm),
        in_specs=[
            pl.BlockSpec((tm, kdim), lambda n, i: (i, 0)),
            _w_spec(kdim, tn, col0 // tn, w_t),
        ],
        out_specs=pl.BlockSpec((tm, tn), lambda n, i: (i, n)),
        compiler_params=_cparams("parallel", "parallel"),
        name=name,
    )(h, w)


def _staged(tiles, issue, finish):
    issue(0)
    for r in range(len(tiles)):
        if r + 1 < len(tiles):
            issue(r + 1)
        finish(r)


def _attn_qk_kernel(h_ref, w_ref, cos_ref, sin_ref, nw_ref, sc_ref, o_ref, acc_ref, *, head_dim):
    tiles = _row_tiles(h_ref)

    def issue(r):
        acc_ref[r % 2] = jnp.dot(h_ref[tiles[r], :], w_ref[...], preferred_element_type=F32)

    def finish(r):
        rows = tiles[r]
        cos = cos_ref[rows, :]
        sin = sin_ref[rows, :]
        for hh in range(acc_ref.shape[2] // head_dim):
            sl = slice(hh * head_dim, (hh + 1) * head_dim)
            xh = acc_ref[r % 2, :, sl]
            ms = jnp.mean(xh * xh, axis=-1, keepdims=True)
            y = xh * lax.rsqrt(ms + NORM_EPS) * nw_ref[:, sl]
            rot = y * cos + pltpu.roll(y, head_dim // 2, axis=1) * sin
            o_ref[rows, sl] = (rot * sc_ref[:, sl]).astype(BF16)

    _staged(tiles, issue, finish)


def _attn_qk_proj(h, w_in, q_norm, k_norm, seq, tm=2048, tn=1024):
    m, kdim = h.shape
    width = w_in.shape[1] // 4
    hd = q_norm.shape[0]
    heads = width // hd
    tps = seq // tm
    half = hd // 2
    inv = ROPE_THETA ** (-jnp.arange(half, dtype=F32) / half)
    ang = jnp.arange(seq, dtype=F32)[:, None] * inv[None, :]
    cos = jnp.cos(ang)
    sin = jnp.sin(ang)
    cos_t = jnp.concatenate([cos, cos], axis=-1)
    sin_t = jnp.concatenate([-sin, sin], axis=-1)
    nw = jnp.concatenate([jnp.tile(q_norm, heads), jnp.tile(k_norm, heads)]).reshape(1, 2 * width)
    q_scale = float(hd) ** -0.5 * LOG2_E
    sc = jnp.concatenate([jnp.full((width,), q_scale, F32), jnp.ones((width,), F32)]).reshape(1, 2 * width)
    return pl.pallas_call(
        functools.partial(_attn_qk_kernel, head_dim=hd),
        out_shape=jax.ShapeDtypeStruct((m, 2 * width), BF16),
        grid=(2 * width // tn, m // tm),
        in_specs=[
            pl.BlockSpec((tm, kdim), lambda n, i: (i, 0)),
            pl.BlockSpec((kdim, tn), lambda n, i: (0, n)),
            pl.BlockSpec((tm, hd), lambda n, i: (i % tps, 0)),
            pl.BlockSpec((tm, hd), lambda n, i: (i % tps, 0)),
            pl.BlockSpec((1, tn), lambda n, i: (0, n)),
            pl.BlockSpec((1, tn), lambda n, i: (0, n)),
        ],
        out_specs=pl.BlockSpec((tm, tn), lambda n, i: (i, n)),
        scratch_shapes=[pltpu.VMEM((2, ROW_SUB, tn), F32)],
        compiler_params=_cparams("parallel", "parallel"),
        name="attn_qk_proj",
    )(h, w_in, cos_t, sin_t, nw, sc)


def _moba_kernel(pairs_ref, q_ref, k_ref, v_ref, z_ref, e_ref, u_ref,
                 qaugt_ref, kaug_ref, vt_ref, m_ref, acc_ref, sbuf_ref, cbuf_ref, abuf_ref, pbuf_ref, *,
                 bs, nb, topk, hd, hp, pw, n_pairs):
    seq = nb * bs
    nt = (((1,), (1,)), ((), ()))

    for h in range(hp):
        hs = slice(h * hd, (h + 1) * hd)
        kf = k_ref[:, hs].astype(F32)
        kmean = jnp.sum(kf.reshape(nb, bs, hd), axis=1) * (1.0 / bs)
        g_t = lax.dot_general(kmean, q_ref[:, hs].astype(F32), nt, preferred_element_type=F32,
                              precision=HIGHEST)
        jj = lax.broadcasted_iota(jnp.int32, (nb, seq), 0)
        qb = lax.broadcasted_iota(jnp.int32, (nb, seq), 1) // bs
        rank = jnp.zeros((nb, seq), jnp.int32)
        for jp in range(nb):
            row = g_t[jp:jp + 1, :]
            beats = (row > g_t) | ((row == g_t) & (jp < jj))
            rank = rank + jnp.where(beats & (jp < qb), 1, 0)
        allowed = ((jj < qb) & (rank < topk)) | (jj == qb)
        bias_t = jnp.where(allowed, 0.0, MASK_BIAS).astype(BF16)
        qaugt_ref[h, hd:hd + nb, :] = bias_t
        qaugt_ref[h, hd + nb:hd + LANES, :] = jnp.zeros((LANES - nb, seq), BF16)
        kaug_ref[h, :, 0:hd] = k_ref[:, hs]
        kaug_ref[h, :, hd:hd + LANES] = e_ref[...]
        for jb in range(nb):
            rs = slice(jb * bs, (jb + 1) * bs)
            qaugt_ref[h, 0:hd, rs] = q_ref[rs, hs].astype(F32).T.astype(BF16)
            vt_ref[h, 0:hd, rs] = v_ref[rs, hs].astype(F32).T.astype(BF16)
        ones_row = lax.broadcasted_iota(jnp.int32, (8, seq), 0) == 0
        vt_ref[h, hd:hd + 8, :] = jnp.where(ones_row, 1.0, 0.0).astype(BF16)
    m_ref[...] = jnp.full(m_ref.shape, -jnp.inf, F32)
    acc_ref[...] = jnp.zeros(acc_ref.shape, F32)

    key_id = lax.broadcasted_iota(jnp.int32, (bs, bs), 0)
    qry_id = lax.broadcasted_iota(jnp.int32, (bs, bs), 1)
    causal = key_id <= qry_id
    nu = pw * hp

    def units(pair, t):
        return [(h,) + pair(t * pw + u) for u in range(pw) for h in range(hp)]

    def stage_scores(us, slot, diag):
        for n, (h, i, j) in enumerate(us):
            q0 = pl.multiple_of(i * bs, bs)
            j0 = pl.multiple_of(j * bs, bs)
            s = jnp.dot(kaug_ref[h, pl.ds(j0, bs), :], qaugt_ref[h, :, pl.ds(q0, bs)],
                        preferred_element_type=F32)
            if diag:
                s = jnp.where(causal, s, -jnp.inf)
            sbuf_ref[slot, n] = s
            cbuf_ref[slot, n] = jnp.max(s, axis=0, keepdims=True)

    def stage_softmax(us, slot):
        for n, (h, i, _) in enumerate(us):
            m_old = m_ref[h, i]
            m_new = jnp.maximum(m_old, cbuf_ref[slot, n])
            abuf_ref[slot, n] = jnp.exp2(m_old - m_new)
            m_ref[h, i] = m_new
            pbuf_ref[slot, n] = jnp.exp2(sbuf_ref[slot, n] - m_new).astype(BF16)

    def stage_pv(us, slot):
        for n, (h, i, j) in enumerate(us):
            j0 = pl.multiple_of(j * bs, bs)
            pv = jnp.dot(vt_ref[h, :, pl.ds(j0, bs)], pbuf_ref[slot, n], preferred_element_type=F32)
            acc_ref[h, i] = abuf_ref[slot, n] * acc_ref[h, i] + pv

    def run_pairs(pair, count, diag):
        steps = count // pw
        pbuf_ref[1] = jnp.zeros(pbuf_ref.shape[1:], BF16)
        abuf_ref[1] = jnp.ones(abuf_ref.shape[1:], F32)
        stage_scores(units(pair, 0), 0, diag)

        def step(t, slot):
            stage_scores(units(pair, t), slot, diag)
            stage_softmax(units(pair, t - 1), 1 - slot)
            stage_pv(units(pair, jnp.maximum(t - 2, 0)), slot)

        def two_steps(t2, _):
            step(2 * t2 + 1, 1)
            step(2 * t2 + 2, 0)
            return 0

        lax.fori_loop(0, (steps - 2) // 2, two_steps, 0)
        step(steps - 1, 1)
        stage_softmax(units(pair, steps - 1), 1)
        stage_pv(units(pair, steps - 2), 0)
        stage_pv(units(pair, steps - 1), 1)

    run_pairs(lambda t: (t, t), nb, True)
    if n_pairs > 0:
        run_pairs(lambda t: (pairs_ref[0, t], pairs_ref[1, t]), n_pairs, False)

    def finalize(i, _):
        q0 = pl.multiple_of(i * bs, bs)
        for h in range(hp):
            hs = slice(h * hd, (h + 1) * hd)
            acc = acc_ref[h, i]
            o = (acc[0:hd, :] / acc[hd:hd + 1, :]).T
            u_ref[pl.ds(q0, bs), hs] = (o * z_ref[pl.ds(q0, bs), hs].astype(F32)).astype(BF16)
        return 0

    lax.fori_loop(0, nb, finalize, 0)


def _moba_attention(qk, v, z_act, batch, seq, heads, hd, hp=1):
    m = qk.shape[0]
    bs = MOBA_BLOCK
    nb = seq // bs
    width = heads * hd
    groups = heads // hp
    e_tab = (jnp.arange(seq)[:, None] // bs == jnp.arange(LANES)[None, :]).astype(BF16)
    pair_list = [(i, j) for j in range(nb) for i in range(j + 1, nb)]
    n_pairs = len(pair_list)
    pairs = jnp.asarray(pair_list or [(0, 0)], jnp.int32).T
    pw = next(w for w in (4, 2, 1) if nb % (2 * w) == 0 and n_pairs % (2 * w) == 0)
    kern = functools.partial(_moba_kernel, bs=bs, nb=nb, topk=min(MOBA_TOPK, nb), hd=hd, hp=hp, pw=pw,
                             n_pairs=n_pairs)
    blk = (seq, hp * hd)
    grid_spec = pltpu.PrefetchScalarGridSpec(
        num_scalar_prefetch=1,
        grid=(batch, groups),
        in_specs=[
            pl.BlockSpec(blk, lambda b, g, pr: (b, g)),
            pl.BlockSpec(blk, lambda b, g, pr: (b, groups + g)),
            pl.BlockSpec(blk, lambda b, g, pr: (b, g)),
            pl.BlockSpec(blk, lambda b, g, pr: (b, g)),
            pl.BlockSpec((seq, LANES), lambda b, g, pr: (0, 0)),
        ],
        out_specs=pl.BlockSpec(blk, lambda b, g, pr: (b, g)),
        scratch_shapes=[
            pltpu.VMEM((hp, hd + LANES, seq), BF16),
            pltpu.VMEM((hp, seq, hd + LANES), BF16),
            pltpu.VMEM((hp, hd + 8, seq), BF16),
            pltpu.VMEM((hp, nb, 1, bs), F32),
            pltpu.VMEM((hp, nb, hd + 8, bs), F32),
            pltpu.VMEM((2, pw * hp, bs, bs), F32),
            pltpu.VMEM((2, pw * hp, 1, bs), F32),
            pltpu.VMEM((2, pw * hp, 1, bs), F32),
            pltpu.VMEM((2, pw * hp, bs, bs), BF16),
        ],
    )
    return pl.pallas_call(
        kern,
        out_shape=jax.ShapeDtypeStruct((m, width), BF16),
        grid_spec=grid_spec,
        compiler_params=_cparams("parallel", "parallel"),
        name="moba_attn",
    )(pairs, qk, qk, v, z_act, e_tab)


def _outproj_kernel(u_ref, w_ref, x_ref, g_ref, o_ref):
    acc = jnp.dot(u_ref[...], w_ref[...], preferred_element_type=F32)
    o_ref[...] = x_ref[...] + g_ref[...] * acc


def _outproj(u, w_out, xf, gate, seq, tm=512, tn=1024):
    m, kdim = u.shape
    d = w_out.shape[1]
    b = m // seq
    tps = seq // tm
    return pl.pallas_call(
        _outproj_kernel,
        out_shape=jax.ShapeDtypeStruct((m, d), F32),
        grid=(d // tn, m // tm),
        in_specs=[
            pl.BlockSpec((tm, kdim), lambda n, i: (i, 0)),
            pl.BlockSpec((kdim, tn), lambda n, i: (0, n)),
            pl.BlockSpec((tm, tn), lambda n, i: (i, n)),
            pl.BlockSpec((None, 1, tn), lambda n, i: (i // tps, 0, n)),
        ],
        out_specs=pl.BlockSpec((tm, tn), lambda n, i: (i, n)),
        compiler_params=_cparams("parallel", "parallel"),
        name="outproj",
    )(u, w_out, xf, gate.reshape(b, 1, d))


def _mlstm_qk_kernel(h_ref, w_ref, cw_ref, sc_ref, o_ref, ext_ref, *, tiles_per_seq, conv_k):
    i = pl.program_id(1)
    sub = ROW_SUB
    tiles = _row_tiles(h_ref)

    @pl.when(i % tiles_per_seq == 0)
    def _():
        ext_ref[0, 0:8, :] = jnp.zeros((8, ext_ref.shape[2]), F32)

    def issue(r):
        ext_ref[r % 2, 8:sub + 8, :] = _mm(h_ref[tiles[r], :], w_ref[...], False)

    def finish(r):
        cur = r % 2
        y = cw_ref[conv_k - 1:conv_k, :] * ext_ref[cur, 8:sub + 8, :]
        for back in range(1, conv_k):
            y = y + cw_ref[conv_k - 1 - back:conv_k - back, :] * ext_ref[cur, 8 - back:8 - back + sub, :]
        o_ref[tiles[r], :] = (_silu(y) * sc_ref[...]).astype(BF16)
        ext_ref[1 - cur, 0:8, :] = ext_ref[cur, sub:sub + 8, :]

    _staged(tiles, issue, finish)


def _mlstm_qk_proj(h, w_in, conv_w, qk_width, dk, seq, tm=2048, tn=1024):
    m, kdim = h.shape
    conv_k = conv_w.shape[0]
    tps = seq // tm
    sc = jnp.concatenate([jnp.ones((qk_width,), F32),
                          jnp.full((qk_width,), float(dk) ** -0.5, F32)]).reshape(1, 2 * qk_width)
    return pl.pallas_call(
        functools.partial(_mlstm_qk_kernel, tiles_per_seq=tps, conv_k=conv_k),
        out_shape=jax.ShapeDtypeStruct((m, 2 * qk_width), BF16),
        grid=(2 * qk_width // tn, m // tm),
        in_specs=[
            pl.BlockSpec((tm, kdim), lambda n, i: (i, 0)),
            _w_spec(kdim, tn, 0, False),
            pl.BlockSpec((conv_k, tn), lambda n, i: (0, n)),
            pl.BlockSpec((1, tn), lambda n, i: (0, n)),
        ],
        out_specs=pl.BlockSpec((tm, tn), lambda n, i: (i, n)),
        scratch_shapes=[pltpu.VMEM((2, ROW_SUB + 8, tn), F32)],
        compiler_params=_cparams("arbitrary", "arbitrary"),
        name="mlstm_qk_proj",
    )(h, w_in, conv_w, sc)


def _mlstm_gz_kernel(h_ref, wo_ref, wz_ref, o_ref):
    for rows in _row_tiles(h_ref):
        a = h_ref[rows, :]
        og = _mm(a, wo_ref[...], False)
        z = _mm(a, wz_ref[...], False)
        o_ref[rows, :] = (_sigmoid(og) * _silu(z)).astype(BF16)


def _mlstm_gz_proj(h, w_in, o_col0, z_col0, width, tm=1024, tn=1024):
    m, kdim = h.shape
    return pl.pallas_call(
        _mlstm_gz_kernel,
        out_shape=jax.ShapeDtypeStruct((m, width), BF16),
        grid=(width // tn, m // tm),
        in_specs=[
            pl.BlockSpec((tm, kdim), lambda n, i: (i, 0)),
            _w_spec(kdim, tn, o_col0 // tn, False),
            _w_spec(kdim, tn, z_col0 // tn, False),
        ],
        out_specs=pl.BlockSpec((tm, tn), lambda n, i: (i, n)),
        compiler_params=_cparams("parallel", "parallel"),
        name="mlstm_gz_proj",
    )(h, w_in, w_in)


def _gate_proj_kernel(h_ref, w_ref, b_ref, rows_ref, cols_ref, *, heads, chunk):
    g = jnp.dot(h_ref[...], w_ref[...].astype(BF16), preferred_element_type=F32) + b_ref[...]
    tm = g.shape[0]
    g_t = g.T
    i_rows = g_t[0:heads, :]
    f_rows = g_t[heads:2 * heads, :]
    b = jnp.minimum(f_rows, 0.0) - jnp.log1p(jnp.exp(-jnp.abs(f_rows)))
    pos = lax.broadcasted_iota(jnp.int32, b.shape, 1) % chunk
    shift = 1
    while shift < chunk:
        b = b + jnp.where(pos >= shift, pltpu.roll(b, shift, axis=1), 0.0)
        shift *= 2
    rows = jnp.concatenate([i_rows, b], axis=0)
    rows_ref[...] = rows
    cols_ref[...] = jnp.concatenate([rows, jnp.zeros((LANES - 2 * heads, tm), F32)], axis=0).T


def _gate_proj(h, w_gate, gate_b, batch, seq, tm=1024):
    m, kdim = h.shape
    ng = w_gate.shape[1]
    heads = ng // 2
    tps = seq // tm
    w_pad = jnp.zeros((kdim, LANES), F32).at[:, :ng].set(w_gate)
    b_pad = jnp.zeros((1, LANES), F32).at[0, :ng].set(gate_b)
    kern = functools.partial(_gate_proj_kernel, heads=heads, chunk=MLSTM_CHUNK)
    return pl.pallas_call(
        kern,
        out_shape=(jax.ShapeDtypeStruct((batch, ng, seq), F32), jax.ShapeDtypeStruct((m, LANES), F32)),
        grid=(m // tm,),
        in_specs=[
            pl.BlockSpec((tm, kdim), lambda i: (i, 0)),
            pl.BlockSpec((kdim, LANES), lambda i: (0, 0)),
            pl.BlockSpec((1, LANES), lambda i: (0, 0)),
        ],
        out_specs=(pl.BlockSpec((None, ng, tm), lambda i: (i // tps, 0, i % tps)),
                   pl.BlockSpec((tm, LANES), lambda i: (i, 0))),
        compiler_params=_cparams("parallel"),
        name="mlstm_gate_proj",
    )(h, w_pad, b_pad)


def _mlstm_kernel(q_ref, k_ref, v_ref, gz_ref, rows_ref, cols_ref, nw_ref, u_ref, c_ref, m_ref, *,
                  chunk, heads, hp, dk, dv):
    g = pl.program_id(1)
    c = pl.program_id(2)
    L = chunk
    nt = (((1,), (1,)), ((), ()))

    @pl.when(c == 0)
    def _():
        c_ref[...] = jnp.zeros(c_ref.shape, F32)
        m_ref[...] = jnp.zeros(m_ref.shape, F32)

    t_id = lax.broadcasted_iota(jnp.int32, (L, L), 0)
    s_id = lax.broadcasted_iota(jnp.int32, (L, L), 1)
    causal = s_id <= t_id
    lane = lax.broadcasted_iota(jnp.int32, (L, LANES), 1)
    ones_blk = jnp.where(lane == 0, 1.0, 0.0).astype(BF16)
    cols = cols_ref[...]

    def col_of(idx):
        return jnp.sum(jnp.where(lane == idx, cols, 0.0), axis=1, keepdims=True)

    gate = []
    for hh in range(hp):
        h = g * hp + hh
        i_row = rows_ref[pl.ds(h, 1), :]
        b_row = rows_ref[pl.ds(heads + h, 1), :]
        i_col = col_of(h)
        b_col = col_of(heads + h)
        m_prev = m_ref[hh]
        log_d = jnp.where(causal, b_col - b_row + i_row, -jnp.inf)
        log_inter = b_col + m_prev
        m_t = jnp.maximum(log_inter, jnp.max(log_d, axis=1, keepdims=True))
        d_mat = jnp.exp(log_d - m_t)
        w_inter = jnp.exp(log_inter - m_t)
        b_last = b_row[:, L - 1:L]
        m_new = jnp.maximum(b_last + m_prev, jnp.max(b_last - b_row + i_row, axis=1, keepdims=True))
        decay = jnp.exp(b_last + m_prev - m_new)
        ws_col = jnp.exp(b_last - b_col + i_col - m_new)
        m_ref[hh] = m_new
        gate.append((m_t, d_mat, w_inter, decay, ws_col))

    s_qk = []
    for hh in range(hp):
        q = q_ref[:, hh * dk:(hh + 1) * dk]
        k = k_ref[:, hh * dk:(hh + 1) * dk]
        s_qk.append((lax.dot_general(q, k, nt, preferred_element_type=F32) * gate[hh][1]).astype(BF16))

    for hh in range(hp):
        m_t, _, w_inter, _, _ = gate[hh]
        q_scaled = (w_inter * q_ref[:, hh * dk:(hh + 1) * dk].astype(F32)).astype(BF16)
        v_aug = jnp.concatenate([v_ref[:, hh * dv:(hh + 1) * dv], ones_blk], axis=1)
        lhs = jnp.concatenate([q_scaled, s_qk[hh]], axis=1)
        rhs = jnp.concatenate([c_ref[hh].astype(BF16), v_aug], axis=0)
        num = jnp.dot(lhs, rhs, preferred_element_type=F32)
        d = jnp.maximum(jnp.abs(num[:, dv:dv + 1]), jnp.exp(-m_t))
        nh = num[:, :dv]
        r = lax.rsqrt(jnp.mean(nh * nh, axis=-1, keepdims=True) + NORM_EPS * (d * d))
        hn = nh * r * nw_ref[:, hh * dv:(hh + 1) * dv]
        u_ref[:, hh * dv:(hh + 1) * dv] = (hn * gz_ref[:, hh * dv:(hh + 1) * dv].astype(F32)).astype(BF16)

    for hh in range(hp):
        _, _, _, decay, ws_col = gate[hh]
        k_scaled = (ws_col * k_ref[:, hh * dk:(hh + 1) * dk].astype(F32)).astype(BF16)
        v_aug = jnp.concatenate([v_ref[:, hh * dv:(hh + 1) * dv], ones_blk], axis=1)
        upd = lax.dot_general(k_scaled, v_aug, (((0,), (0,)), ((), ())), preferred_element_type=F32)
        c_ref[hh] = decay * c_ref[hh] + upd


def _mlstm_core(qk, v, gz, rows, cols, norm_w, batch, seq, heads, dk, dv, hp=None):
    m = qk.shape[0]
    hp = heads if hp is None else hp
    L = MLSTM_CHUNK
    nc = seq // L
    groups = heads // hp
    v_width = heads * dv
    k_off = groups
    kern = functools.partial(_mlstm_kernel, chunk=L, heads=heads, hp=hp, dk=dk, dv=dv)
    return pl.pallas_call(
        kern,
        out_shape=jax.ShapeDtypeStruct((m, v_width), BF16),
        grid=(batch, groups, nc),
        in_specs=[
            pl.BlockSpec((L, hp * dk), lambda b, g, c: (b * nc + c, g)),
            pl.BlockSpec((L, hp * dk), lambda b, g, c: (b * nc + c, k_off + g)),
            pl.BlockSpec((L, hp * dv), lambda b, g, c: (b * nc + c, g)),
            pl.BlockSpec((L, hp * dv), lambda b, g, c: (b * nc + c, g)),
            pl.BlockSpec((None, 2 * heads, L), lambda b, g, c: (b, 0, c)),
            pl.BlockSpec((L, LANES), lambda b, g, c: (b * nc + c, 0)),
            pl.BlockSpec((1, hp * dv), lambda b, g, c: (0, g)),
        ],
        out_specs=pl.BlockSpec((L, hp * dv), lambda b, g, c: (b * nc + c, g)),
        scratch_shapes=[pltpu.VMEM((hp, dk, dv + LANES), F32), pltpu.VMEM((hp, 1, 1), F32)],
        compiler_params=_cparams("parallel", "parallel", "arbitrary"),
        name="mlstm_core",
    )(qk, qk, v, gz, rows, cols, norm_w.reshape(1, v_width))


def _moba_layer(xf, h, gate, w_in, q_norm, k_norm, w_out, batch, seq):
    hd = q_norm.shape[0]
    heads = w_in.shape[1] // 4 // hd
    width = heads * hd
    w_in = w_in.astype(BF16)
    qk = _attn_qk_proj(h, w_in, q_norm, k_norm, seq)
    v = _proj_act(h, w_in, 2 * width, width, None, "attn_v_proj")
    z_act = _proj_act(h, w_in, 3 * width, width, _silu, "attn_z_proj")
    u = _moba_attention(qk, v, z_act, batch, seq, heads, hd)
    return _outproj(u, w_out.astype(BF16), xf, gate, seq)


def _mlstm_layer(xf, h, gate, w_in, gate_b, conv_w, norm_w, w_out, batch, seq):
    heads = gate_b.shape[0] // 2
    v_width = w_out.shape[0]
    dv = v_width // heads
    qk_width = conv_w.shape[1] // 2
    dk = qk_width // heads
    n_main = 2 * qk_width + 3 * v_width
    w_main = w_in[:, :n_main].astype(BF16)
    qk = _mlstm_qk_proj(h, w_main, conv_w, qk_width, dk, seq)
    v = _proj_act(h, w_main, 2 * qk_width, v_width, None, "mlstm_v_proj")
    gz = _mlstm_gz_proj(h, w_main, 2 * qk_width + v_width, 2 * qk_width + 2 * v_width, v_width)
    rows, cols = _gate_proj(h, w_in[:, n_main:], gate_b, batch, seq)
    u = _mlstm_core(qk, v, gz, rows, cols, norm_w, batch, seq, heads, dk, dv)
    return _outproj(u, w_out.astype(BF16), xf, gate, seq)


def kernel(x, c, norm_w, ada_w, ada_b, att_w_in, att_q_norm, att_k_norm, att_w_out, mlstm_w_in,
           mlstm_gate_b, mlstm_conv_w, mlstm_norm_w, mlstm_w_out):
    batch, seq, d = x.shape
    depth = norm_w.shape[0]
    mod = _adaln(c, ada_w, ada_b)
    xf = x.reshape(batch * seq, d)
    for layer in range(depth):
        shift, scale, gate = mod[layer, :, :d], mod[layer, :, d:2 * d], mod[layer, :, 2 * d:]
        h = _norm_mod(xf, norm_w[layer], scale, shift, seq)
        j = layer // 2
        if layer % 2 == 0:
            xf = _moba_layer(xf, h, gate, att_w_in[j], att_q_norm[j], att_k_norm[j], att_w_out[j], batch, seq)
        else:
            xf = _mlstm_layer(xf, h, gate, mlstm_w_in[j], mlstm_gate_b[j], mlstm_conv_w[j],
                              mlstm_norm_w[j], mlstm_w_out[j], batch, seq)
    return xf.reshape(batch, seq, d)
```

```python
import functools

import jax
import jax.numpy as jnp
from jax import lax
from jax.experimental import pallas as pl
from jax.experimental.pallas import tpu as pltpu

F32 = jnp.float32
BF16 = jnp.bfloat16
HIGHEST = lax.Precision.HIGHEST

NORM_EPS = 1e-6
MOBA_BLOCK = 256
MOBA_TOPK = 3
ROPE_THETA = 10000.0
MLSTM_CHUNK = 256
MASK_BIAS = -1e30
LANES = 128
LOG2_E = 1.4426950408889634
ROW_SUB = 512

_VMEM_LIMIT = 48 * 1024 * 1024


def _sigmoid(x):
    return 1.0 / (1.0 + jnp.exp(-x))


def _cparams(*sem):
    return pltpu.CompilerParams(dimension_semantics=sem, vmem_limit_bytes=_VMEM_LIMIT)


def _adaln_kernel(c_ref, w_ref, b_ref, o_ref):
    c = c_ref[...]
    cond = c * _sigmoid(c)
    o_ref[...] = jnp.dot(cond, w_ref[...], preferred_element_type=F32, precision=HIGHEST) + b_ref[...]


def _adaln(c, ada_w, ada_b, tn=768):
    depth, d, n3 = ada_w.shape
    b = c.shape[0]
    rows = 8
    c_pad = jnp.zeros((rows, d), F32).at[:b].set(c)
    out = pl.pallas_call(
        _adaln_kernel,
        out_shape=jax.ShapeDtypeStruct((depth, rows, n3), F32),
        grid=(depth, n3 // tn),
        in_specs=[
            pl.BlockSpec((rows, d), lambda l, n: (0, 0)),
            pl.BlockSpec((None, d, tn), lambda l, n: (l, 0, n)),
            pl.BlockSpec((None, 1, tn), lambda l, n: (l, 0, n)),
        ],
        out_specs=pl.BlockSpec((None, rows, tn), lambda l, n: (l, 0, n)),
        compiler_params=_cparams("parallel", "parallel"),
        name="adaln_mod",
    )(c_pad, ada_w, ada_b.reshape(depth, 1, n3))
    return out[:, :b]


def _norm_mod_kernel(x_ref, nw_ref, sc_ref, sh_ref, o_ref):
    x = x_ref[...]
    ms = jnp.mean(x * x, axis=-1, keepdims=True)
    y = x * lax.rsqrt(ms + NORM_EPS) * nw_ref[...]
    o_ref[...] = (y * (1.0 + sc_ref[...]) + sh_ref[...]).astype(BF16)


def _norm_mod(xf, nw, scale, shift, seq, ts=512):
    m, d = xf.shape
    b = m // seq
    tps = seq // ts
    return pl.pallas_call(
        _norm_mod_kernel,
        out_shape=jax.ShapeDtypeStruct((m, d), BF16),
        grid=(m // ts,),
        in_specs=[
            pl.BlockSpec((ts, d), lambda i: (i, 0)),
            pl.BlockSpec((1, d), lambda i: (0, 0)),
            pl.BlockSpec((None, 1, d), lambda i: (i // tps, 0, 0)),
            pl.BlockSpec((None, 1, d), lambda i: (i // tps, 0, 0)),
        ],
        out_specs=pl.BlockSpec((ts, d), lambda i: (i, 0)),
        compiler_params=_cparams("parallel"),
        name="norm_mod",
    )(xf, nw.reshape(1, d), scale.reshape(b, 1, d), shift.reshape(b, 1, d))


def _row_tiles(a_ref):
    return [slice(r, r + ROW_SUB) for r in range(0, a_ref.shape[0], ROW_SUB)]


def _silu(x):
    return x * _sigmoid(x)


def _mm(a, w, w_t):
    dims = (((1,), (1,)), ((), ())) if w_t else (((1,), (0,)), ((), ()))
    return lax.dot_general(a, w, dims, preferred_element_type=F32)


def _w_spec(kdim, tn, off, w_t):
    if w_t:
        return pl.BlockSpec((tn, kdim), lambda n, i: (off + n, 0))
    return pl.BlockSpec((kdim, tn), lambda n, i: (0, off + n))


def _proj_act_kernel(h_ref, w_ref, o_ref, *, act, w_t):
    for rows in _row_tiles(h_ref):
        acc = _mm(h_ref[rows, :], w_ref[...], w_t)
        o_ref[rows, :] = (acc if act is None else act(acc)).astype(BF16)


def _proj_act(h, w, col0, width, act, name, w_t=False, tm=1024, tn=1024):
    m, kdim = h.shape
    return pl.pallas_call(
        functools.partial(_proj_act_kernel, act=act, w_t=w_t),
        out_shape=jax.ShapeDtypeStruct((m, width), BF16),
        grid=(width // tn, m // tm),
        in_specs=[
            pl.BlockSpec((tm, kdim), lambda n, i: (i, 0)),
            _w_spec(kdim, tn, col0 // tn, w_t),
        ],
        out_specs=pl.BlockSpec((tm, tn), lambda n, i: (i, n)),
        compiler_params=_cparams("parallel", "parallel"),
        name=name,
    )(h, w)


def _staged(tiles, issue, finish):
    issue(0)
    for r in range(len(tiles)):
        if r + 1 < len(tiles):
            issue(r + 1)
        finish(r)


def _attn_qk_kernel(h_ref, w_ref, cos_ref, sin_ref, nw_ref, sc_ref, o_ref, acc_ref, *, head_dim):
    tiles = _row_tiles(h_ref)

    def issue(r):
        acc_ref[r % 2] = jnp.dot(h_ref[tiles[r], :], w_ref[...], preferred_element_type=F32)

    def finish(r):
        rows = tiles[r]
        cos = cos_ref[rows, :]
        sin = sin_ref[rows, :]
        for hh in range(acc_ref.shape[2] // head_dim):
            sl = slice(hh * head_dim, (hh + 1) * head_dim)
            xh = acc_ref[r % 2, :, sl]
            ms = jnp.mean(xh * xh, axis=-1, keepdims=True)
            y = xh * lax.rsqrt(ms + NORM_EPS) * nw_ref[:, sl]
            rot = y * cos + pltpu.roll(y, head_dim // 2, axis=1) * sin
            o_ref[rows, sl] = (rot * sc_ref[:, sl]).astype(BF16)

    _staged(tiles, issue, finish)


def _attn_qk_proj(h, w_in, q_norm, k_norm, seq, tm=2048, tn=1024):
    m, kdim = h.shape
    width = w_in.shape[1] // 4
    hd = q_norm.shape[0]
    heads = width // hd
    tps = seq // tm
    half = hd // 2
    inv = ROPE_THETA ** (-jnp.arange(half, dtype=F32) / half)
    ang = jnp.arange(seq, dtype=F32)[:, None] * inv[None, :]
    cos = jnp.cos(ang)
    sin = jnp.sin(ang)
    cos_t = jnp.concatenate([cos, cos], axis=-1)
    sin_t = jnp.concatenate([-sin, sin], axis=-1)
    nw = jnp.concatenate([jnp.tile(q_norm, heads), jnp.tile(k_norm, heads)]).reshape(1, 2 * width)
    q_scale = float(hd) ** -0.5 * LOG2_E
    sc = jnp.concatenate([jnp.full((width,), q_scale, F32), jnp.ones((width,), F32)]).reshape(1, 2 * width)
    return pl.pallas_call(
        functools.partial(_attn_qk_kernel, head_dim=hd),
        out_shape=jax.ShapeDtypeStruct((m, 2 * width), BF16),
        grid=(2 * width // tn, m // tm),
        in_specs=[
            pl.BlockSpec((tm, kdim), lambda n, i: (i, 0)),
            pl.BlockSpec((kdim, tn), lambda n, i: (0, n)),
            pl.BlockSpec((tm, hd), lambda n, i: (i % tps, 0)),
            pl.BlockSpec((tm, hd), lambda n, i: (i % tps, 0)),
            pl.BlockSpec((1, tn), lambda n, i: (0, n)),
            pl.BlockSpec((1, tn), lambda n, i: (0, n)),
        ],
        out_specs=pl.BlockSpec((tm, tn), lambda n, i: (i, n)),
        scratch_shapes=[pltpu.VMEM((2, ROW_SUB, tn), F32)],
        compiler_params=_cparams("parallel", "parallel"),
        name="attn_qk_proj",
    )(h, w_in, cos_t, sin_t, nw, sc)


def _moba_kernel(pairs_ref, q_ref, k_ref, v_ref, z_ref, e_ref, u_ref,
                 qaugt_ref, kaug_ref, vt_ref, m_ref, acc_ref, sbuf_ref, cbuf_ref, abuf_ref, pbuf_ref, *,
                 bs, nb, topk, hd, hp, pw, n_pairs):
    seq = nb * bs
    nt = (((1,), (1,)), ((), ()))

    for h in range(hp):
        hs = slice(h * hd, (h + 1) * hd)
        kf = k_ref[:, hs].astype(F32)
        kmean = jnp.sum(kf.reshape(nb, bs, hd), axis=1) * (1.0 / bs)
        g_t = lax.dot_general(kmean, q_ref[:, hs].astype(F32), nt, preferred_element_type=F32,
                              precision=HIGHEST)
        jj = lax.broadcasted_iota(jnp.int32, (nb, seq), 0)
        qb = lax.broadcasted_iota(jnp.int32, (nb, seq), 1) // bs
        rank = jnp.zeros((nb, seq), jnp.int32)
        for jp in range(nb):
            row = g_t[jp:jp + 1, :]
            beats = (row > g_t) | ((row == g_t) & (jp < jj))
            rank = rank + jnp.where(beats & (jp < qb), 1, 0)
        allowed = ((jj < qb) & (rank < topk)) | (jj == qb)
        bias_t = jnp.where(allowed, 0.0, MASK_BIAS).astype(BF16)
        qaugt_ref[h, hd:hd + nb, :] = bias_t
        qaugt_ref[h, hd + nb:hd + LANES, :] = jnp.zeros((LANES - nb, seq), BF16)
        kaug_ref[h, :, 0:hd] = k_ref[:, hs]
        kaug_ref[h, :, hd:hd + LANES] = e_ref[...]
        for jb in range(nb):
            rs = slice(jb * bs, (jb + 1) * bs)
            qaugt_ref[h, 0:hd, rs] = q_ref[rs, hs].astype(F32).T.astype(BF16)
            vt_ref[h, 0:hd, rs] = v_ref[rs, hs].astype(F32).T.astype(BF16)
        ones_row = lax.broadcasted_iota(jnp.int32, (8, seq), 0) == 0
        vt_ref[h, hd:hd + 8, :] = jnp.where(ones_row, 1.0, 0.0).astype(BF16)
    m_ref[...] = jnp.full(m_ref.shape, -jnp.inf, F32)
    acc_ref[...] = jnp.zeros(acc_ref.shape, F32)

    key_id = lax.broadcasted_iota(jnp.int32, (bs, bs), 0)
    qry_id = lax.broadcasted_iota(jnp.int32, (bs, bs), 1)
    causal = key_id <= qry_id
    nu = pw * hp

    def units(pair, t):
        return [(h,) + pair(t * pw + u) for u in range(pw) for h in range(hp)]

    def stage_scores(us, slot, diag):
        for n, (h, i, j) in enumerate(us):
            q0 = pl.multiple_of(i * bs, bs)
            j0 = pl.multiple_of(j * bs, bs)
            s = jnp.dot(kaug_ref[h, pl.ds(j0, bs), :], qaugt_ref[h, :, pl.ds(q0, bs)],
                        preferred_element_type=F32)
            if diag:
                s = jnp.where(causal, s, -jnp.inf)
            sbuf_ref[slot, n] = s
            cbuf_ref[slot, n] = jnp.max(s, axis=0, keepdims=True)

    def stage_softmax(us, slot):
        for n, (h, i, _) in enumerate(us):
            m_old = m_ref[h, i]
            m_new = jnp.maximum(m_old, cbuf_ref[slot, n])
            abuf_ref[slot, n] = jnp.exp2(m_old - m_new)
            m_ref[h, i] = m_new
            pbuf_ref[slot, n] = jnp.exp2(sbuf_ref[slot, n] - m_new).astype(BF16)

    def stage_pv(us, slot):
        for n, (h, i, j) in enumerate(us):
            j0 = pl.multiple_of(j * bs, bs)
            pv = jnp.dot(vt_ref[h, :, pl.ds(j0, bs)], pbuf_ref[slot, n], preferred_element_type=F32)
            acc_ref[h, i] = abuf_ref[slot, n] * acc_ref[h, i] + pv

    def run_pairs(pair, count, diag):
        steps = count // pw
        pbuf_ref[1] = jnp.zeros(pbuf_ref.shape[1:], BF16)
        abuf_ref[1] = jnp.ones(abuf_ref.shape[1:], F32)
        stage_scores(units(pair, 0), 0, diag)

        def step(t, slot):
            stage_scores(units(pair, t), slot, diag)
            stage_softmax(units(pair, t - 1), 1 - slot)
            stage_pv(units(pair, jnp.maximum(t - 2, 0)), slot)

        def two_steps(t2, _):
            step(2 * t2 + 1, 1)
            step(2 * t2 + 2, 0)
            return 0

        lax.fori_loop(0, (steps - 2) // 2, two_steps, 0)
        step(steps - 1, 1)
        stage_softmax(units(pair, steps - 1), 1)
        stage_pv(units(pair, steps - 2), 0)
        stage_pv(units(pair, steps - 1), 1)

    run_pairs(lambda t: (t, t), nb, True)
    if n_pairs > 0:
        run_pairs(lambda t: (pairs_ref[0, t], pairs_ref[1, t]), n_pairs, False)

    def finalize(i, _):
        q0 = pl.multiple_of(i * bs, bs)
        for h in range(hp):
            hs = slice(h * hd, (h + 1) * hd)
            acc = acc_ref[h, i]
            o = (acc[0:hd, :] / acc[hd:hd + 1, :]).T
            u_ref[pl.ds(q0, bs), hs] = (o * z_ref[pl.ds(q0, bs), hs].astype(F32)).astype(BF16)
        return 0

    lax.fori_loop(0, nb, finalize, 0)


def _moba_attention(qk, v, z_act, batch, seq, heads, hd, hp=2):
    m = qk.shape[0]
    bs = MOBA_BLOCK
    nb = seq // bs
    width = heads * hd
    groups = heads // hp
    e_tab = (jnp.arange(seq)[:, None] // bs == jnp.arange(LANES)[None, :]).astype(BF16)
    pair_list = [(i, j) for j in range(nb) for i in range(j + 1, nb)]
    n_pairs = len(pair_list)
    pairs = jnp.asarray(pair_list or [(0, 0)], jnp.int32).T
    pw = next(w for w in (4, 2, 1) if nb % (2 * w) == 0 and n_pairs % (2 * w) == 0)
    kern = functools.partial(_moba_kernel, bs=bs, nb=nb, topk=min(MOBA_TOPK, nb), hd=hd, hp=hp, pw=pw,
                             n_pairs=n_pairs)
    blk = (seq, hp * hd)
    grid_spec = pltpu.PrefetchScalarGridSpec(
        num_scalar_prefetch=1,
        grid=(batch, groups),
        in_specs=[
            pl.BlockSpec(blk, lambda b, g, pr: (b, g)),
            pl.BlockSpec(blk, lambda b, g, pr: (b, groups + g)),
            pl.BlockSpec(blk, lambda b, g, pr: (b, g)),
            pl.BlockSpec(blk, lambda b, g, pr: (b, g)),
            pl.BlockSpec((seq, LANES), lambda b, g, pr: (0, 0)),
        ],
        out_specs=pl.BlockSpec(blk, lambda b, g, pr: (b, g)),
        scratch_shapes=[
            pltpu.VMEM((hp, hd + LANES, seq), BF16),
            pltpu.VMEM((hp, seq, hd + LANES), BF16),
            pltpu.VMEM((hp, hd + 8, seq), BF16),
            pltpu.VMEM((hp, nb, 1, bs), F32),
            pltpu.VMEM((hp, nb, hd + 8, bs), F32),
            pltpu.VMEM((2, pw * hp, bs, bs), F32),
            pltpu.VMEM((2, pw * hp, 1, bs), F32),
            pltpu.VMEM((2, pw * hp, 1, bs), F32),
            pltpu.VMEM((2, pw * hp, bs, bs), BF16),
        ],
    )
    return pl.pallas_call(
        kern,
        out_shape=jax.ShapeDtypeStruct((m, width), BF16),
        grid_spec=grid_spec,
        compiler_params=_cparams("parallel", "parallel"),
        name="moba_attn",
    )(pairs, qk, qk, v, z_act, e_tab)


def _outproj_kernel(u_ref, w_ref, x_ref, g_ref, o_ref):
    acc = jnp.dot(u_ref[...], w_ref[...], preferred_element_type=F32)
    o_ref[...] = x_ref[...] + g_ref[...] * acc


def _outproj(u, w_out, xf, gate, seq, tm=512, tn=1024):
    m, kdim = u.shape
    d = w_out.shape[1]
    b = m // seq
    tps = seq // tm
    return pl.pallas_call(
        _outproj_kernel,
        out_shape=jax.ShapeDtypeStruct((m, d), F32),
        grid=(d // tn, m // tm),
        in_specs=[
            pl.BlockSpec((tm, kdim), lambda n, i: (i, 0)),
            pl.BlockSpec((kdim, tn), lambda n, i: (0, n)),
            pl.BlockSpec((tm, tn), lambda n, i: (i, n)),
            pl.BlockSpec((None, 1, tn), lambda n, i: (i // tps, 0, n)),
        ],
        out_specs=pl.BlockSpec((tm, tn), lambda n, i: (i, n)),
        compiler_params=_cparams("parallel", "parallel"),
        name="outproj",
    )(u, w_out, xf, gate.reshape(b, 1, d))


def _mlstm_qk_kernel(h_ref, w_ref, cw_ref, sc_ref, o_ref, ext_ref, *, tiles_per_seq, conv_k):
    i = pl.program_id(1)
    sub = ROW_SUB
    tiles = _row_tiles(h_ref)

    @pl.when(i % tiles_per_seq == 0)
    def _():
        ext_ref[0, 0:8, :] = jnp.zeros((8, ext_ref.shape[2]), F32)

    def issue(r):
        ext_ref[r % 2, 8:sub + 8, :] = _mm(h_ref[tiles[r], :], w_ref[...], False)

    def finish(r):
        cur = r % 2
        y = cw_ref[conv_k - 1:conv_k, :] * ext_ref[cur, 8:sub + 8, :]
        for back in range(1, conv_k):
            y = y + cw_ref[conv_k - 1 - back:conv_k - back, :] * ext_ref[cur, 8 - back:8 - back + sub, :]
        o_ref[tiles[r], :] = (_silu(y) * sc_ref[...]).astype(BF16)
        ext_ref[1 - cur, 0:8, :] = ext_ref[cur, sub:sub + 8, :]

    _staged(tiles, issue, finish)


def _mlstm_qk_proj(h, w_in, conv_w, qk_width, dk, seq, tm=2048, tn=1024):
    m, kdim = h.shape
    conv_k = conv_w.shape[0]
    tps = seq // tm
    sc = jnp.concatenate([jnp.ones((qk_width,), F32),
                          jnp.full((qk_width,), float(dk) ** -0.5, F32)]).reshape(1, 2 * qk_width)
    return pl.pallas_call(
        functools.partial(_mlstm_qk_kernel, tiles_per_seq=tps, conv_k=conv_k),
        out_shape=jax.ShapeDtypeStruct((m, 2 * qk_width), BF16),
        grid=(2 * qk_width // tn, m // tm),
        in_specs=[
            pl.BlockSpec((tm, kdim), lambda n, i: (i, 0)),
            _w_spec(kdim, tn, 0, False),
            pl.BlockSpec((conv_k, tn), lambda n, i: (0, n)),
            pl.BlockSpec((1, tn), lambda n, i: (0, n)),
        ],
        out_specs=pl.BlockSpec((tm, tn), lambda n, i: (i, n)),
        scratch_shapes=[pltpu.VMEM((2, ROW_SUB + 8, tn), F32)],
        compiler_params=_cparams("arbitrary", "arbitrary"),
        name="mlstm_qk_proj",
    )(h, w_in, conv_w, sc)


def _mlstm_gz_kernel(h_ref, wo_ref, wz_ref, o_ref):
    for rows in _row_tiles(h_ref):
        a = h_ref[rows, :]
        og = _mm(a, wo_ref[...], False)
        z = _mm(a, wz_ref[...], False)
        o_ref[rows, :] = (_sigmoid(og) * _silu(z)).astype(BF16)


def _mlstm_gz_proj(h, w_in, o_col0, z_col0, width, tm=1024, tn=1024):
    m, kdim = h.shape
    return pl.pallas_call(
        _mlstm_gz_kernel,
        out_shape=jax.ShapeDtypeStruct((m, width), BF16),
        grid=(width // tn, m // tm),
        in_specs=[
            pl.BlockSpec((tm, kdim), lambda n, i: (i, 0)),
            _w_spec(kdim, tn, o_col0 // tn, False),
            _w_spec(kdim, tn, z_col0 // tn, False),
        ],
        out_specs=pl.BlockSpec((tm, tn), lambda n, i: (i, n)),
        compiler_params=_cparams("parallel", "parallel"),
        name="mlstm_gz_proj",
    )(h, w_in, w_in)


def _gate_proj_kernel(h_ref, w_ref, b_ref, rows_ref, cols_ref, *, heads, chunk):
    g = jnp.dot(h_ref[...], w_ref[...].astype(BF16), preferred_element_type=F32) + b_ref[...]
    tm = g.shape[0]
    g_t = g.T
    i_rows = g_t[0:heads, :]
    f_rows = g_t[heads:2 * heads, :]
    b = jnp.minimum(f_rows, 0.0) - jnp.log1p(jnp.exp(-jnp.abs(f_rows)))
    pos = lax.broadcasted_iota(jnp.int32, b.shape, 1) % chunk
    shift = 1
    while shift < chunk:
        b = b + jnp.where(pos >= shift, pltpu.roll(b, shift, axis=1), 0.0)
        shift *= 2
    rows = jnp.concatenate([i_rows, b], axis=0)
    rows_ref[...] = rows
    cols_ref[...] = jnp.concatenate([rows, jnp.zeros((LANES - 2 * heads, tm), F32)], axis=0).T


def _gate_proj(h, w_gate, gate_b, batch, seq, tm=1024):
    m, kdim = h.shape
    ng = w_gate.shape[1]
    heads = ng // 2
    tps = seq // tm
    w_pad = jnp.zeros((kdim, LANES), F32).at[:, :ng].set(w_gate)
    b_pad = jnp.zeros((1, LANES), F32).at[0, :ng].set(gate_b)
    kern = functools.partial(_gate_proj_kernel, heads=heads, chunk=MLSTM_CHUNK)
    return pl.pallas_call(
        kern,
        out_shape=(jax.ShapeDtypeStruct((batch, ng, seq), F32), jax.ShapeDtypeStruct((m, LANES), F32)),
        grid=(m // tm,),
        in_specs=[
            pl.BlockSpec((tm, kdim), lambda i: (i, 0)),
            pl.BlockSpec((kdim, LANES), lambda i: (0, 0)),
            pl.BlockSpec((1, LANES), lambda i: (0, 0)),
        ],
        out_specs=(pl.BlockSpec((None, ng, tm), lambda i: (i // tps, 0, i % tps)),
                   pl.BlockSpec((tm, LANES), lambda i: (i, 0))),
        compiler_params=_cparams("parallel"),
        name="mlstm_gate_proj",
    )(h, w_pad, b_pad)


def _mlstm_kernel(q_ref, k_ref, v_ref, gz_ref, rows_ref, cols_ref, nw_ref, u_ref, c_ref, m_ref, *,
                  chunk, heads, hp, dk, dv):
    g = pl.program_id(1)
    c = pl.program_id(2)
    L = chunk
    nt = (((1,), (1,)), ((), ()))

    @pl.when(c == 0)
    def _():
        c_ref[...] = jnp.zeros(c_ref.shape, F32)
        m_ref[...] = jnp.zeros(m_ref.shape, F32)

    t_id = lax.broadcasted_iota(jnp.int32, (L, L), 0)
    s_id = lax.broadcasted_iota(jnp.int32, (L, L), 1)
    causal = s_id <= t_id
    lane = lax.broadcasted_iota(jnp.int32, (L, LANES), 1)
    ones_blk = jnp.where(lane == 0, 1.0, 0.0).astype(BF16)
    cols = cols_ref[...]

    def col_of(idx):
        return jnp.sum(jnp.where(lane == idx, cols, 0.0), axis=1, keepdims=True)

    gate = []
    for hh in range(hp):
        h = g * hp + hh
        i_row = rows_ref[pl.ds(h, 1), :]
        b_row = rows_ref[pl.ds(heads + h, 1), :]
        i_col = col_of(h)
        b_col = col_of(heads + h)
        m_prev = m_ref[hh]
        log_d = jnp.where(causal, b_col - b_row + i_row, -jnp.inf)
        log_inter = b_col + m_prev
        m_t = jnp.maximum(log_inter, jnp.max(log_d, axis=1, keepdims=True))
        d_mat = jnp.exp(log_d - m_t)
        w_inter = jnp.exp(log_inter - m_t)
        b_last = b_row[:, L - 1:L]
        m_new = jnp.maximum(b_last + m_prev, jnp.max(b_last - b_row + i_row, axis=1, keepdims=True))
        decay = jnp.exp(b_last + m_prev - m_new)
        ws_col = jnp.exp(b_last - b_col + i_col - m_new)
        m_ref[hh] = m_new
        gate.append((m_t, d_mat, w_inter, decay, ws_col))

    s_qk = []
    for hh in range(hp):
        q = q_ref[:, hh * dk:(hh + 1) * dk]
        k = k_ref[:, hh * dk:(hh + 1) * dk]
        s_qk.append((lax.dot_general(q, k, nt, preferred_element_type=F32) * gate[hh][1]).astype(BF16))

    for hh in range(hp):
        m_t, _, w_inter, _, _ = gate[hh]
        q_scaled = (w_inter * q_ref[:, hh * dk:(hh + 1) * dk].astype(F32)).astype(BF16)
        v_aug = jnp.concatenate([v_ref[:, hh * dv:(hh + 1) * dv], ones_blk], axis=1)
        lhs = jnp.concatenate([q_scaled, s_qk[hh]], axis=1)
        rhs = jnp.concatenate([c_ref[hh].astype(BF16), v_aug], axis=0)
        num = jnp.dot(lhs, rhs, preferred_element_type=F32)
        d = jnp.maximum(jnp.abs(num[:, dv:dv + 1]), jnp.exp(-m_t))
        nh = num[:, :dv]
        r = lax.rsqrt(jnp.mean(nh * nh, axis=-1, keepdims=True) + NORM_EPS * (d * d))
        hn = nh * r * nw_ref[:, hh * dv:(hh + 1) * dv]
        u_ref[:, hh * dv:(hh + 1) * dv] = (hn * gz_ref[:, hh * dv:(hh + 1) * dv].astype(F32)).astype(BF16)

    for hh in range(hp):
        _, _, _, decay, ws_col = gate[hh]
        k_scaled = (ws_col * k_ref[:, hh * dk:(hh + 1) * dk].astype(F32)).astype(BF16)
        v_aug = jnp.concatenate([v_ref[:, hh * dv:(hh + 1) * dv], ones_blk], axis=1)
        upd = lax.dot_general(k_scaled, v_aug, (((0,), (0,)), ((), ())), preferred_element_type=F32)
        c_ref[hh] = decay * c_ref[hh] + upd


def _mlstm_core(qk, v, gz, rows, cols, norm_w, batch, seq, heads, dk, dv, hp=None):
    m = qk.shape[0]
    hp = heads if hp is None else hp
    L = MLSTM_CHUNK
    nc = seq // L
    groups = heads // hp
    v_width = heads * dv
    k_off = groups
    kern = functools.partial(_mlstm_kernel, chunk=L, heads=heads, hp=hp, dk=dk, dv=dv)
    return pl.pallas_call(
        kern,
        out_shape=jax.ShapeDtypeStruct((m, v_width), BF16),
        grid=(batch, groups, nc),
        in_specs=[
            pl.BlockSpec((L, hp * dk), lambda b, g, c: (b * nc + c, g)),
            pl.BlockSpec((L, hp * dk), lambda b, g, c: (b * nc + c, k_off + g)),
            pl.BlockSpec((L, hp * dv), lambda b, g, c: (b * nc + c, g)),
            pl.BlockSpec((L, hp * dv), lambda b, g, c: (b * nc + c, g)),
            pl.BlockSpec((None, 2 * heads, L), lambda b, g, c: (b, 0, c)),
            pl.BlockSpec((L, LANES), lambda b, g, c: (b * nc + c, 0)),
            pl.BlockSpec((1, hp * dv), lambda b, g, c: (0, g)),
        ],
        out_specs=pl.BlockSpec((L, hp * dv), lambda b, g, c: (b * nc + c, g)),
        scratch_shapes=[pltpu.VMEM((hp, dk, dv + LANES), F32), pltpu.VMEM((hp, 1, 1), F32)],
        compiler_params=_cparams("parallel", "parallel", "arbitrary"),
        name="mlstm_core",
    )(qk, qk, v, gz, rows, cols, norm_w.reshape(1, v_width))


def _moba_layer(xf, h, gate, w_in, q_norm, k_norm, w_out, batch, seq):
    hd = q_norm.shape[0]
    heads = w_in.shape[1] // 4 // hd
    width = heads * hd
    w_in = w_in.astype(BF16)
    qk = _attn_qk_proj(h, w_in, q_norm, k_norm, seq)
    v = _proj_act(h, w_in, 2 * width, width, None, "attn_v_proj")
    z_act = _proj_act(h, w_in, 3 * width, width, _silu, "attn_z_proj")
    u = _moba_attention(qk, v, z_act, batch, seq, heads, hd)
    return _outproj(u, w_out.astype(BF16), xf, gate, seq)


def _mlstm_layer(xf, h, gate, w_in, gate_b, conv_w, norm_w, w_out, batch, seq):
    heads = gate_b.shape[0] // 2
    v_width = w_out.shape[0]
    dv = v_width // heads
    qk_width = conv_w.shape[1] // 2
    dk = qk_width // heads
    n_main = 2 * qk_width + 3 * v_width
    w_main = w_in[:, :n_main].astype(BF16)
    qk = _mlstm_qk_proj(h, w_main, conv_w, qk_width, dk, seq)
    v = _proj_act(h, w_main, 2 * qk_width, v_width, None, "mlstm_v_proj")
    gz = _mlstm_gz_proj(h, w_main, 2 * qk_width + v_width, 2 * qk_width + 2 * v_width, v_width)
    rows, cols = _gate_proj(h, w_in[:, n_main:], gate_b, batch, seq)
    u = _mlstm_core(qk, v, gz, rows, cols, norm_w, batch, seq, heads, dk, dv)
    return _outproj(u, w_out.astype(BF16), xf, gate, seq)


def kernel(x, c, norm_w, ada_w, ada_b, att_w_in, att_q_norm, att_k_norm, att_w_out, mlstm_w_in,
           mlstm_gate_b, mlstm_conv_w, mlstm_norm_w, mlstm_w_out):
    batch, seq, d = x.shape
    depth = norm_w.shape[0]
    mod = _adaln(c, ada_w, ada_b)
    xf = x.reshape(batch * seq, d)
    for layer in range(depth):
        shift, scale, gate = mod[layer, :, :d], mod[layer, :, d:2 * d], mod[layer, :, 2 * d:]
        h = _norm_mod(xf, norm_w[layer], scale, shift, seq)
        j = layer // 2
        if layer % 2 == 0:
            xf = _moba_layer(xf, h, gate, att_w_in[j], att_q_norm[j], att_k_norm[j], att_w_out[j], batch, seq)
        else:
            xf = _mlstm_layer(xf, h, gate, mlstm_w_in[j], mlstm_gate_b[j], mlstm_conv_w[j],
                              mlstm_norm_w[j], mlstm_w_out[j], batch, seq)
    return xf.reshape(batch, seq, d)
```

```python
import functools

import jax
import jax.numpy as jnp
from jax import lax
from jax.experimental import pallas as pl
from jax.experimental.pallas import tpu as pltpu

F32 = jnp.float32
BF16 = jnp.bfloat16
HIGHEST = lax.Precision.HIGHEST

NORM_EPS = 1e-6
MOBA_BLOCK = 256
MOBA_TOPK = 3
ROPE_THETA = 10000.0
MLSTM_CHUNK = 256
MASK_BIAS = -1e30
LANES = 128
LOG2_E = 1.4426950408889634
ROW_SUB = 512

_VMEM_LIMIT = 48 * 1024 * 1024


def _sigmoid(x):
    return 1.0 / (1.0 + jnp.exp(-x))


def _cparams(*sem):
    return pltpu.CompilerParams(dimension_semantics=sem, vmem_limit_bytes=_VMEM_LIMIT)


def _adaln_kernel(c_ref, w_ref, b_ref, o_ref):
    c = c_ref[...]
    cond = c * _sigmoid(c)
    o_ref[...] = jnp.dot(cond, w_ref[...], preferred_element_type=F32, precision=HIGHEST) + b_ref[...]


def _adaln(c, ada_w, ada_b, tn=768):
    depth, d, n3 = ada_w.shape
    b = c.shape[0]
    rows = 8
    c_pad = jnp.zeros((rows, d), F32).at[:b].set(c)
    out = pl.pallas_call(
        _adaln_kernel,
        out_shape=jax.ShapeDtypeStruct((depth, rows, n3), F32),
        grid=(depth, n3 // tn),
        in_specs=[
            pl.BlockSpec((rows, d), lambda l, n: (0, 0)),
            pl.BlockSpec((None, d, tn), lambda l, n: (l, 0, n)),
            pl.BlockSpec((None, 1, tn), lambda l, n: (l, 0, n)),
        ],
        out_specs=pl.BlockSpec((None, rows, tn), lambda l, n: (l, 0, n)),
        compiler_params=_cparams("parallel", "parallel"),
        name="adaln_mod",
    )(c_pad, ada_w, ada_b.reshape(depth, 1, n3))
    return out[:, :b]


def _norm_mod_kernel(x_ref, nw_ref, sc_ref, sh_ref, o_ref):
    x = x_ref[...]
    ms = jnp.mean(x * x, axis=-1, keepdims=True)
    y = x * lax.rsqrt(ms + NORM_EPS) * nw_ref[...]
    o_ref[...] = (y * (1.0 + sc_ref[...]) + sh_ref[...]).astype(BF16)


def _norm_mod(xf, nw, scale, shift, seq, ts=512):
    m, d = xf.shape
    b = m // seq
    tps = seq // ts
    return pl.pallas_call(
        _norm_mod_kernel,
        out_shape=jax.ShapeDtypeStruct((m, d), BF16),
        grid=(m // ts,),
        in_specs=[
            pl.BlockSpec((ts, d), lambda i: (i, 0)),
            pl.BlockSpec((1, d), lambda i: (0, 0)),
            pl.BlockSpec((None, 1, d), lambda i: (i // tps, 0, 0)),
            pl.BlockSpec((None, 1, d), lambda i: (i // tps, 0, 0)),
        ],
        out_specs=pl.BlockSpec((ts, d), lambda i: (i, 0)),
        compiler_params=_cparams("parallel"),
        name="norm_mod",
    )(xf, nw.reshape(1, d), scale.reshape(b, 1, d), shift.reshape(b, 1, d))


def _row_tiles(a_ref):
    return [slice(r, r + ROW_SUB) for r in range(0, a_ref.shape[0], ROW_SUB)]


def _silu(x):
    return x * _sigmoid(x)


def _mm(a, w, w_t):
    dims = (((1,), (1,)), ((), ())) if w_t else (((1,), (0,)), ((), ()))
    return lax.dot_general(a, w, dims, preferred_element_type=F32)


def _w_spec(kdim, tn, off, w_t):
    if w_t:
        return pl.BlockSpec((tn, kdim), lambda n, i: (off + n, 0))
    return pl.BlockSpec((kdim, tn), lambda n, i: (0, off + n))


def _proj_act_kernel(h_ref, w_ref, o_ref, *, act, w_t):
    for rows in _row_tiles(h_ref):
        acc = _mm(h_ref[rows, :], w_ref[...], w_t)
        o_ref[rows, :] = (acc if act is None else act(acc)).astype(BF16)


def _proj_act(h, w, col0, width, act, name, w_t=False, tm=1024, tn=1024):
    m, kdim = h.shape
    return pl.pallas_call(
        functools.partial(_proj_act_kernel, act=act, w_t=w_t),
        out_shape=jax.ShapeDtypeStruct((m, width), BF16),
        grid=(width // tn, m // tm),
        in_specs=[
            pl.BlockSpec((tm, kdim), lambda n, i: (i, 0)),
            _w_spec(kdim, tn, col0 // tn, w_t),
        ],
        out_specs=pl.BlockSpec((tm, tn), lambda n, i: (i, n)),
        compiler_params=_cparams("parallel", "parallel"),
        name=name,
    )(h, w)


def _staged(tiles, issue, finish):
    issue(0)
    for r in range(len(tiles)):
        if r + 1 < len(tiles):
            issue(r + 1)
        finish(r)


def _attn_qk_kernel(h_ref, w_ref, cos_ref, sin_ref, nw_ref, sc_ref, o_ref, acc_ref, *, head_dim):
    tiles = _row_tiles(h_ref)

    def issue(r):
        acc_ref[r % 2] = jnp.dot(h_ref[tiles[r], :], w_ref[...], preferred_element_type=F32)

    def finish(r):
        rows = tiles[r]
        cos = cos_ref[rows, :]
        sin = sin_ref[rows, :]
        for hh in range(acc_ref.shape[2] // head_dim):
            sl = slice(hh * head_dim, (hh + 1) * head_dim)
            xh = acc_ref[r % 2, :, sl]
            ms = jnp.mean(xh * xh, axis=-1, keepdims=True)
            y = xh * lax.rsqrt(ms + NORM_EPS) * nw_ref[:, sl]
            rot = y * cos + pltpu.roll(y, head_dim // 2, axis=1) * sin
            o_ref[rows, sl] = (rot * sc_ref[:, sl]).astype(BF16)

    _staged(tiles, issue, finish)


def _attn_qk_proj(h, w_in, q_norm, k_norm, seq, tm=2048, tn=1024):
    m, kdim = h.shape
    width = w_in.shape[1] // 4
    hd = q_norm.shape[0]
    heads = width // hd
    tps = seq // tm
    half = hd // 2
    inv = ROPE_THETA ** (-jnp.arange(half, dtype=F32) / half)
    ang = jnp.arange(seq, dtype=F32)[:, None] * inv[None, :]
    cos = jnp.cos(ang)
    sin = jnp.sin(ang)
    cos_t = jnp.concatenate([cos, cos], axis=-1)
    sin_t = jnp.concatenate([-sin, sin], axis=-1)
    nw = jnp.concatenate([jnp.tile(q_norm, heads), jnp.tile(k_norm, heads)]).reshape(1, 2 * width)
    q_scale = float(hd) ** -0.5 * LOG2_E
    sc = jnp.concatenate([jnp.full((width,), q_scale, F32), jnp.ones((width,), F32)]).reshape(1, 2 * width)
    return pl.pallas_call(
        functools.partial(_attn_qk_kernel, head_dim=hd),
        out_shape=jax.ShapeDtypeStruct((m, 2 * width), BF16),
        grid=(2 * width // tn, m // tm),
        in_specs=[
            pl.BlockSpec((tm, kdim), lambda n, i: (i, 0)),
            pl.BlockSpec((kdim, tn), lambda n, i: (0, n)),
            pl.BlockSpec((tm, hd), lambda n, i: (i % tps, 0)),
            pl.BlockSpec((tm, hd), lambda n, i: (i % tps, 0)),
            pl.BlockSpec((1, tn), lambda n, i: (0, n)),
            pl.BlockSpec((1, tn), lambda n, i: (0, n)),
        ],
        out_specs=pl.BlockSpec((tm, tn), lambda n, i: (i, n)),
        scratch_shapes=[pltpu.VMEM((2, ROW_SUB, tn), F32)],
        compiler_params=_cparams("parallel", "parallel"),
        name="attn_qk_proj",
    )(h, w_in, cos_t, sin_t, nw, sc)


def _moba_kernel(q_ref, k_ref, v_ref, z_ref, e_ref, u_ref,
                 qaugt_ref, kaug_ref, vt_ref, m_ref, acc_ref, sbuf_ref, cbuf_ref, abuf_ref, pbuf_ref, *,
                 bs, nb, topk, hd, hp, pw, pair_list):
    seq = nb * bs
    nt = (((1,), (1,)), ((), ()))

    for h in range(hp):
        hs = slice(h * hd, (h + 1) * hd)
        kf = k_ref[:, hs].astype(F32)
        kmean = jnp.sum(kf.reshape(nb, bs, hd), axis=1) * (1.0 / bs)
        g_t = lax.dot_general(kmean, q_ref[:, hs].astype(F32), nt, preferred_element_type=F32,
                              precision=HIGHEST)
        jj = lax.broadcasted_iota(jnp.int32, (nb, seq), 0)
        qb = lax.broadcasted_iota(jnp.int32, (nb, seq), 1) // bs
        rank = jnp.zeros((nb, seq), jnp.int32)
        for jp in range(nb):
            row = g_t[jp:jp + 1, :]
            beats = (row > g_t) | ((row == g_t) & (jp < jj))
            rank = rank + jnp.where(beats & (jp < qb), 1, 0)
        allowed = ((jj < qb) & (rank < topk)) | (jj == qb)
        bias_t = jnp.where(allowed, 0.0, MASK_BIAS).astype(BF16)
        qaugt_ref[h, hd:hd + nb, :] = bias_t
        qaugt_ref[h, hd + nb:hd + LANES, :] = jnp.zeros((LANES - nb, seq), BF16)
        kaug_ref[h, :, 0:hd] = k_ref[:, hs]
        kaug_ref[h, :, hd:hd + LANES] = e_ref[...]
        for jb in range(nb):
            rs = slice(jb * bs, (jb + 1) * bs)
            qaugt_ref[h, 0:hd, rs] = q_ref[rs, hs].astype(F32).T.astype(BF16)
            vt_ref[h, 0:hd, rs] = v_ref[rs, hs].astype(F32).T.astype(BF16)
        ones_row = lax.broadcasted_iota(jnp.int32, (8, seq), 0) == 0
        vt_ref[h, hd:hd + 8, :] = jnp.where(ones_row, 1.0, 0.0).astype(BF16)
    m_ref[...] = jnp.full(m_ref.shape, -jnp.inf, F32)
    acc_ref[...] = jnp.zeros(acc_ref.shape, F32)

    key_id = lax.broadcasted_iota(jnp.int32, (bs, bs), 0)
    qry_id = lax.broadcasted_iota(jnp.int32, (bs, bs), 1)
    causal = key_id <= qry_id
    steps = [[(h, i, j) for (i, j) in pair_list[t:t + pw] for h in range(hp)]
             for t in range(0, len(pair_list), pw)]

    def stage_scores(us, slot):
        for n, (h, i, j) in enumerate(us):
            s = jnp.dot(kaug_ref[h, j * bs:(j + 1) * bs, :], qaugt_ref[h, :, i * bs:(i + 1) * bs],
                        preferred_element_type=F32)
            if i == j:
                s = jnp.where(causal, s, -jnp.inf)
            sbuf_ref[slot, n] = s
            cbuf_ref[slot, n] = jnp.max(s, axis=0, keepdims=True)

    def stage_softmax(us, slot):
        for n, (h, i, _) in enumerate(us):
            m_old = m_ref[h, i]
            m_new = jnp.maximum(m_old, cbuf_ref[slot, n])
            abuf_ref[slot, n] = jnp.exp2(m_old - m_new)
            m_ref[h, i] = m_new
            pbuf_ref[slot, n] = jnp.exp2(sbuf_ref[slot, n] - m_new).astype(BF16)

    def stage_pv(us, slot):
        for n, (h, i, j) in enumerate(us):
            pv = jnp.dot(vt_ref[h, :, j * bs:(j + 1) * bs], pbuf_ref[slot, n], preferred_element_type=F32)
            acc_ref[h, i] = abuf_ref[slot, n] * acc_ref[h, i] + pv

    for t in range(len(steps) + 2):
        if t < len(steps):
            stage_scores(steps[t], t % 2)
        if 1 <= t <= len(steps):
            stage_softmax(steps[t - 1], (t - 1) % 2)
        if t >= 2:
            stage_pv(steps[t - 2], t % 2)

    def finalize(i, _):
        q0 = pl.multiple_of(i * bs, bs)
        for h in range(hp):
            hs = slice(h * hd, (h + 1) * hd)
            acc = acc_ref[h, i]
            o = (acc[0:hd, :] / acc[hd:hd + 1, :]).T
            u_ref[pl.ds(q0, bs), hs] = (o * z_ref[pl.ds(q0, bs), hs].astype(F32)).astype(BF16)
        return 0

    lax.fori_loop(0, nb, finalize, 0)


def _moba_attention(qk, v, z_act, batch, seq, heads, hd, hp=1):
    m = qk.shape[0]
    bs = MOBA_BLOCK
    nb = seq // bs
    width = heads * hd
    groups = heads // hp
    e_tab = (jnp.arange(seq)[:, None] // bs == jnp.arange(LANES)[None, :]).astype(BF16)
    pair_list = [(i, i) for i in range(nb)] + [(i, j) for j in range(nb) for i in range(j + 1, nb)]
    pw = 4
    kern = functools.partial(_moba_kernel, bs=bs, nb=nb, topk=min(MOBA_TOPK, nb), hd=hd, hp=hp, pw=pw,
                             pair_list=tuple(pair_list))
    blk = (seq, hp * hd)
    return pl.pallas_call(
        kern,
        out_shape=jax.ShapeDtypeStruct((m, width), BF16),
        grid=(batch, groups),
        in_specs=[
            pl.BlockSpec(blk, lambda b, g: (b, g)),
            pl.BlockSpec(blk, lambda b, g: (b, groups + g)),
            pl.BlockSpec(blk, lambda b, g: (b, g)),
            pl.BlockSpec(blk, lambda b, g: (b, g)),
            pl.BlockSpec((seq, LANES), lambda b, g: (0, 0)),
        ],
        out_specs=pl.BlockSpec(blk, lambda b, g: (b, g)),
        scratch_shapes=[
            pltpu.VMEM((hp, hd + LANES, seq), BF16),
            pltpu.VMEM((hp, seq, hd + LANES), BF16),
            pltpu.VMEM((hp, hd + 8, seq), BF16),
            pltpu.VMEM((hp, nb, 1, bs), F32),
            pltpu.VMEM((hp, nb, hd + 8, bs), F32),
            pltpu.VMEM((2, pw * hp, bs, bs), F32),
            pltpu.VMEM((2, pw * hp, 1, bs), F32),
            pltpu.VMEM((2, pw * hp, 1, bs), F32),
            pltpu.VMEM((2, pw * hp, bs, bs), BF16),
        ],
        compiler_params=_cparams("parallel", "parallel"),
        name="moba_attn",
    )(qk, qk, v, z_act, e_tab)


def _outproj_kernel(u_ref, w_ref, x_ref, g_ref, o_ref):
    acc = jnp.dot(u_ref[...], w_ref[...], preferred_element_type=F32)
    o_ref[...] = x_ref[...] + g_ref[...] * acc


def _outproj(u, w_out, xf, gate, seq, tm=512, tn=1024):
    m, kdim = u.shape
    d = w_out.shape[1]
    b = m // seq
    tps = seq // tm
    return pl.pallas_call(
        _outproj_kernel,
        out_shape=jax.ShapeDtypeStruct((m, d), F32),
        grid=(d // tn, m // tm),
        in_specs=[
            pl.BlockSpec((tm, kdim), lambda n, i: (i, 0)),
            pl.BlockSpec((kdim, tn), lambda n, i: (0, n)),
            pl.BlockSpec((tm, tn), lambda n, i: (i, n)),
            pl.BlockSpec((None, 1, tn), lambda n, i: (i // tps, 0, n)),
        ],
        out_specs=pl.BlockSpec((tm, tn), lambda n, i: (i, n)),
        compiler_params=_cparams("parallel", "parallel"),
        name="outproj",
    )(u, w_out, xf, gate.reshape(b, 1, d))


def _mlstm_qk_kernel(h_ref, w_ref, cw_ref, sc_ref, o_ref, ext_ref, *, tiles_per_seq, conv_k):
    i = pl.program_id(1)
    sub = ROW_SUB
    tiles = _row_tiles(h_ref)

    @pl.when(i % tiles_per_seq == 0)
    def _():
        ext_ref[0, 0:8, :] = jnp.zeros((8, ext_ref.shape[2]), F32)

    def issue(r):
        ext_ref[r % 2, 8:sub + 8, :] = _mm(h_ref[tiles[r], :], w_ref[...], False)

    def finish(r):
        cur = r % 2
        y = cw_ref[conv_k - 1:conv_k, :] * ext_ref[cur, 8:sub + 8, :]
        for back in range(1, conv_k):
            y = y + cw_ref[conv_k - 1 - back:conv_k - back, :] * ext_ref[cur, 8 - back:8 - back + sub, :]
        o_ref[tiles[r], :] = (_silu(y) * sc_ref[...]).astype(BF16)
        ext_ref[1 - cur, 0:8, :] = ext_ref[cur, sub:sub + 8, :]

    _staged(tiles, issue, finish)


def _mlstm_qk_proj(h, w_in, conv_w, qk_width, dk, seq, tm=2048, tn=1024):
    m, kdim = h.shape
    conv_k = conv_w.shape[0]
    tps = seq // tm
    sc = jnp.concatenate([jnp.ones((qk_width,), F32),
                          jnp.full((qk_width,), float(dk) ** -0.5, F32)]).reshape(1, 2 * qk_width)
    return pl.pallas_call(
        functools.partial(_mlstm_qk_kernel, tiles_per_seq=tps, conv_k=conv_k),
        out_shape=jax.ShapeDtypeStruct((m, 2 * qk_width), BF16),
        grid=(2 * qk_width // tn, m // tm),
        in_specs=[
            pl.BlockSpec((tm, kdim), lambda n, i: (i, 0)),
            _w_spec(kdim, tn, 0, False),
            pl.BlockSpec((conv_k, tn), lambda n, i: (0, n)),
            pl.BlockSpec((1, tn), lambda n, i: (0, n)),
        ],
        out_specs=pl.BlockSpec((tm, tn), lambda n, i: (i, n)),
        scratch_shapes=[pltpu.VMEM((2, ROW_SUB + 8, tn), F32)],
        compiler_params=_cparams("arbitrary", "arbitrary"),
        name="mlstm_qk_proj",
    )(h, w_in, conv_w, sc)


def _mlstm_gz_kernel(h_ref, wo_ref, wz_ref, o_ref):
    for rows in _row_tiles(h_ref):
        a = h_ref[rows, :]
        og = _mm(a, wo_ref[...], False)
        z = _mm(a, wz_ref[...], False)
        o_ref[rows, :] = (_sigmoid(og) * _silu(z)).astype(BF16)


def _mlstm_gz_proj(h, w_in, o_col0, z_col0, width, tm=1024, tn=1024):
    m, kdim = h.shape
    return pl.pallas_call(
        _mlstm_gz_kernel,
        out_shape=jax.ShapeDtypeStruct((m, width), BF16),
        grid=(width // tn, m // tm),
        in_specs=[
            pl.BlockSpec((tm, kdim), lambda n, i: (i, 0)),
            _w_spec(kdim, tn, o_col0 // tn, False),
            _w_spec(kdim, tn, z_col0 // tn, False),
        ],
        out_specs=pl.BlockSpec((tm, tn), lambda n, i: (i, n)),
        compiler_params=_cparams("parallel", "parallel"),
        name="mlstm_gz_proj",
    )(h, w_in, w_in)


def _gate_proj_kernel(h_ref, w_ref, b_ref, rows_ref, cols_ref, *, heads, chunk):
    g = jnp.dot(h_ref[...], w_ref[...].astype(BF16), preferred_element_type=F32) + b_ref[...]
    tm = g.shape[0]
    g_t = g.T
    i_rows = g_t[0:heads, :]
    f_rows = g_t[heads:2 * heads, :]
    b = jnp.minimum(f_rows, 0.0) - jnp.log1p(jnp.exp(-jnp.abs(f_rows)))
    pos = lax.broadcasted_iota(jnp.int32, b.shape, 1) % chunk
    shift = 1
    while shift < chunk:
        b = b + jnp.where(pos >= shift, pltpu.roll(b, shift, axis=1), 0.0)
        shift *= 2
    rows = jnp.concatenate([i_rows, b], axis=0)
    rows_ref[...] = rows
    cols_ref[...] = jnp.concatenate([rows, jnp.zeros((LANES - 2 * heads, tm), F32)], axis=0).T


def _gate_proj(h, w_gate, gate_b, batch, seq, tm=1024):
    m, kdim = h.shape
    ng = w_gate.shape[1]
    heads = ng // 2
    tps = seq // tm
    w_pad = jnp.zeros((kdim, LANES), F32).at[:, :ng].set(w_gate)
    b_pad = jnp.zeros((1, LANES), F32).at[0, :ng].set(gate_b)
    kern = functools.partial(_gate_proj_kernel, heads=heads, chunk=MLSTM_CHUNK)
    return pl.pallas_call(
        kern,
        out_shape=(jax.ShapeDtypeStruct((batch, ng, seq), F32), jax.ShapeDtypeStruct((m, LANES), F32)),
        grid=(m // tm,),
        in_specs=[
            pl.BlockSpec((tm, kdim), lambda i: (i, 0)),
            pl.BlockSpec((kdim, LANES), lambda i: (0, 0)),
            pl.BlockSpec((1, LANES), lambda i: (0, 0)),
        ],
        out_specs=(pl.BlockSpec((None, ng, tm), lambda i: (i // tps, 0, i % tps)),
                   pl.BlockSpec((tm, LANES), lambda i: (i, 0))),
        compiler_params=_cparams("parallel"),
        name="mlstm_gate_proj",
    )(h, w_pad, b_pad)


def _mlstm_kernel(q_ref, k_ref, v_ref, gz_ref, rows_ref, cols_ref, nw_ref, u_ref, c_ref, m_ref, *,
                  chunk, heads, hp, dk, dv):
    g = pl.program_id(1)
    c = pl.program_id(2)
    L = chunk
    nt = (((1,), (1,)), ((), ()))

    @pl.when(c == 0)
    def _():
        c_ref[...] = jnp.zeros(c_ref.shape, F32)
        m_ref[...] = jnp.zeros(m_ref.shape, F32)

    t_id = lax.broadcasted_iota(jnp.int32, (L, L), 0)
    s_id = lax.broadcasted_iota(jnp.int32, (L, L), 1)
    causal = s_id <= t_id
    lane = lax.broadcasted_iota(jnp.int32, (L, LANES), 1)
    ones_blk = jnp.where(lane == 0, 1.0, 0.0).astype(BF16)
    cols = cols_ref[...]

    def col_of(idx):
        return jnp.sum(jnp.where(lane == idx, cols, 0.0), axis=1, keepdims=True)

    gate = []
    for hh in range(hp):
        h = g * hp + hh
        i_row = rows_ref[pl.ds(h, 1), :]
        b_row = rows_ref[pl.ds(heads + h, 1), :]
        i_col = col_of(h)
        b_col = col_of(heads + h)
        m_prev = m_ref[hh]
        log_d = jnp.where(causal, b_col - b_row + i_row, -jnp.inf)
        log_inter = b_col + m_prev
        m_t = jnp.maximum(log_inter, jnp.max(log_d, axis=1, keepdims=True))
        d_mat = jnp.exp(log_d - m_t)
        w_inter = jnp.exp(log_inter - m_t)
        b_last = b_row[:, L - 1:L]
        m_new = jnp.maximum(b_last + m_prev, jnp.max(b_last - b_row + i_row, axis=1, keepdims=True))
        decay = jnp.exp(b_last + m_prev - m_new)
        ws_col = jnp.exp(b_last - b_col + i_col - m_new)
        m_ref[hh] = m_new
        gate.append((m_t, d_mat, w_inter, decay, ws_col))

    s_qk = []
    for hh in range(hp):
        q = q_ref[:, hh * dk:(hh + 1) * dk]
        k = k_ref[:, hh * dk:(hh + 1) * dk]
        s_qk.append((lax.dot_general(q, k, nt, preferred_element_type=F32) * gate[hh][1]).astype(BF16))

    for hh in range(hp):
        m_t, _, w_inter, _, _ = gate[hh]
        q_scaled = (w_inter * q_ref[:, hh * dk:(hh + 1) * dk].astype(F32)).astype(BF16)
        v_aug = jnp.concatenate([v_ref[:, hh * dv:(hh + 1) * dv], ones_blk], axis=1)
        lhs = jnp.concatenate([q_scaled, s_qk[hh]], axis=1)
        rhs = jnp.concatenate([c_ref[hh].astype(BF16), v_aug], axis=0)
        num = jnp.dot(lhs, rhs, preferred_element_type=F32)
        d = jnp.maximum(jnp.abs(num[:, dv:dv + 1]), jnp.exp(-m_t))
        nh = num[:, :dv]
        r = lax.rsqrt(jnp.mean(nh * nh, axis=-1, keepdims=True) + NORM_EPS * (d * d))
        hn = nh * r * nw_ref[:, hh * dv:(hh + 1) * dv]
        u_ref[:, hh * dv:(hh + 1) * dv] = (hn * gz_ref[:, hh * dv:(hh + 1) * dv].astype(F32)).astype(BF16)

    for hh in range(hp):
        _, _, _, decay, ws_col = gate[hh]
        k_scaled = (ws_col * k_ref[:, hh * dk:(hh + 1) * dk].astype(F32)).astype(BF16)
        v_aug = jnp.concatenate([v_ref[:, hh * dv:(hh + 1) * dv], ones_blk], axis=1)
        upd = lax.dot_general(k_scaled, v_aug, (((0,), (0,)), ((), ())), preferred_element_type=F32)
        c_ref[hh] = decay * c_ref[hh] + upd


def _mlstm_core(qk, v, gz, rows, cols, norm_w, batch, seq, heads, dk, dv, hp=None):
    m = qk.shape[0]
    hp = heads if hp is None else hp
    L = MLSTM_CHUNK
    nc = seq // L
    groups = heads // hp
    v_width = heads * dv
    k_off = groups
    kern = functools.partial(_mlstm_kernel, chunk=L, heads=heads, hp=hp, dk=dk, dv=dv)
    return pl.pallas_call(
        kern,
        out_shape=jax.ShapeDtypeStruct((m, v_width), BF16),
        grid=(batch, groups, nc),
        in_specs=[
            pl.BlockSpec((L, hp * dk), lambda b, g, c: (b * nc + c, g)),
            pl.BlockSpec((L, hp * dk), lambda b, g, c: (b * nc + c, k_off + g)),
            pl.BlockSpec((L, hp * dv), lambda b, g, c: (b * nc + c, g)),
            pl.BlockSpec((L, hp * dv), lambda b, g, c: (b * nc + c, g)),
            pl.BlockSpec((None, 2 * heads, L), lambda b, g, c: (b, 0, c)),
            pl.BlockSpec((L, LANES), lambda b, g, c: (b * nc + c, 0)),
            pl.BlockSpec((1, hp * dv), lambda b, g, c: (0, g)),
        ],
        out_specs=pl.BlockSpec((L, hp * dv), lambda b, g, c: (b * nc + c, g)),
        scratch_shapes=[pltpu.VMEM((hp, dk, dv + LANES), F32), pltpu.VMEM((hp, 1, 1), F32)],
        compiler_params=_cparams("parallel", "parallel", "arbitrary"),
        name="mlstm_core",
    )(qk, qk, v, gz, rows, cols, norm_w.reshape(1, v_width))


def _moba_layer(xf, h, gate, w_in, q_norm, k_norm, w_out, batch, seq):
    hd = q_norm.shape[0]
    heads = w_in.shape[1] // 4 // hd
    width = heads * hd
    w_in = w_in.astype(BF16)
    qk = _attn_qk_proj(h, w_in, q_norm, k_norm, seq)
    v = _proj_act(h, w_in, 2 * width, width, None, "attn_v_proj")
    z_act = _proj_act(h, w_in, 3 * width, width, _silu, "attn_z_proj")
    u = _moba_attention(qk, v, z_act, batch, seq, heads, hd)
    return _outproj(u, w_out.astype(BF16), xf, gate, seq)


def _mlstm_layer(xf, h, gate, w_in, gate_b, conv_w, norm_w, w_out, batch, seq):
    heads = gate_b.shape[0] // 2
    v_width = w_out.shape[0]
    dv = v_width // heads
    qk_width = conv_w.shape[1] // 2
    dk = qk_width // heads
    n_main = 2 * qk_width + 3 * v_width
    w_main = w_in[:, :n_main].astype(BF16)
    qk = _mlstm_qk_proj(h, w_main, conv_w, qk_width, dk, seq)
    v = _proj_act(h, w_main, 2 * qk_width, v_width, None, "mlstm_v_proj")
    gz = _mlstm_gz_proj(h, w_main, 2 * qk_width + v_width, 2 * qk_width + 2 * v_width, v_width)
    rows, cols = _gate_proj(h, w_in[:, n_main:], gate_b, batch, seq)
    u = _mlstm_core(qk, v, gz, rows, cols, norm_w, batch, seq, heads, dk, dv)
    return _outproj(u, w_out.astype(BF16), xf, gate, seq)


def kernel(x, c, norm_w, ada_w, ada_b, att_w_in, att_q_norm, att_k_norm, att_w_out, mlstm_w_in,
           mlstm_gate_b, mlstm_conv_w, mlstm_norm_w, mlstm_w_out):
    batch, seq, d = x.shape
    depth = norm_w.shape[0]
    mod = _adaln(c, ada_w, ada_b)
    xf = x.reshape(batch * seq, d)
    for layer in range(depth):
        shift, scale, gate = mod[layer, :, :d], mod[layer, :, d:2 * d], mod[layer, :, 2 * d:]
        h = _norm_mod(xf, norm_w[layer], scale, shift, seq)
        j = layer // 2
        if layer % 2 == 0:
            xf = _moba_layer(xf, h, gate, att_w_in[j], att_q_norm[j], att_k_norm[j], att_w_out[j], batch, seq)
        else:
            xf = _mlstm_layer(xf, h, gate, mlstm_w_in[j], mlstm_gate_b[j], mlstm_conv_w[j],
                              mlstm_norm_w[j], mlstm_w_out[j], batch, seq)
    return xf.reshape(batch, seq, d)
```

```python
import functools

import jax
import jax.numpy as jnp
from jax import lax
from jax.experimental import pallas as pl
from jax.experimental.pallas import tpu as pltpu

F32 = jnp.float32
BF16 = jnp.bfloat16
HIGHEST = lax.Precision.HIGHEST

NORM_EPS = 1e-6
MOBA_BLOCK = 256
MOBA_TOPK = 3
ROPE_THETA = 10000.0
MLSTM_CHUNK = 256
MASK_BIAS = -1e30
LANES = 128
LOG2_E = 1.4426950408889634
ROW_SUB = 512

_VMEM_LIMIT = 48 * 1024 * 1024


def _sigmoid(x):
    return 1.0 / (1.0 + jnp.exp(-x))


def _cparams(*sem):
    return pltpu.CompilerParams(dimension_semantics=sem, vmem_limit_bytes=_VMEM_LIMIT)


def _adaln_kernel(c_ref, w_ref, b_ref, o_ref):
    c = c_ref[...]
    cond = c * _sigmoid(c)
    o_ref[...] = jnp.dot(cond, w_ref[...], preferred_element_type=F32, precision=HIGHEST) + b_ref[...]


def _adaln(c, ada_w, ada_b, tn=768):
    depth, d, n3 = ada_w.shape
    b = c.shape[0]
    rows = 8
    c_pad = jnp.zeros((rows, d), F32).at[:b].set(c)
    out = pl.pallas_call(
        _adaln_kernel,
        out_shape=jax.ShapeDtypeStruct((depth, rows, n3), F32),
        grid=(depth, n3 // tn),
        in_specs=[
            pl.BlockSpec((rows, d), lambda l, n: (0, 0)),
            pl.BlockSpec((None, d, tn), lambda l, n: (l, 0, n)),
            pl.BlockSpec((None, 1, tn), lambda l, n: (l, 0, n)),
        ],
        out_specs=pl.BlockSpec((None, rows, tn), lambda l, n: (l, 0, n)),
        compiler_params=_cparams("parallel", "parallel"),
        name="adaln_mod",
    )(c_pad, ada_w, ada_b.reshape(depth, 1, n3))
    return out[:, :b]


def _norm_mod_kernel(x_ref, nw_ref, sc_ref, sh_ref, o_ref):
    x = x_ref[...]
    ms = jnp.mean(x * x, axis=-1, keepdims=True)
    y = x * lax.rsqrt(ms + NORM_EPS) * nw_ref[...]
    o_ref[...] = (y * (1.0 + sc_ref[...]) + sh_ref[...]).astype(BF16)


def _norm_mod(xf, nw, scale, shift, seq, ts=512):
    m, d = xf.shape
    b = m // seq
    tps = seq // ts
    return pl.pallas_call(
        _norm_mod_kernel,
        out_shape=jax.ShapeDtypeStruct((m, d), BF16),
        grid=(m // ts,),
        in_specs=[
            pl.BlockSpec((ts, d), lambda i: (i, 0)),
            pl.BlockSpec((1, d), lambda i: (0, 0)),
            pl.BlockSpec((None, 1, d), lambda i: (i // tps, 0, 0)),
            pl.BlockSpec((None, 1, d), lambda i: (i // tps, 0, 0)),
        ],
        out_specs=pl.BlockSpec((ts, d), lambda i: (i, 0)),
        compiler_params=_cparams("parallel"),
        name="norm_mod",
    )(xf, nw.reshape(1, d), scale.reshape(b, 1, d), shift.reshape(b, 1, d))


def _row_tiles(a_ref):
    return [slice(r, r + ROW_SUB) for r in range(0, a_ref.shape[0], ROW_SUB)]


def _silu(x):
    return x * _sigmoid(x)


def _mm(a, w, w_t):
    dims = (((1,), (1,)), ((), ())) if w_t else (((1,), (0,)), ((), ()))
    return lax.dot_general(a, w, dims, preferred_element_type=F32)


def _w_spec(kdim, tn, off, w_t):
    if w_t:
        return pl.BlockSpec((tn, kdim), lambda n, i: (off + n, 0))
    return pl.BlockSpec((kdim, tn), lambda n, i: (0, off + n))


def _proj_act_kernel(h_ref, w_ref, o_ref, *, act, w_t):
    for rows in _row_tiles(h_ref):
        acc = _mm(h_ref[rows, :], w_ref[...], w_t)
        o_ref[rows, :] = (acc if act is None else act(acc)).astype(BF16)


def _proj_act(h, w, col0, width, act, name, w_t=False, tm=1024, tn=1024):
    m, kdim = h.shape
    return pl.pallas_call(
        functools.partial(_proj_act_kernel, act=act, w_t=w_t),
        out_shape=jax.ShapeDtypeStruct((m, width), BF16),
        grid=(width // tn, m // tm),
        in_specs=[
            pl.BlockSpec((tm, kdim), lambda n, i: (i, 0)),
            _w_spec(kdim, tn, col0 // tn, w_t),
        ],
        out_specs=pl.BlockSpec((tm, tn), lambda n, i: (i, n)),
        compiler_params=_cparams("parallel", "parallel"),
        name=name,
    )(h, w)


def _staged(tiles, issue, finish):
    issue(0)
    for r in range(len(tiles)):
        if r + 1 < len(tiles):
            issue(r + 1)
        finish(r)


def _attn_qk_kernel(h_ref, w_ref, cos_ref, sin_ref, nw_ref, sc_ref, o_ref, acc_ref, *, head_dim):
    tiles = _row_tiles(h_ref)

    def issue(r):
        acc_ref[r % 2] = jnp.dot(h_ref[tiles[r], :], w_ref[...], preferred_element_type=F32)

    def finish(r):
        rows = tiles[r]
        cos = cos_ref[rows, :]
        sin = sin_ref[rows, :]
        for hh in range(acc_ref.shape[2] // head_dim):
            sl = slice(hh * head_dim, (hh + 1) * head_dim)
            xh = acc_ref[r % 2, :, sl]
            ms = jnp.mean(xh * xh, axis=-1, keepdims=True)
            y = xh * lax.rsqrt(ms + NORM_EPS) * nw_ref[:, sl]
            rot = y * cos + pltpu.roll(y, head_dim // 2, axis=1) * sin
            o_ref[rows, sl] = (rot * sc_ref[:, sl]).astype(BF16)

    _staged(tiles, issue, finish)


def _attn_qk_proj(h, w_in, q_norm, k_norm, seq, tm=2048, tn=1024):
    m, kdim = h.shape
    width = w_in.shape[1] // 4
    hd = q_norm.shape[0]
    heads = width // hd
    tps = seq // tm
    half = hd // 2
    inv = ROPE_THETA ** (-jnp.arange(half, dtype=F32) / half)
    ang = jnp.arange(seq, dtype=F32)[:, None] * inv[None, :]
    cos = jnp.cos(ang)
    sin = jnp.sin(ang)
    cos_t = jnp.concatenate([cos, cos], axis=-1)
    sin_t = jnp.concatenate([-sin, sin], axis=-1)
    nw = jnp.concatenate([jnp.tile(q_norm, heads), jnp.tile(k_norm, heads)]).reshape(1, 2 * width)
    q_scale = float(hd) ** -0.5 * LOG2_E
    sc = jnp.concatenate([jnp.full((width,), q_scale, F32), jnp.ones((width,), F32)]).reshape(1, 2 * width)
    return pl.pallas_call(
        functools.partial(_attn_qk_kernel, head_dim=hd),
        out_shape=jax.ShapeDtypeStruct((m, 2 * width), BF16),
        grid=(2 * width // tn, m // tm),
        in_specs=[
            pl.BlockSpec((tm, kdim), lambda n, i: (i, 0)),
            pl.BlockSpec((kdim, tn), lambda n, i: (0, n)),
            pl.BlockSpec((tm, hd), lambda n, i: (i % tps, 0)),
            pl.BlockSpec((tm, hd), lambda n, i: (i % tps, 0)),
            pl.BlockSpec((1, tn), lambda n, i: (0, n)),
            pl.BlockSpec((1, tn), lambda n, i: (0, n)),
        ],
        out_specs=pl.BlockSpec((tm, tn), lambda n, i: (i, n)),
        scratch_shapes=[pltpu.VMEM((2, ROW_SUB, tn), F32)],
        compiler_params=_cparams("parallel", "parallel"),
        name="attn_qk_proj",
    )(h, w_in, cos_t, sin_t, nw, sc)


def _moba_kernel(q_ref, k_ref, v_ref, z_ref, e_ref, u_ref,
                 qaugt_ref, kaug_ref, vt_ref, m_ref, acc_ref, sbuf_ref, cbuf_ref, abuf_ref, pbuf_ref, *,
                 bs, nb, topk, hd, hp, pw, pair_list):
    seq = nb * bs
    nt = (((1,), (1,)), ((), ()))

    for h in range(hp):
        hs = slice(h * hd, (h + 1) * hd)
        kf = k_ref[:, hs].astype(F32)
        kmean = jnp.sum(kf.reshape(nb, bs, hd), axis=1) * (1.0 / bs)
        g_t = lax.dot_general(kmean, q_ref[:, hs].astype(F32), nt, preferred_element_type=F32,
                              precision=HIGHEST)
        jj = lax.broadcasted_iota(jnp.int32, (nb, seq), 0)
        qb = lax.broadcasted_iota(jnp.int32, (nb, seq), 1) // bs
        rank = jnp.zeros((nb, seq), jnp.int32)
        for jp in range(nb):
            row = g_t[jp:jp + 1, :]
            beats = (row > g_t) | ((row == g_t) & (jp < jj))
            rank = rank + jnp.where(beats & (jp < qb), 1, 0)
        allowed = ((jj < qb) & (rank < topk)) | (jj == qb)
        bias_t = jnp.where(allowed, 0.0, MASK_BIAS).astype(BF16)
        qaugt_ref[h, hd:hd + nb, :] = bias_t
        qaugt_ref[h, hd + nb:hd + LANES, :] = jnp.zeros((LANES - nb, seq), BF16)
        kaug_ref[h, :, 0:hd] = k_ref[:, hs]
        kaug_ref[h, :, hd:hd + LANES] = e_ref[...]
        for jb in range(nb):
            rs = slice(jb * bs, (jb + 1) * bs)
            qaugt_ref[h, 0:hd, rs] = q_ref[rs, hs].astype(F32).T.astype(BF16)
            vt_ref[h, 0:hd, rs] = v_ref[rs, hs].astype(F32).T.astype(BF16)
        ones_row = lax.broadcasted_iota(jnp.int32, (8, seq), 0) == 0
        vt_ref[h, hd:hd + 8, :] = jnp.where(ones_row, 1.0, 0.0).astype(BF16)
    m_ref[...] = jnp.full(m_ref.shape, -jnp.inf, F32)
    acc_ref[...] = jnp.zeros(acc_ref.shape, F32)

    key_id = lax.broadcasted_iota(jnp.int32, (bs, bs), 0)
    qry_id = lax.broadcasted_iota(jnp.int32, (bs, bs), 1)
    causal = key_id <= qry_id
    steps = [[(h, i, j) for (i, j) in pair_list[t:t + pw] for h in range(hp)]
             for t in range(0, len(pair_list), pw)]

    def stage_scores(us, slot):
        for n, (h, i, j) in enumerate(us):
            s = jnp.dot(kaug_ref[h, j * bs:(j + 1) * bs, :], qaugt_ref[h, :, i * bs:(i + 1) * bs],
                        preferred_element_type=F32)
            if i == j:
                s = jnp.where(causal, s, -jnp.inf)
            sbuf_ref[slot, n] = s
            cbuf_ref[slot, n] = jnp.max(s, axis=0, keepdims=True)

    def stage_softmax(us, slot):
        for n, (h, i, _) in enumerate(us):
            m_old = m_ref[h, i]
            m_new = jnp.maximum(m_old, cbuf_ref[slot, n])
            abuf_ref[slot, n] = jnp.exp2(m_old - m_new)
            m_ref[h, i] = m_new
            pbuf_ref[slot, n] = jnp.exp2(sbuf_ref[slot, n] - m_new).astype(BF16)

    def stage_pv(us, slot):
        for n, (h, i, j) in enumerate(us):
            pv = jnp.dot(vt_ref[h, :, j * bs:(j + 1) * bs], pbuf_ref[slot, n], preferred_element_type=F32)
            acc_ref[h, i] = abuf_ref[slot, n] * acc_ref[h, i] + pv

    def finalize(h, i):
        rows, hs = slice(i * bs, (i + 1) * bs), slice(h * hd, (h + 1) * hd)
        acc = acc_ref[h, i]
        o = (acc[0:hd, :] / acc[hd:hd + 1, :]).T
        u_ref[rows, hs] = (o * z_ref[rows, hs].astype(F32)).astype(BF16)

    last_step = {(h, i): t for t, us in enumerate(steps) for (h, i, _) in us}
    for t in range(len(steps) + 2):
        if t < len(steps):
            stage_scores(steps[t], t % 2)
        if 1 <= t <= len(steps):
            stage_softmax(steps[t - 1], (t - 1) % 2)
        if t >= 2:
            stage_pv(steps[t - 2], t % 2)
            for (h, i), t_last in last_step.items():
                if t_last == t - 2:
                    finalize(h, i)


def _moba_attention(qk, v, z_act, batch, seq, heads, hd, hp=1):
    m = qk.shape[0]
    bs = MOBA_BLOCK
    nb = seq // bs
    width = heads * hd
    groups = heads // hp
    e_tab = (jnp.arange(seq)[:, None] // bs == jnp.arange(LANES)[None, :]).astype(BF16)
    pair_list = [(i, i) for i in range(nb)] + [(i, j) for j in range(nb) for i in range(j + 1, nb)]
    pw = 4
    kern = functools.partial(_moba_kernel, bs=bs, nb=nb, topk=min(MOBA_TOPK, nb), hd=hd, hp=hp, pw=pw,
                             pair_list=tuple(pair_list))
    blk = (seq, hp * hd)
    return pl.pallas_call(
        kern,
        out_shape=jax.ShapeDtypeStruct((m, width), BF16),
        grid=(batch, groups),
        in_specs=[
            pl.BlockSpec(blk, lambda b, g: (b, g)),
            pl.BlockSpec(blk, lambda b, g: (b, groups + g)),
            pl.BlockSpec(blk, lambda b, g: (b, g)),
            pl.BlockSpec(blk, lambda b, g: (b, g)),
            pl.BlockSpec((seq, LANES), lambda b, g: (0, 0)),
        ],
        out_specs=pl.BlockSpec(blk, lambda b, g: (b, g)),
        scratch_shapes=[
            pltpu.VMEM((hp, hd + LANES, seq), BF16),
            pltpu.VMEM((hp, seq, hd + LANES), BF16),
            pltpu.VMEM((hp, hd + 8, seq), BF16),
            pltpu.VMEM((hp, nb, 1, bs), F32),
            pltpu.VMEM((hp, nb, hd + 8, bs), F32),
            pltpu.VMEM((2, pw * hp, bs, bs), F32),
            pltpu.VMEM((2, pw * hp, 1, bs), F32),
            pltpu.VMEM((2, pw * hp, 1, bs), F32),
            pltpu.VMEM((2, pw * hp, bs, bs), BF16),
        ],
        compiler_params=_cparams("parallel", "parallel"),
        name="moba_attn",
    )(qk, qk, v, z_act, e_tab)


def _outproj_kernel(u_ref, w_ref, x_ref, g_ref, o_ref):
    acc = jnp.dot(u_ref[...], w_ref[...], preferred_element_type=F32)
    o_ref[...] = x_ref[...] + g_ref[...] * acc


def _outproj(u, w_out, xf, gate, seq, tm=512, tn=1024):
    m, kdim = u.shape
    d = w_out.shape[1]
    b = m // seq
    tps = seq // tm
    return pl.pallas_call(
        _outproj_kernel,
        out_shape=jax.ShapeDtypeStruct((m, d), F32),
        grid=(d // tn, m // tm),
        in_specs=[
            pl.BlockSpec((tm, kdim), lambda n, i: (i, 0)),
            pl.BlockSpec((kdim, tn), lambda n, i: (0, n)),
            pl.BlockSpec((tm, tn), lambda n, i: (i, n)),
            pl.BlockSpec((None, 1, tn), lambda n, i: (i // tps, 0, n)),
        ],
        out_specs=pl.BlockSpec((tm, tn), lambda n, i: (i, n)),
        compiler_params=_cparams("parallel", "parallel"),
        name="outproj",
    )(u, w_out, xf, gate.reshape(b, 1, d))


def _mlstm_qk_kernel(h_ref, w_ref, cw_ref, sc_ref, o_ref, ext_ref, *, tiles_per_seq, conv_k):
    i = pl.program_id(1)
    sub = ROW_SUB
    tiles = _row_tiles(h_ref)

    @pl.when(i % tiles_per_seq == 0)
    def _():
        ext_ref[0, 0:8, :] = jnp.zeros((8, ext_ref.shape[2]), F32)

    def issue(r):
        ext_ref[r % 2, 8:sub + 8, :] = _mm(h_ref[tiles[r], :], w_ref[...], False)

    def finish(r):
        cur = r % 2
        y = cw_ref[conv_k - 1:conv_k, :] * ext_ref[cur, 8:sub + 8, :]
        for back in range(1, conv_k):
            y = y + cw_ref[conv_k - 1 - back:conv_k - back, :] * ext_ref[cur, 8 - back:8 - back + sub, :]
        o_ref[tiles[r], :] = (_silu(y) * sc_ref[...]).astype(BF16)
        ext_ref[1 - cur, 0:8, :] = ext_ref[cur, sub:sub + 8, :]

    _staged(tiles, issue, finish)


def _mlstm_qk_proj(h, w_in, conv_w, qk_width, dk, seq, tm=2048, tn=1024):
    m, kdim = h.shape
    conv_k = conv_w.shape[0]
    tps = seq // tm
    sc = jnp.concatenate([jnp.ones((qk_width,), F32),
                          jnp.full((qk_width,), float(dk) ** -0.5, F32)]).reshape(1, 2 * qk_width)
    return pl.pallas_call(
        functools.partial(_mlstm_qk_kernel, tiles_per_seq=tps, conv_k=conv_k),
        out_shape=jax.ShapeDtypeStruct((m, 2 * qk_width), BF16),
        grid=(2 * qk_width // tn, m // tm),
        in_specs=[
            pl.BlockSpec((tm, kdim), lambda n, i: (i, 0)),
            _w_spec(kdim, tn, 0, False),
            pl.BlockSpec((conv_k, tn), lambda n, i: (0, n)),
            pl.BlockSpec((1, tn), lambda n, i: (0, n)),
        ],
        out_specs=pl.BlockSpec((tm, tn), lambda n, i: (i, n)),
        scratch_shapes=[pltpu.VMEM((2, ROW_SUB + 8, tn), F32)],
        compiler_params=_cparams("arbitrary", "arbitrary"),
        name="mlstm_qk_proj",
    )(h, w_in, conv_w, sc)


def _mlstm_gz_kernel(h_ref, wo_ref, wz_ref, o_ref):
    for rows in _row_tiles(h_ref):
        a = h_ref[rows, :]
        og = _mm(a, wo_ref[...], False)
        z = _mm(a, wz_ref[...], False)
        o_ref[rows, :] = (_sigmoid(og) * _silu(z)).astype(BF16)


def _mlstm_gz_proj(h, w_in, o_col0, z_col0, width, tm=1024, tn=1024):
    m, kdim = h.shape
    return pl.pallas_call(
        _mlstm_gz_kernel,
        out_shape=jax.ShapeDtypeStruct((m, width), BF16),
        grid=(width // tn, m // tm),
        in_specs=[
            pl.BlockSpec((tm, kdim), lambda n, i: (i, 0)),
            _w_spec(kdim, tn, o_col0 // tn, False),
            _w_spec(kdim, tn, z_col0 // tn, False),
        ],
        out_specs=pl.BlockSpec((tm, tn), lambda n, i: (i, n)),
        compiler_params=_cparams("parallel", "parallel"),
        name="mlstm_gz_proj",
    )(h, w_in, w_in)


def _gate_proj_kernel(h_ref, w_ref, b_ref, rows_ref, cols_ref, *, heads, chunk):
    g = jnp.dot(h_ref[...], w_ref[...].astype(BF16), preferred_element_type=F32) + b_ref[...]
    tm = g.shape[0]
    g_t = g.T
    i_rows = g_t[0:heads, :]
    f_rows = g_t[heads:2 * heads, :]
    b = jnp.minimum(f_rows, 0.0) - jnp.log1p(jnp.exp(-jnp.abs(f_rows)))
    pos = lax.broadcasted_iota(jnp.int32, b.shape, 1) % chunk
    shift = 1
    while shift < chunk:
        b = b + jnp.where(pos >= shift, pltpu.roll(b, shift, axis=1), 0.0)
        shift *= 2
    rows = jnp.concatenate([i_rows, b], axis=0)
    rows_ref[...] = rows
    cols_ref[...] = jnp.concatenate([rows, jnp.zeros((LANES - 2 * heads, tm), F32)], axis=0).T


def _gate_proj(h, w_gate, gate_b, batch, seq, tm=1024):
    m, kdim = h.shape
    ng = w_gate.shape[1]
    heads = ng // 2
    tps = seq // tm
    w_pad = jnp.zeros((kdim, LANES), F32).at[:, :ng].set(w_gate)
    b_pad = jnp.zeros((1, LANES), F32).at[0, :ng].set(gate_b)
    kern = functools.partial(_gate_proj_kernel, heads=heads, chunk=MLSTM_CHUNK)
    return pl.pallas_call(
        kern,
        out_shape=(jax.ShapeDtypeStruct((batch, ng, seq), F32), jax.ShapeDtypeStruct((m, LANES), F32)),
        grid=(m // tm,),
        in_specs=[
            pl.BlockSpec((tm, kdim), lambda i: (i, 0)),
            pl.BlockSpec((kdim, LANES), lambda i: (0, 0)),
            pl.BlockSpec((1, LANES), lambda i: (0, 0)),
        ],
        out_specs=(pl.BlockSpec((None, ng, tm), lambda i: (i // tps, 0, i % tps)),
                   pl.BlockSpec((tm, LANES), lambda i: (i, 0))),
        compiler_params=_cparams("parallel"),
        name="mlstm_gate_proj",
    )(h, w_pad, b_pad)


def _mlstm_kernel(q_ref, k_ref, v_ref, gz_ref, rows_ref, cols_ref, nw_ref, u_ref, c_ref, m_ref, *,
                  chunk, heads, hp, dk, dv):
    g = pl.program_id(1)
    c = pl.program_id(2)
    L = chunk
    nt = (((1,), (1,)), ((), ()))

    @pl.when(c == 0)
    def _():
        c_ref[...] = jnp.zeros(c_ref.shape, F32)
        m_ref[...] = jnp.zeros(m_ref.shape, F32)

    t_id = lax.broadcasted_iota(jnp.int32, (L, L), 0)
    s_id = lax.broadcasted_iota(jnp.int32, (L, L), 1)
    causal = s_id <= t_id
    lane = lax.broadcasted_iota(jnp.int32, (L, LANES), 1)
    ones_blk = jnp.where(lane == 0, 1.0, 0.0).astype(BF16)
    cols = cols_ref[...]

    def col_of(idx):
        return jnp.sum(jnp.where(lane == idx, cols, 0.0), axis=1, keepdims=True)

    gate = []
    for hh in range(hp):
        h = g * hp + hh
        i_row = rows_ref[pl.ds(h, 1), :]
        b_row = rows_ref[pl.ds(heads + h, 1), :]
        i_col = col_of(h)
        b_col = col_of(heads + h)
        m_prev = m_ref[hh]
        log_d = jnp.where(causal, b_col - b_row + i_row, -jnp.inf)
        log_inter = b_col + m_prev
        m_t = jnp.maximum(log_inter, jnp.max(log_d, axis=1, keepdims=True))
        d_mat = jnp.exp(log_d - m_t)
        w_inter = jnp.exp(log_inter - m_t)
        b_last = b_row[:, L - 1:L]
        m_new = jnp.maximum(b_last + m_prev, jnp.max(b_last - b_row + i_row, axis=1, keepdims=True))
        decay = jnp.exp(b_last + m_prev - m_new)
        ws_col = jnp.exp(b_last - b_col + i_col - m_new)
        m_ref[hh] = m_new
        gate.append((m_t, d_mat, w_inter, decay, ws_col))

    s_qk = []
    for hh in range(hp):
        q = q_ref[:, hh * dk:(hh + 1) * dk]
        k = k_ref[:, hh * dk:(hh + 1) * dk]
        s_qk.append((lax.dot_general(q, k, nt, preferred_element_type=F32) * gate[hh][1]).astype(BF16))

    for hh in range(hp):
        m_t, _, w_inter, _, _ = gate[hh]
        q_scaled = (w_inter * q_ref[:, hh * dk:(hh + 1) * dk].astype(F32)).astype(BF16)
        v_aug = jnp.concatenate([v_ref[:, hh * dv:(hh + 1) * dv], ones_blk], axis=1)
        lhs = jnp.concatenate([q_scaled, s_qk[hh]], axis=1)
        rhs = jnp.concatenate([c_ref[hh].astype(BF16), v_aug], axis=0)
        num = jnp.dot(lhs, rhs, preferred_element_type=F32)
        d = jnp.maximum(jnp.abs(num[:, dv:dv + 1]), jnp.exp(-m_t))
        nh = num[:, :dv]
        r = lax.rsqrt(jnp.mean(nh * nh, axis=-1, keepdims=True) + NORM_EPS * (d * d))
        hn = nh * r * nw_ref[:, hh * dv:(hh + 1) * dv]
        u_ref[:, hh * dv:(hh + 1) * dv] = (hn * gz_ref[:, hh * dv:(hh + 1) * dv].astype(F32)).astype(BF16)

    for hh in range(hp):
        _, _, _, decay, ws_col = gate[hh]
        k_scaled = (ws_col * k_ref[:, hh * dk:(hh + 1) * dk].astype(F32)).astype(BF16)
        v_aug = jnp.concatenate([v_ref[:, hh * dv:(hh + 1) * dv], ones_blk], axis=1)
        upd = lax.dot_general(k_scaled, v_aug, (((0,), (0,)), ((), ())), preferred_element_type=F32)
        c_ref[hh] = decay * c_ref[hh] + upd


def _mlstm_core(qk, v, gz, rows, cols, norm_w, batch, seq, heads, dk, dv, hp=None):
    m = qk.shape[0]
    hp = heads if hp is None else hp
    L = MLSTM_CHUNK
    nc = seq // L
    groups = heads // hp
    v_width = heads * dv
    k_off = groups
    kern = functools.partial(_mlstm_kernel, chunk=L, heads=heads, hp=hp, dk=dk, dv=dv)
    return pl.pallas_call(
        kern,
        out_shape=jax.ShapeDtypeStruct((m, v_width), BF16),
        grid=(batch, groups, nc),
        in_specs=[
            pl.BlockSpec((L, hp * dk), lambda b, g, c: (b * nc + c, g)),
            pl.BlockSpec((L, hp * dk), lambda b, g, c: (b * nc + c, k_off + g)),
            pl.BlockSpec((L, hp * dv), lambda b, g, c: (b * nc + c, g)),
            pl.BlockSpec((L, hp * dv), lambda b, g, c: (b * nc + c, g)),
            pl.BlockSpec((None, 2 * heads, L), lambda b, g, c: (b, 0, c)),
            pl.BlockSpec((L, LANES), lambda b, g, c: (b * nc + c, 0)),
            pl.BlockSpec((1, hp * dv), lambda b, g, c: (0, g)),
        ],
        out_specs=pl.BlockSpec((L, hp * dv), lambda b, g, c: (b * nc + c, g)),
        scratch_shapes=[pltpu.VMEM((hp, dk, dv + LANES), F32), pltpu.VMEM((hp, 1, 1), F32)],
        compiler_params=_cparams("parallel", "parallel", "arbitrary"),
        name="mlstm_core",
    )(qk, qk, v, gz, rows, cols, norm_w.reshape(1, v_width))


def _moba_layer(xf, h, gate, w_in, q_norm, k_norm, w_out, batch, seq):
    hd = q_norm.shape[0]
    heads = w_in.shape[1] // 4 // hd
    width = heads * hd
    w_in = w_in.astype(BF16)
    qk = _attn_qk_proj(h, w_in, q_norm, k_norm, seq)
    v = _proj_act(h, w_in, 2 * width, width, None, "attn_v_proj")
    z_act = _proj_act(h, w_in, 3 * width, width, _silu, "attn_z_proj")
    u = _moba_attention(qk, v, z_act, batch, seq, heads, hd)
    return _outproj(u, w_out.astype(BF16), xf, gate, seq)


def _mlstm_layer(xf, h, gate, w_in, gate_b, conv_w, norm_w, w_out, batch, seq):
    heads = gate_b.shape[0] // 2
    v_width = w_out.shape[0]
    dv = v_width // heads
    qk_width = conv_w.shape[1] // 2
    dk = qk_width // heads
    n_main = 2 * qk_width + 3 * v_width
    w_main = w_in[:, :n_main].astype(BF16)
    qk = _mlstm_qk_proj(h, w_main, conv_w, qk_width, dk, seq)
    v = _proj_act(h, w_main, 2 * qk_width, v_width, None, "mlstm_v_proj")
    gz = _mlstm_gz_proj(h, w_main, 2 * qk_width + v_width, 2 * qk_width + 2 * v_width, v_width)
    rows, cols = _gate_proj(h, w_in[:, n_main:], gate_b, batch, seq)
    u = _mlstm_core(qk, v, gz, rows, cols, norm_w, batch, seq, heads, dk, dv)
    return _outproj(u, w_out.astype(BF16), xf, gate, seq)


def kernel(x, c, norm_w, ada_w, ada_b, att_w_in, att_q_norm, att_k_norm, att_w_out, mlstm_w_in,
           mlstm_gate_b, mlstm_conv_w, mlstm_norm_w, mlstm_w_out):
    batch, seq, d = x.shape
    depth = norm_w.shape[0]
    mod = _adaln(c, ada_w, ada_b)
    xf = x.reshape(batch * seq, d)
    for layer in range(depth):
        shift, scale, gate = mod[layer, :, :d], mod[layer, :, d:2 * d], mod[layer, :, 2 * d:]
        h = _norm_mod(xf, norm_w[layer], scale, shift, seq)
        j = layer // 2
        if layer % 2 == 0:
            xf = _moba_layer(xf, h, gate, att_w_in[j], att_q_norm[j], att_k_norm[j], att_w_out[j], batch, seq)
        else:
            xf = _mlstm_layer(xf, h, gate, mlstm_w_in[j], mlstm_gate_b[j], mlstm_conv_w[j],
                              mlstm_norm_w[j], mlstm_w_out[j], batch, seq)
    return xf.reshape(batch, seq, d)
```

```python
import functools

import jax
import jax.numpy as jnp
from jax import lax
from jax.experimental import pallas as pl
from jax.experimental.pallas import tpu as pltpu

F32 = jnp.float32
BF16 = jnp.bfloat16
HIGHEST = lax.Precision.HIGHEST

NORM_EPS = 1e-6
MOBA_BLOCK = 256
MOBA_TOPK = 3
ROPE_THETA = 10000.0
MLSTM_CHUNK = 256
MASK_BIAS = -1e30
LANES = 128
LOG2_E = 1.4426950408889634
ROW_SUB = 512

_VMEM_LIMIT = 48 * 1024 * 1024


def _sigmoid(x):
    return 1.0 / (1.0 + jnp.exp(-x))


def _cparams(*sem):
    return pltpu.CompilerParams(dimension_semantics=sem, vmem_limit_bytes=_VMEM_LIMIT)


def _adaln_kernel(c_ref, w_ref, b_ref, o_ref):
    c = c_ref[...]
    cond = c * _sigmoid(c)
    o_ref[...] = jnp.dot(cond, w_ref[...], preferred_element_type=F32, precision=HIGHEST) + b_ref[...]


def _adaln(c, ada_w, ada_b, tn=768):
    depth, d, n3 = ada_w.shape
    b = c.shape[0]
    rows = 8
    c_pad = jnp.zeros((rows, d), F32).at[:b].set(c)
    out = pl.pallas_call(
        _adaln_kernel,
        out_shape=jax.ShapeDtypeStruct((depth, rows, n3), F32),
        grid=(depth, n3 // tn),
        in_specs=[
            pl.BlockSpec((rows, d), lambda l, n: (0, 0)),
            pl.BlockSpec((None, d, tn), lambda l, n: (l, 0, n)),
            pl.BlockSpec((None, 1, tn), lambda l, n: (l, 0, n)),
        ],
        out_specs=pl.BlockSpec((None, rows, tn), lambda l, n: (l, 0, n)),
        compiler_params=_cparams("parallel", "parallel"),
        name="adaln_mod",
    )(c_pad, ada_w, ada_b.reshape(depth, 1, n3))
    return out[:, :b]


def _norm_mod_kernel(x_ref, nw_ref, sc_ref, sh_ref, o_ref):
    x = x_ref[...]
    ms = jnp.mean(x * x, axis=-1, keepdims=True)
    y = x * lax.rsqrt(ms + NORM_EPS) * nw_ref[...]
    o_ref[...] = (y * (1.0 + sc_ref[...]) + sh_ref[...]).astype(BF16)


def _norm_mod(xf, nw, scale, shift, seq, ts=512):
    m, d = xf.shape
    b = m // seq
    tps = seq // ts
    return pl.pallas_call(
        _norm_mod_kernel,
        out_shape=jax.ShapeDtypeStruct((m, d), BF16),
        grid=(m // ts,),
        in_specs=[
            pl.BlockSpec((ts, d), lambda i: (i, 0)),
            pl.BlockSpec((1, d), lambda i: (0, 0)),
            pl.BlockSpec((None, 1, d), lambda i: (i // tps, 0, 0)),
            pl.BlockSpec((None, 1, d), lambda i: (i // tps, 0, 0)),
        ],
        out_specs=pl.BlockSpec((ts, d), lambda i: (i, 0)),
        compiler_params=_cparams("parallel"),
        name="norm_mod",
    )(xf, nw.reshape(1, d), scale.reshape(b, 1, d), shift.reshape(b, 1, d))


def _row_tiles(a_ref):
    return [slice(r, r + ROW_SUB) for r in range(0, a_ref.shape[0], ROW_SUB)]


def _silu(x):
    return x * _sigmoid(x)


def _mm(a, w, w_t):
    dims = (((1,), (1,)), ((), ())) if w_t else (((1,), (0,)), ((), ()))
    return lax.dot_general(a, w, dims, preferred_element_type=F32)


def _w_spec(kdim, tn, off, w_t):
    if w_t:
        return pl.BlockSpec((tn, kdim), lambda n, i: (off + n, 0))
    return pl.BlockSpec((kdim, tn), lambda n, i: (0, off + n))


def _proj_act_kernel(h_ref, w_ref, o_ref, *, act, w_t):
    for rows in _row_tiles(h_ref):
        acc = _mm(h_ref[rows, :], w_ref[...], w_t)
        o_ref[rows, :] = (acc if act is None else act(acc)).astype(BF16)


def _proj_act(h, w, col0, width, act, name, w_t=False, tm=1024, tn=1024):
    m, kdim = h.shape
    return pl.pallas_call(
        functools.partial(_proj_act_kernel, act=act, w_t=w_t),
        out_shape=jax.ShapeDtypeStruct((m, width), BF16),
        grid=(width // tn, m // tm),
        in_specs=[
            pl.BlockSpec((tm, kdim), lambda n, i: (i, 0)),
            _w_spec(kdim, tn, col0 // tn, w_t),
        ],
        out_specs=pl.BlockSpec((tm, tn), lambda n, i: (i, n)),
        compiler_params=_cparams("parallel", "parallel"),
        name=name,
    )(h, w)


def _staged(tiles, issue, finish):
    issue(0)
    for r in range(len(tiles)):
        if r + 1 < len(tiles):
            issue(r + 1)
        finish(r)


def _attn_qk_kernel(h_ref, w_ref, cos_ref, sin_ref, nw_ref, sc_ref, o_ref, acc_ref, *, head_dim):
    tiles = _row_tiles(h_ref)

    def issue(r):
        acc_ref[r % 2] = jnp.dot(h_ref[tiles[r], :], w_ref[...], preferred_element_type=F32)

    def finish(r):
        rows = tiles[r]
        cos = cos_ref[rows, :]
        sin = sin_ref[rows, :]
        for hh in range(acc_ref.shape[2] // head_dim):
            sl = slice(hh * head_dim, (hh + 1) * head_dim)
            xh = acc_ref[r % 2, :, sl]
            ms = jnp.mean(xh * xh, axis=-1, keepdims=True)
            y = xh * lax.rsqrt(ms + NORM_EPS) * nw_ref[:, sl]
            rot = y * cos + pltpu.roll(y, head_dim // 2, axis=1) * sin
            o_ref[rows, sl] = (rot * sc_ref[:, sl]).astype(BF16)

    _staged(tiles, issue, finish)


def _attn_qk_proj(h, w_in, q_norm, k_norm, seq, tm=2048, tn=1024):
    m, kdim = h.shape
    width = w_in.shape[1] // 4
    hd = q_norm.shape[0]
    heads = width // hd
    assert seq % tm == 0, "row tiles must not straddle sequences"
    tps = seq // tm
    half = hd // 2
    inv = ROPE_THETA ** (-jnp.arange(half, dtype=F32) / half)
    ang = jnp.arange(seq, dtype=F32)[:, None] * inv[None, :]
    cos = jnp.cos(ang)
    sin = jnp.sin(ang)
    cos_t = jnp.concatenate([cos, cos], axis=-1)
    sin_t = jnp.concatenate([-sin, sin], axis=-1)
    nw = jnp.concatenate([jnp.tile(q_norm, heads), jnp.tile(k_norm, heads)]).reshape(1, 2 * width)
    q_scale = float(hd) ** -0.5 * LOG2_E
    sc = jnp.concatenate([jnp.full((width,), q_scale, F32), jnp.ones((width,), F32)]).reshape(1, 2 * width)
    return pl.pallas_call(
        functools.partial(_attn_qk_kernel, head_dim=hd),
        out_shape=jax.ShapeDtypeStruct((m, 2 * width), BF16),
        grid=(2 * width // tn, m // tm),
        in_specs=[
            pl.BlockSpec((tm, kdim), lambda n, i: (i, 0)),
            pl.BlockSpec((kdim, tn), lambda n, i: (0, n)),
            pl.BlockSpec((tm, hd), lambda n, i: (i % tps, 0)),
            pl.BlockSpec((tm, hd), lambda n, i: (i % tps, 0)),
            pl.BlockSpec((1, tn), lambda n, i: (0, n)),
            pl.BlockSpec((1, tn), lambda n, i: (0, n)),
        ],
        out_specs=pl.BlockSpec((tm, tn), lambda n, i: (i, n)),
        scratch_shapes=[pltpu.VMEM((2, ROW_SUB, tn), F32)],
        compiler_params=_cparams("parallel", "parallel"),
        name="attn_qk_proj",
    )(h, w_in, cos_t, sin_t, nw, sc)


def _moba_kernel(q_ref, k_ref, v_ref, z_ref, e_ref, u_ref,
                 qaugt_ref, kaug_ref, vt_ref, m_ref, acc_ref, sbuf_ref, cbuf_ref, abuf_ref, pbuf_ref, *,
                 bs, nb, topk, hd, hp, pw, pair_list):
    seq = nb * bs
    nt = (((1,), (1,)), ((), ()))

    for h in range(hp):
        hs = slice(h * hd, (h + 1) * hd)
        kf = k_ref[:, hs].astype(F32)
        kmean = jnp.sum(kf.reshape(nb, bs, hd), axis=1) * (1.0 / bs)
        g_t = lax.dot_general(kmean, q_ref[:, hs].astype(F32), nt, preferred_element_type=F32,
                              precision=HIGHEST)
        jj = lax.broadcasted_iota(jnp.int32, (nb, seq), 0)
        qb = lax.broadcasted_iota(jnp.int32, (nb, seq), 1) // bs
        rank = jnp.zeros((nb, seq), jnp.int32)
        for jp in range(nb):
            row = g_t[jp:jp + 1, :]
            beats = (row > g_t) | ((row == g_t) & (jp < jj))
            rank = rank + jnp.where(beats & (jp < qb), 1, 0)
        allowed = ((jj < qb) & (rank < topk)) | (jj == qb)
        bias_t = jnp.where(allowed, 0.0, MASK_BIAS).astype(BF16)
        qaugt_ref[h, hd:hd + nb, :] = bias_t
        qaugt_ref[h, hd + nb:hd + LANES, :] = jnp.zeros((LANES - nb, seq), BF16)
        kaug_ref[h, :, 0:hd] = k_ref[:, hs]
        kaug_ref[h, :, hd:hd + LANES] = e_ref[...]
        for jb in range(nb):
            rs = slice(jb * bs, (jb + 1) * bs)
            qaugt_ref[h, 0:hd, rs] = q_ref[rs, hs].astype(F32).T.astype(BF16)
            vt_ref[h, 0:hd, rs] = v_ref[rs, hs].astype(F32).T.astype(BF16)
        ones_row = lax.broadcasted_iota(jnp.int32, (8, seq), 0) == 0
        vt_ref[h, hd:hd + 8, :] = jnp.where(ones_row, 1.0, 0.0).astype(BF16)
    m_ref[...] = jnp.full(m_ref.shape, -jnp.inf, F32)
    acc_ref[...] = jnp.zeros(acc_ref.shape, F32)

    key_id = lax.broadcasted_iota(jnp.int32, (bs, bs), 0)
    qry_id = lax.broadcasted_iota(jnp.int32, (bs, bs), 1)
    causal = key_id <= qry_id
    steps = [[(h, i, j) for (i, j) in pair_list[t:t + pw] for h in range(hp)]
             for t in range(0, len(pair_list), pw)]

    def stage_scores(us, slot):
        for n, (h, i, j) in enumerate(us):
            s = jnp.dot(kaug_ref[h, j * bs:(j + 1) * bs, :], qaugt_ref[h, :, i * bs:(i + 1) * bs],
                        preferred_element_type=F32)
            if i == j:
                s = jnp.where(causal, s, -jnp.inf)
            sbuf_ref[slot, n] = s
            cbuf_ref[slot, n] = jnp.max(s, axis=0, keepdims=True)

    def stage_softmax(us, slot):
        for n, (h, i, _) in enumerate(us):
            m_old = m_ref[h, i]
            m_new = jnp.maximum(m_old, cbuf_ref[slot, n])
            abuf_ref[slot, n] = jnp.exp2(m_old - m_new)
            m_ref[h, i] = m_new
            pbuf_ref[slot, n] = jnp.exp2(sbuf_ref[slot, n] - m_new).astype(BF16)

    def stage_pv(us, slot):
        for n, (h, i, j) in enumerate(us):
            pv = jnp.dot(vt_ref[h, :, j * bs:(j + 1) * bs], pbuf_ref[slot, n], preferred_element_type=F32)
            acc_ref[h, i] = abuf_ref[slot, n] * acc_ref[h, i] + pv

    def finalize(h, i):
        rows, hs = slice(i * bs, (i + 1) * bs), slice(h * hd, (h + 1) * hd)
        acc = acc_ref[h, i]
        o = (acc[0:hd, :] / acc[hd:hd + 1, :]).T
        u_ref[rows, hs] = (o * z_ref[rows, hs].astype(F32)).astype(BF16)

    last_step = {(h, i): t for t, us in enumerate(steps) for (h, i, _) in us}
    for t in range(len(steps) + 2):
        if t < len(steps):
            stage_scores(steps[t], t % 2)
        if 1 <= t <= len(steps):
            stage_softmax(steps[t - 1], (t - 1) % 2)
        if t >= 2:
            stage_pv(steps[t - 2], t % 2)
            for (h, i), t_last in last_step.items():
                if t_last == t - 2:
                    finalize(h, i)


def _moba_attention(qk, v, z_act, batch, seq, heads, hd, hp=1):
    m = qk.shape[0]
    bs = MOBA_BLOCK
    nb = seq // bs
    width = heads * hd
    groups = heads // hp
    e_tab = (jnp.arange(seq)[:, None] // bs == jnp.arange(LANES)[None, :]).astype(BF16)
    pair_list = [(i, i) for i in range(nb)] + [(i, j) for j in range(nb) for i in range(j + 1, nb)]
    pw = 4
    kern = functools.partial(_moba_kernel, bs=bs, nb=nb, topk=min(MOBA_TOPK, nb), hd=hd, hp=hp, pw=pw,
                             pair_list=tuple(pair_list))
    blk = (seq, hp * hd)
    return pl.pallas_call(
        kern,
        out_shape=jax.ShapeDtypeStruct((m, width), BF16),
        grid=(batch, groups),
        in_specs=[
            pl.BlockSpec(blk, lambda b, g: (b, g)),
            pl.BlockSpec(blk, lambda b, g: (b, groups + g)),
            pl.BlockSpec(blk, lambda b, g: (b, g)),
            pl.BlockSpec(blk, lambda b, g: (b, g)),
            pl.BlockSpec((seq, LANES), lambda b, g: (0, 0)),
        ],
        out_specs=pl.BlockSpec(blk, lambda b, g: (b, g)),
        scratch_shapes=[
            pltpu.VMEM((hp, hd + LANES, seq), BF16),
            pltpu.VMEM((hp, seq, hd + LANES), BF16),
            pltpu.VMEM((hp, hd + 8, seq), BF16),
            pltpu.VMEM((hp, nb, 1, bs), F32),
            pltpu.VMEM((hp, nb, hd + 8, bs), F32),
            pltpu.VMEM((2, pw * hp, bs, bs), F32),
            pltpu.VMEM((2, pw * hp, 1, bs), F32),
            pltpu.VMEM((2, pw * hp, 1, bs), F32),
            pltpu.VMEM((2, pw * hp, bs, bs), BF16),
        ],
        compiler_params=_cparams("parallel", "parallel"),
        name="moba_attn",
    )(qk, qk, v, z_act, e_tab)


def _outproj_kernel(u_ref, w_ref, x_ref, g_ref, o_ref):
    acc = jnp.dot(u_ref[...], w_ref[...], preferred_element_type=F32)
    o_ref[...] = x_ref[...] + g_ref[...] * acc


def _outproj(u, w_out, xf, gate, seq, tm=512, tn=1024):
    m, kdim = u.shape
    d = w_out.shape[1]
    b = m // seq
    assert seq % tm == 0, "row tiles must not straddle sequences"
    tps = seq // tm
    return pl.pallas_call(
        _outproj_kernel,
        out_shape=jax.ShapeDtypeStruct((m, d), F32),
        grid=(d // tn, m // tm),
        in_specs=[
            pl.BlockSpec((tm, kdim), lambda n, i: (i, 0)),
            pl.BlockSpec((kdim, tn), lambda n, i: (0, n)),
            pl.BlockSpec((tm, tn), lambda n, i: (i, n)),
            pl.BlockSpec((None, 1, tn), lambda n, i: (i // tps, 0, n)),
        ],
        out_specs=pl.BlockSpec((tm, tn), lambda n, i: (i, n)),
        compiler_params=_cparams("parallel", "parallel"),
        name="outproj",
    )(u, w_out, xf, gate.reshape(b, 1, d))


def _mlstm_qk_kernel(h_ref, w_ref, cw_ref, sc_ref, o_ref, ext_ref, *, tiles_per_seq, conv_k):
    i = pl.program_id(1)
    sub = ROW_SUB
    tiles = _row_tiles(h_ref)

    @pl.when(i % tiles_per_seq == 0)
    def _():
        ext_ref[0, 0:8, :] = jnp.zeros((8, ext_ref.shape[2]), F32)

    def issue(r):
        ext_ref[r % 2, 8:sub + 8, :] = _mm(h_ref[tiles[r], :], w_ref[...], False)

    def finish(r):
        cur = r % 2
        y = cw_ref[conv_k - 1:conv_k, :] * ext_ref[cur, 8:sub + 8, :]
        for back in range(1, conv_k):
            y = y + cw_ref[conv_k - 1 - back:conv_k - back, :] * ext_ref[cur, 8 - back:8 - back + sub, :]
        o_ref[tiles[r], :] = (_silu(y) * sc_ref[...]).astype(BF16)
        ext_ref[1 - cur, 0:8, :] = ext_ref[cur, sub:sub + 8, :]

    _staged(tiles, issue, finish)


def _mlstm_qk_proj(h, w_in, conv_w, qk_width, dk, seq, tm=2048, tn=1024):
    m, kdim = h.shape
    conv_k = conv_w.shape[0]
    assert seq % tm == 0, "row tiles must not straddle sequences"
    tps = seq // tm
    sc = jnp.concatenate([jnp.ones((qk_width,), F32),
                          jnp.full((qk_width,), float(dk) ** -0.5, F32)]).reshape(1, 2 * qk_width)
    return pl.pallas_call(
        functools.partial(_mlstm_qk_kernel, tiles_per_seq=tps, conv_k=conv_k),
        out_shape=jax.ShapeDtypeStruct((m, 2 * qk_width), BF16),
        grid=(2 * qk_width // tn, m // tm),
        in_specs=[
            pl.BlockSpec((tm, kdim), lambda n, i: (i, 0)),
            _w_spec(kdim, tn, 0, False),
            pl.BlockSpec((conv_k, tn), lambda n, i: (0, n)),
            pl.BlockSpec((1, tn), lambda n, i: (0, n)),
        ],
        out_specs=pl.BlockSpec((tm, tn), lambda n, i: (i, n)),
        scratch_shapes=[pltpu.VMEM((2, ROW_SUB + 8, tn), F32)],
        compiler_params=_cparams("arbitrary", "arbitrary"),
        name="mlstm_qk_proj",
    )(h, w_in, conv_w, sc)


def _mlstm_gz_kernel(h_ref, wo_ref, wz_ref, nw_ref, o_ref):
    for rows in _row_tiles(h_ref):
        a = h_ref[rows, :]
        og = _mm(a, wo_ref[...], False)
        z = _mm(a, wz_ref[...], False)
        o_ref[rows, :] = (_sigmoid(og) * _silu(z) * nw_ref[...]).astype(BF16)


def _mlstm_gz_proj(h, w_in, norm_w, o_col0, z_col0, width, tm=1024, tn=1024):
    m, kdim = h.shape
    return pl.pallas_call(
        _mlstm_gz_kernel,
        out_shape=jax.ShapeDtypeStruct((m, width), BF16),
        grid=(width // tn, m // tm),
        in_specs=[
            pl.BlockSpec((tm, kdim), lambda n, i: (i, 0)),
            _w_spec(kdim, tn, o_col0 // tn, False),
            _w_spec(kdim, tn, z_col0 // tn, False),
            pl.BlockSpec((1, tn), lambda n, i: (0, n)),
        ],
        out_specs=pl.BlockSpec((tm, tn), lambda n, i: (i, n)),
        compiler_params=_cparams("parallel", "parallel"),
        name="mlstm_gz_proj",
    )(h, w_in, w_in, norm_w.reshape(1, width))


def _gate_proj_kernel(h_ref, w_ref, b_ref, rows_ref, cols_ref, *, heads, chunk):
    g = jnp.dot(h_ref[...], w_ref[...].astype(BF16), preferred_element_type=F32) + b_ref[...]
    tm = g.shape[0]
    g_t = g.T
    i_rows = g_t[0:heads, :]
    f_rows = g_t[heads:2 * heads, :]
    b = jnp.minimum(f_rows, 0.0) - jnp.log1p(jnp.exp(-jnp.abs(f_rows)))
    pos = lax.broadcasted_iota(jnp.int32, b.shape, 1) % chunk
    shift = 1
    while shift < chunk:
        b = b + jnp.where(pos >= shift, pltpu.roll(b, shift, axis=1), 0.0)
        shift *= 2
    rows = jnp.concatenate([i_rows, b], axis=0)
    rows_ref[...] = rows
    cols_ref[...] = jnp.concatenate([rows, jnp.zeros((LANES - 2 * heads, tm), F32)], axis=0).T


def _gate_proj(h, w_gate, gate_b, batch, seq, tm=1024):
    m, kdim = h.shape
    ng = w_gate.shape[1]
    heads = ng // 2
    assert seq % tm == 0, "row tiles must not straddle sequences"
    tps = seq // tm
    w_pad = jnp.zeros((kdim, LANES), F32).at[:, :ng].set(w_gate)
    b_pad = jnp.zeros((1, LANES), F32).at[0, :ng].set(gate_b)
    kern = functools.partial(_gate_proj_kernel, heads=heads, chunk=MLSTM_CHUNK)
    return pl.pallas_call(
        kern,
        out_shape=(jax.ShapeDtypeStruct((batch, ng, seq), F32), jax.ShapeDtypeStruct((m, LANES), F32)),
        grid=(m // tm,),
        in_specs=[
            pl.BlockSpec((tm, kdim), lambda i: (i, 0)),
            pl.BlockSpec((kdim, LANES), lambda i: (0, 0)),
            pl.BlockSpec((1, LANES), lambda i: (0, 0)),
        ],
        out_specs=(pl.BlockSpec((None, ng, tm), lambda i: (i // tps, 0, i % tps)),
                   pl.BlockSpec((tm, LANES), lambda i: (i, 0))),
        compiler_params=_cparams("parallel"),
        name="mlstm_gate_proj",
    )(h, w_pad, b_pad)


def _mlstm_kernel(q_ref, k_ref, v_ref, gz_ref, rows_ref, cols_ref, u_ref, c_ref, m_ref, *,
                  chunk, heads, hp, dk, dv):
    g = pl.program_id(1)
    c = pl.program_id(2)
    L = chunk
    nt = (((1,), (1,)), ((), ()))

    @pl.when(c == 0)
    def _():
        c_ref[...] = jnp.zeros(c_ref.shape, F32)
        m_ref[...] = jnp.zeros(m_ref.shape, F32)

    t_id = lax.broadcasted_iota(jnp.int32, (L, L), 0)
    s_id = lax.broadcasted_iota(jnp.int32, (L, L), 1)
    causal = s_id <= t_id
    lane = lax.broadcasted_iota(jnp.int32, (L, LANES), 1)
    ones_blk = jnp.where(lane == 0, 1.0, 0.0).astype(BF16)
    ones_cols = jnp.ones((dv, LANES), BF16)
    cols = cols_ref[...]

    def col_of(idx):
        return jnp.sum(jnp.where(lane == idx, cols, 0.0), axis=1, keepdims=True)

    gate = []
    for hh in range(hp):
        h = g * hp + hh
        i_row = rows_ref[pl.ds(h, 1), :]
        b_row = rows_ref[pl.ds(heads + h, 1), :]
        i_col = col_of(h)
        b_col = col_of(heads + h)
        m_prev = m_ref[hh]
        log_d = jnp.where(causal, b_col - b_row + i_row, -jnp.inf)
        log_inter = b_col + m_prev
        m_t = jnp.maximum(log_inter, jnp.max(log_d, axis=1, keepdims=True))
        d_mat = jnp.exp(log_d - m_t)
        w_inter = jnp.exp(log_inter - m_t)
        b_last = b_row[:, L - 1:L]
        m_new = jnp.maximum(b_last + m_prev, jnp.max(b_last - b_row + i_row, axis=1, keepdims=True))
        decay = jnp.exp(b_last + m_prev - m_new)
        ws_col = jnp.exp(b_last - b_col + i_col - m_new)
        m_ref[hh] = m_new
        gate.append((m_t, d_mat, w_inter, decay, ws_col))

    s_qk = []
    for hh in range(hp):
        q = q_ref[:, hh * dk:(hh + 1) * dk]
        k = k_ref[:, hh * dk:(hh + 1) * dk]
        s_qk.append((lax.dot_general(q, k, nt, preferred_element_type=F32) * gate[hh][1]).astype(BF16))

    for hh in range(hp):
        m_t, _, w_inter, _, _ = gate[hh]
        q_scaled = q_ref[:, hh * dk:(hh + 1) * dk] * w_inter.astype(BF16)
        v_aug = jnp.concatenate([v_ref[:, hh * dv:(hh + 1) * dv], ones_blk], axis=1)
        lhs = jnp.concatenate([q_scaled, s_qk[hh]], axis=1)
        rhs = jnp.concatenate([c_ref[hh].astype(BF16), v_aug], axis=0)
        num = jnp.dot(lhs, rhs, preferred_element_type=F32)
        d = jnp.maximum(jnp.abs(num[:, dv:dv + 1]), jnp.exp(-m_t))
        nh = num[:, :dv]
        sq_sum = jnp.dot((nh * nh).astype(BF16), ones_cols, preferred_element_type=F32)[:, 0:1]
        r = lax.rsqrt(sq_sum * (1.0 / dv) + NORM_EPS * (d * d))
        u_ref[:, hh * dv:(hh + 1) * dv] = (nh * r * gz_ref[:, hh * dv:(hh + 1) * dv].astype(F32)).astype(BF16)

    for hh in range(hp):
        _, _, _, decay, ws_col = gate[hh]
        k_scaled = k_ref[:, hh * dk:(hh + 1) * dk] * ws_col.astype(BF16)
        v_aug = jnp.concatenate([v_ref[:, hh * dv:(hh + 1) * dv], ones_blk], axis=1)
        upd = lax.dot_general(k_scaled, v_aug, (((0,), (0,)), ((), ())), preferred_element_type=F32)
        c_ref[hh] = decay * c_ref[hh] + upd


def _mlstm_core(qk, v, gz, rows, cols, batch, seq, heads, dk, dv, hp=None):
    m = qk.shape[0]
    hp = heads if hp is None else hp
    L = MLSTM_CHUNK
    nc = seq // L
    groups = heads // hp
    v_width = heads * dv
    k_off = groups
    kern = functools.partial(_mlstm_kernel, chunk=L, heads=heads, hp=hp, dk=dk, dv=dv)
    return pl.pallas_call(
        kern,
        out_shape=jax.ShapeDtypeStruct((m, v_width), BF16),
        grid=(batch, groups, nc),
        in_specs=[
            pl.BlockSpec((L, hp * dk), lambda b, g, c: (b * nc + c, g)),
            pl.BlockSpec((L, hp * dk), lambda b, g, c: (b * nc + c, k_off + g)),
            pl.BlockSpec((L, hp * dv), lambda b, g, c: (b * nc + c, g)),
            pl.BlockSpec((L, hp * dv), lambda b, g, c: (b * nc + c, g)),
            pl.BlockSpec((None, 2 * heads, L), lambda b, g, c: (b, 0, c)),
            pl.BlockSpec((L, LANES), lambda b, g, c: (b * nc + c, 0)),
        ],
        out_specs=pl.BlockSpec((L, hp * dv), lambda b, g, c: (b * nc + c, g)),
        scratch_shapes=[pltpu.VMEM((hp, dk, dv + LANES), F32), pltpu.VMEM((hp, 1, 1), F32)],
        compiler_params=_cparams("parallel", "parallel", "arbitrary"),
        name="mlstm_core",
    )(qk, qk, v, gz, rows, cols)


def _moba_layer(xf, h, gate, w_in, q_norm, k_norm, w_out, batch, seq):
    hd = q_norm.shape[0]
    heads = w_in.shape[1] // 4 // hd
    width = heads * hd
    w_in = w_in.astype(BF16)
    qk = _attn_qk_proj(h, w_in, q_norm, k_norm, seq)
    v = _proj_act(h, w_in, 2 * width, width, None, "attn_v_proj")
    z_act = _proj_act(h, w_in, 3 * width, width, _silu, "attn_z_proj")
    u = _moba_attention(qk, v, z_act, batch, seq, heads, hd)
    return _outproj(u, w_out.astype(BF16), xf, gate, seq)


def _mlstm_layer(xf, h, gate, w_in, gate_b, conv_w, norm_w, w_out, batch, seq):
    heads = gate_b.shape[0] // 2
    v_width = w_out.shape[0]
    dv = v_width // heads
    qk_width = conv_w.shape[1] // 2
    dk = qk_width // heads
    n_main = 2 * qk_width + 3 * v_width
    w_main = w_in[:, :n_main].astype(BF16)
    qk = _mlstm_qk_proj(h, w_main, conv_w, qk_width, dk, seq)
    v = _proj_act(h, w_main, 2 * qk_width, v_width, None, "mlstm_v_proj")
    gz = _mlstm_gz_proj(h, w_main, norm_w, 2 * qk_width + v_width, 2 * qk_width + 2 * v_width, v_width)
    rows, cols = _gate_proj(h, w_in[:, n_main:], gate_b, batch, seq)
    u = _mlstm_core(qk, v, gz, rows, cols, batch, seq, heads, dk, dv)
    return _outproj(u, w_out.astype(BF16), xf, gate, seq)


def kernel(x, c, norm_w, ada_w, ada_b, att_w_in, att_q_norm, att_k_norm, att_w_out, mlstm_w_in,
           mlstm_gate_b, mlstm_conv_w, mlstm_norm_w, mlstm_w_out):
    batch, seq, d = x.shape
    depth = norm_w.shape[0]
    mod = _adaln(c, ada_w, ada_b)
    xf = x.reshape(batch * seq, d)
    for layer in range(depth):
        shift, scale, gate = mod[layer, :, :d], mod[layer, :, d:2 * d], mod[layer, :, 2 * d:]
        h = _norm_mod(xf, norm_w[layer], scale, shift, seq)
        j = layer // 2
        if layer % 2 == 0:
            xf = _moba_layer(xf, h, gate, att_w_in[j], att_q_norm[j], att_k_norm[j], att_w_out[j], batch, seq)
        else:
            xf = _mlstm_layer(xf, h, gate, mlstm_w_in[j], mlstm_gate_b[j], mlstm_conv_w[j],
                              mlstm_norm_w[j], mlstm_w_out[j], batch, seq)
    return xf.reshape(batch, seq, d)
```

```python
import functools

import jax
import jax.numpy as jnp
from jax import lax
from jax.experimental import pallas as pl
from jax.experimental.pallas import tpu as pltpu

F32 = jnp.float32
BF16 = jnp.bfloat16
HIGHEST = lax.Precision.HIGHEST

NORM_EPS = 1e-6
MOBA_BLOCK = 256
MOBA_TOPK = 3
ROPE_THETA = 10000.0
MLSTM_CHUNK = 256
MASK_BIAS = -1e30
LANES = 128
LOG2_E = 1.4426950408889634
ROW_SUB = 512

_VMEM_LIMIT = 48 * 1024 * 1024


def _sigmoid(x):
    return 1.0 / (1.0 + jnp.exp(-x))


def _cparams(*sem):
    return pltpu.CompilerParams(dimension_semantics=sem, vmem_limit_bytes=_VMEM_LIMIT)


def _adaln_kernel(c_ref, w_ref, b_ref, o_ref):
    c = c_ref[...]
    cond = c * _sigmoid(c)
    o_ref[...] = jnp.dot(cond, w_ref[...], preferred_element_type=F32, precision=HIGHEST) + b_ref[...]


def _adaln(c, ada_w, ada_b, tn=768):
    depth, d, n3 = ada_w.shape
    b = c.shape[0]
    rows = 8
    c_pad = jnp.zeros((rows, d), F32).at[:b].set(c)
    out = pl.pallas_call(
        _adaln_kernel,
        out_shape=jax.ShapeDtypeStruct((depth, rows, n3), F32),
        grid=(depth, n3 // tn),
        in_specs=[
            pl.BlockSpec((rows, d), lambda l, n: (0, 0)),
            pl.BlockSpec((None, d, tn), lambda l, n: (l, 0, n)),
            pl.BlockSpec((None, 1, tn), lambda l, n: (l, 0, n)),
        ],
        out_specs=pl.BlockSpec((None, rows, tn), lambda l, n: (l, 0, n)),
        compiler_params=_cparams("parallel", "parallel"),
        name="adaln_mod",
    )(c_pad, ada_w, ada_b.reshape(depth, 1, n3))
    return out[:, :b]


def _norm_mod_kernel(x_ref, nw_ref, sc_ref, sh_ref, o_ref):
    x = x_ref[...]
    ms = jnp.mean(x * x, axis=-1, keepdims=True)
    y = x * lax.rsqrt(ms + NORM_EPS) * nw_ref[...]
    o_ref[...] = (y * (1.0 + sc_ref[...]) + sh_ref[...]).astype(BF16)


def _norm_mod(xf, nw, scale, shift, seq, ts=512):
    m, d = xf.shape
    b = m // seq
    tps = seq // ts
    return pl.pallas_call(
        _norm_mod_kernel,
        out_shape=jax.ShapeDtypeStruct((m, d), BF16),
        grid=(m // ts,),
        in_specs=[
            pl.BlockSpec((ts, d), lambda i: (i, 0)),
            pl.BlockSpec((1, d), lambda i: (0, 0)),
            pl.BlockSpec((None, 1, d), lambda i: (i // tps, 0, 0)),
            pl.BlockSpec((None, 1, d), lambda i: (i // tps, 0, 0)),
        ],
        out_specs=pl.BlockSpec((ts, d), lambda i: (i, 0)),
        compiler_params=_cparams("parallel"),
        name="norm_mod",
    )(xf, nw.reshape(1, d), scale.reshape(b, 1, d), shift.reshape(b, 1, d))


def _row_tiles(a_ref):
    return [slice(r, r + ROW_SUB) for r in range(0, a_ref.shape[0], ROW_SUB)]


def _silu(x):
    return x * _sigmoid(x)


def _mm(a, w, w_t):
    dims = (((1,), (1,)), ((), ())) if w_t else (((1,), (0,)), ((), ()))
    return lax.dot_general(a, w, dims, preferred_element_type=F32)


def _w_spec(kdim, tn, off, w_t):
    if w_t:
        return pl.BlockSpec((tn, kdim), lambda n, i: (off + n, 0))
    return pl.BlockSpec((kdim, tn), lambda n, i: (0, off + n))


def _proj_act_kernel(h_ref, w_ref, o_ref, *, act, w_t):
    for rows in _row_tiles(h_ref):
        acc = _mm(h_ref[rows, :], w_ref[...], w_t)
        o_ref[rows, :] = (acc if act is None else act(acc)).astype(BF16)


def _proj_act(h, w, col0, width, act, name, w_t=False, tm=1024, tn=1024):
    m, kdim = h.shape
    return pl.pallas_call(
        functools.partial(_proj_act_kernel, act=act, w_t=w_t),
        out_shape=jax.ShapeDtypeStruct((m, width), BF16),
        grid=(width // tn, m // tm),
        in_specs=[
            pl.BlockSpec((tm, kdim), lambda n, i: (i, 0)),
            _w_spec(kdim, tn, col0 // tn, w_t),
        ],
        out_specs=pl.BlockSpec((tm, tn), lambda n, i: (i, n)),
        compiler_params=_cparams("parallel", "parallel"),
        name=name,
    )(h, w)


def _staged(tiles, issue, finish):
    issue(0)
    for r in range(len(tiles)):
        if r + 1 < len(tiles):
            issue(r + 1)
        finish(r)


def _attn_qk_kernel(h_ref, w_ref, cos_ref, sin_ref, nw_ref, sc_ref, o_ref, acc_ref, *, head_dim):
    tiles = _row_tiles(h_ref)

    def issue(r):
        acc_ref[r % 2] = jnp.dot(h_ref[tiles[r], :], w_ref[...], preferred_element_type=F32)

    def finish(r):
        rows = tiles[r]
        cos = cos_ref[rows, :]
        sin = sin_ref[rows, :]
        for hh in range(acc_ref.shape[2] // head_dim):
            sl = slice(hh * head_dim, (hh + 1) * head_dim)
            xh = acc_ref[r % 2, :, sl]
            ms = jnp.mean(xh * xh, axis=-1, keepdims=True)
            y = xh * lax.rsqrt(ms + NORM_EPS) * nw_ref[:, sl]
            rot = y * cos + pltpu.roll(y, head_dim // 2, axis=1) * sin
            o_ref[rows, sl] = (rot * sc_ref[:, sl]).astype(BF16)

    _staged(tiles, issue, finish)


def _attn_qk_proj(h, w_in, q_norm, k_norm, seq, tm=2048, tn=1024):
    m, kdim = h.shape
    width = w_in.shape[1] // 4
    hd = q_norm.shape[0]
    heads = width // hd
    assert seq % tm == 0, "row tiles must not straddle sequences"
    tps = seq // tm
    half = hd // 2
    inv = ROPE_THETA ** (-jnp.arange(half, dtype=F32) / half)
    ang = jnp.arange(seq, dtype=F32)[:, None] * inv[None, :]
    cos = jnp.cos(ang)
    sin = jnp.sin(ang)
    cos_t = jnp.concatenate([cos, cos], axis=-1)
    sin_t = jnp.concatenate([-sin, sin], axis=-1)
    nw = jnp.concatenate([jnp.tile(q_norm, heads), jnp.tile(k_norm, heads)]).reshape(1, 2 * width)
    q_scale = float(hd) ** -0.5 * LOG2_E
    sc = jnp.concatenate([jnp.full((width,), q_scale, F32), jnp.ones((width,), F32)]).reshape(1, 2 * width)
    return pl.pallas_call(
        functools.partial(_attn_qk_kernel, head_dim=hd),
        out_shape=jax.ShapeDtypeStruct((m, 2 * width), BF16),
        grid=(2 * width // tn, m // tm),
        in_specs=[
            pl.BlockSpec((tm, kdim), lambda n, i: (i, 0)),
            pl.BlockSpec((kdim, tn), lambda n, i: (0, n)),
            pl.BlockSpec((tm, hd), lambda n, i: (i % tps, 0)),
            pl.BlockSpec((tm, hd), lambda n, i: (i % tps, 0)),
            pl.BlockSpec((1, tn), lambda n, i: (0, n)),
            pl.BlockSpec((1, tn), lambda n, i: (0, n)),
        ],
        out_specs=pl.BlockSpec((tm, tn), lambda n, i: (i, n)),
        scratch_shapes=[pltpu.VMEM((2, ROW_SUB, tn), F32)],
        compiler_params=_cparams("parallel", "parallel"),
        name="attn_qk_proj",
    )(h, w_in, cos_t, sin_t, nw, sc)


def _moba_kernel(q_ref, k_ref, v_ref, z_ref, e_ref, u_ref,
                 qaugt_ref, kaug_ref, vt_ref, m_ref, acc_ref, sbuf_ref, cbuf_ref, abuf_ref, pbuf_ref, *,
                 bs, nb, topk, hd, hp, pw, pair_list):
    seq = nb * bs
    nt = (((1,), (1,)), ((), ()))

    for h in range(hp):
        hs = slice(h * hd, (h + 1) * hd)
        kf = k_ref[:, hs].astype(F32)
        kmean = jnp.sum(kf.reshape(nb, bs, hd), axis=1) * (1.0 / bs)
        g_t = lax.dot_general(kmean, q_ref[:, hs].astype(F32), nt, preferred_element_type=F32,
                              precision=HIGHEST)
        jj = lax.broadcasted_iota(jnp.int32, (nb, seq), 0)
        qb = lax.broadcasted_iota(jnp.int32, (nb, seq), 1) // bs
        rank = jnp.zeros((nb, seq), jnp.int32)
        for jp in range(nb):
            row = g_t[jp:jp + 1, :]
            beats = (row > g_t) | ((row == g_t) & (jp < jj))
            rank = rank + jnp.where(beats & (jp < qb), 1, 0)
        allowed = ((jj < qb) & (rank < topk)) | (jj == qb)
        bias_t = jnp.where(allowed, 0.0, MASK_BIAS).astype(BF16)
        qaugt_ref[h, hd:hd + nb, :] = bias_t
        qaugt_ref[h, hd + nb:hd + LANES, :] = jnp.zeros((LANES - nb, seq), BF16)
        kaug_ref[h, :, 0:hd] = k_ref[:, hs]
        kaug_ref[h, :, hd:hd + LANES] = e_ref[...]
        for jb in range(nb):
            rs = slice(jb * bs, (jb + 1) * bs)
            qaugt_ref[h, 0:hd, rs] = q_ref[rs, hs].astype(F32).T.astype(BF16)
            vt_ref[h, 0:hd, rs] = v_ref[rs, hs].astype(F32).T.astype(BF16)
        ones_row = lax.broadcasted_iota(jnp.int32, (8, seq), 0) == 0
        vt_ref[h, hd:hd + 8, :] = jnp.where(ones_row, 1.0, 0.0).astype(BF16)
    m_ref[...] = jnp.full(m_ref.shape, -jnp.inf, F32)
    acc_ref[...] = jnp.zeros(acc_ref.shape, F32)

    key_id = lax.broadcasted_iota(jnp.int32, (bs, bs), 0)
    qry_id = lax.broadcasted_iota(jnp.int32, (bs, bs), 1)
    causal = key_id <= qry_id
    steps = [[(h, i, j) for (i, j) in pair_list[t:t + pw] for h in range(hp)]
             for t in range(0, len(pair_list), pw)]

    def stage_scores(us, slot):
        for n, (h, i, j) in enumerate(us):
            s = jnp.dot(kaug_ref[h, j * bs:(j + 1) * bs, :], qaugt_ref[h, :, i * bs:(i + 1) * bs],
                        preferred_element_type=F32)
            if i == j:
                s = jnp.where(causal, s, -jnp.inf)
            sbuf_ref[slot, n] = s
            cbuf_ref[slot, n] = jnp.max(s, axis=0, keepdims=True)

    def stage_softmax(us, slot):
        for n, (h, i, _) in enumerate(us):
            m_old = m_ref[h, i]
            m_new = jnp.maximum(m_old, cbuf_ref[slot, n])
            abuf_ref[slot, n] = jnp.exp2(m_old - m_new)
            m_ref[h, i] = m_new
            pbuf_ref[slot, n] = jnp.exp2(sbuf_ref[slot, n] - m_new).astype(BF16)

    def stage_pv(us, slot):
        for n, (h, i, j) in enumerate(us):
            pv = jnp.dot(vt_ref[h, :, j * bs:(j + 1) * bs], pbuf_ref[slot, n], preferred_element_type=F32)
            acc_ref[h, i] = abuf_ref[slot, n] * acc_ref[h, i] + pv

    def finalize(h, i):
        rows, hs = slice(i * bs, (i + 1) * bs), slice(h * hd, (h + 1) * hd)
        acc = acc_ref[h, i]
        o = (acc[0:hd, :] / acc[hd:hd + 1, :]).T
        u_ref[rows, hs] = (o * z_ref[rows, hs].astype(F32)).astype(BF16)

    last_step = {(h, i): t for t, us in enumerate(steps) for (h, i, _) in us}
    for t in range(len(steps) + 2):
        if t < len(steps):
            stage_scores(steps[t], t % 2)
        if 1 <= t <= len(steps):
            stage_softmax(steps[t - 1], (t - 1) % 2)
        if t >= 2:
            stage_pv(steps[t - 2], t % 2)
            for (h, i), t_last in last_step.items():
                if t_last == t - 2:
                    finalize(h, i)


def _moba_attention(qk, v, z_act, batch, seq, heads, hd, hp=1):
    m = qk.shape[0]
    bs = MOBA_BLOCK
    nb = seq // bs
    width = heads * hd
    groups = heads // hp
    e_tab = (jnp.arange(seq)[:, None] // bs == jnp.arange(LANES)[None, :]).astype(BF16)
    pair_list = [(i, i) for i in range(nb)] + [(i, j) for j in range(nb) for i in range(j + 1, nb)]
    pw = 16
    kern = functools.partial(_moba_kernel, bs=bs, nb=nb, topk=min(MOBA_TOPK, nb), hd=hd, hp=hp, pw=pw,
                             pair_list=tuple(pair_list))
    blk = (seq, hp * hd)
    return pl.pallas_call(
        kern,
        out_shape=jax.ShapeDtypeStruct((m, width), BF16),
        grid=(batch, groups),
        in_specs=[
            pl.BlockSpec(blk, lambda b, g: (b, g)),
            pl.BlockSpec(blk, lambda b, g: (b, groups + g)),
            pl.BlockSpec(blk, lambda b, g: (b, g)),
            pl.BlockSpec(blk, lambda b, g: (b, g)),
            pl.BlockSpec((seq, LANES), lambda b, g: (0, 0)),
        ],
        out_specs=pl.BlockSpec(blk, lambda b, g: (b, g)),
        scratch_shapes=[
            pltpu.VMEM((hp, hd + LANES, seq), BF16),
            pltpu.VMEM((hp, seq, hd + LANES), BF16),
            pltpu.VMEM((hp, hd + 8, seq), BF16),
            pltpu.VMEM((hp, nb, 1, bs), F32),
            pltpu.VMEM((hp, nb, hd + 8, bs), F32),
            pltpu.VMEM((2, pw * hp, bs, bs), F32),
            pltpu.VMEM((2, pw * hp, 1, bs), F32),
            pltpu.VMEM((2, pw * hp, 1, bs), F32),
            pltpu.VMEM((2, pw * hp, bs, bs), BF16),
        ],
        compiler_params=_cparams("parallel", "parallel"),
        name="moba_attn",
    )(qk, qk, v, z_act, e_tab)


def _outproj_kernel(u_ref, w_ref, x_ref, g_ref, o_ref):
    acc = jnp.dot(u_ref[...], w_ref[...], preferred_element_type=F32)
    o_ref[...] = x_ref[...] + g_ref[...] * acc


def _outproj(u, w_out, xf, gate, seq, tm=512, tn=1024):
    m, kdim = u.shape
    d = w_out.shape[1]
    b = m // seq
    assert seq % tm == 0, "row tiles must not straddle sequences"
    tps = seq // tm
    return pl.pallas_call(
        _outproj_kernel,
        out_shape=jax.ShapeDtypeStruct((m, d), F32),
        grid=(d // tn, m // tm),
        in_specs=[
            pl.BlockSpec((tm, kdim), lambda n, i: (i, 0)),
            pl.BlockSpec((kdim, tn), lambda n, i: (0, n)),
            pl.BlockSpec((tm, tn), lambda n, i: (i, n)),
            pl.BlockSpec((None, 1, tn), lambda n, i: (i // tps, 0, n)),
        ],
        out_specs=pl.BlockSpec((tm, tn), lambda n, i: (i, n)),
        compiler_params=_cparams("parallel", "parallel"),
        name="outproj",
    )(u, w_out, xf, gate.reshape(b, 1, d))


def _mlstm_qk_kernel(h_ref, w_ref, cw_ref, sc_ref, o_ref, ext_ref, *, tiles_per_seq, conv_k):
    i = pl.program_id(1)
    sub = ROW_SUB
    tiles = _row_tiles(h_ref)

    @pl.when(i % tiles_per_seq == 0)
    def _():
        ext_ref[0, 0:8, :] = jnp.zeros((8, ext_ref.shape[2]), F32)

    def issue(r):
        ext_ref[r % 2, 8:sub + 8, :] = _mm(h_ref[tiles[r], :], w_ref[...], False)

    def finish(r):
        cur = r % 2
        y = cw_ref[conv_k - 1:conv_k, :] * ext_ref[cur, 8:sub + 8, :]
        for back in range(1, conv_k):
            y = y + cw_ref[conv_k - 1 - back:conv_k - back, :] * ext_ref[cur, 8 - back:8 - back + sub, :]
        o_ref[tiles[r], :] = (_silu(y) * sc_ref[...]).astype(BF16)
        ext_ref[1 - cur, 0:8, :] = ext_ref[cur, sub:sub + 8, :]

    _staged(tiles, issue, finish)


def _mlstm_qk_proj(h, w_in, conv_w, qk_width, dk, seq, tm=2048, tn=1024):
    m, kdim = h.shape
    conv_k = conv_w.shape[0]
    assert seq % tm == 0, "row tiles must not straddle sequences"
    tps = seq // tm
    sc = jnp.concatenate([jnp.ones((qk_width,), F32),
                          jnp.full((qk_width,), float(dk) ** -0.5, F32)]).reshape(1, 2 * qk_width)
    return pl.pallas_call(
        functools.partial(_mlstm_qk_kernel, tiles_per_seq=tps, conv_k=conv_k),
        out_shape=jax.ShapeDtypeStruct((m, 2 * qk_width), BF16),
        grid=(2 * qk_width // tn, m // tm),
        in_specs=[
            pl.BlockSpec((tm, kdim), lambda n, i: (i, 0)),
            _w_spec(kdim, tn, 0, False),
            pl.BlockSpec((conv_k, tn), lambda n, i: (0, n)),
            pl.BlockSpec((1, tn), lambda n, i: (0, n)),
        ],
        out_specs=pl.BlockSpec((tm, tn), lambda n, i: (i, n)),
        scratch_shapes=[pltpu.VMEM((2, ROW_SUB + 8, tn), F32)],
        compiler_params=_cparams("arbitrary", "arbitrary"),
        name="mlstm_qk_proj",
    )(h, w_in, conv_w, sc)


def _mlstm_gz_kernel(h_ref, wo_ref, wz_ref, nw_ref, o_ref):
    for rows in _row_tiles(h_ref):
        a = h_ref[rows, :]
        og = _mm(a, wo_ref[...], False)
        z = _mm(a, wz_ref[...], False)
        o_ref[rows, :] = (_sigmoid(og) * _silu(z) * nw_ref[...]).astype(BF16)


def _mlstm_gz_proj(h, w_in, norm_w, o_col0, z_col0, width, tm=1024, tn=1024):
    m, kdim = h.shape
    return pl.pallas_call(
        _mlstm_gz_kernel,
        out_shape=jax.ShapeDtypeStruct((m, width), BF16),
        grid=(width // tn, m // tm),
        in_specs=[
            pl.BlockSpec((tm, kdim), lambda n, i: (i, 0)),
            _w_spec(kdim, tn, o_col0 // tn, False),
            _w_spec(kdim, tn, z_col0 // tn, False),
            pl.BlockSpec((1, tn), lambda n, i: (0, n)),
        ],
        out_specs=pl.BlockSpec((tm, tn), lambda n, i: (i, n)),
        compiler_params=_cparams("parallel", "parallel"),
        name="mlstm_gz_proj",
    )(h, w_in, w_in, norm_w.reshape(1, width))


def _gate_proj_kernel(h_ref, w_ref, b_ref, rows_ref, cols_ref, *, heads, chunk):
    g = jnp.dot(h_ref[...], w_ref[...].astype(BF16), preferred_element_type=F32) + b_ref[...]
    tm = g.shape[0]
    g_t = g.T
    i_rows = g_t[0:heads, :]
    f_rows = g_t[heads:2 * heads, :]
    b = jnp.minimum(f_rows, 0.0) - jnp.log1p(jnp.exp(-jnp.abs(f_rows)))
    pos = lax.broadcasted_iota(jnp.int32, b.shape, 1) % chunk
    shift = 1
    while shift < chunk:
        b = b + jnp.where(pos >= shift, pltpu.roll(b, shift, axis=1), 0.0)
        shift *= 2
    rows = jnp.concatenate([i_rows, b], axis=0)
    rows_ref[...] = rows
    cols_ref[...] = jnp.concatenate([rows, jnp.zeros((LANES - 2 * heads, tm), F32)], axis=0).T


def _gate_proj(h, w_gate, gate_b, batch, seq, tm=1024):
    m, kdim = h.shape
    ng = w_gate.shape[1]
    heads = ng // 2
    assert seq % tm == 0, "row tiles must not straddle sequences"
    tps = seq // tm
    w_pad = jnp.zeros((kdim, LANES), F32).at[:, :ng].set(w_gate)
    b_pad = jnp.zeros((1, LANES), F32).at[0, :ng].set(gate_b)
    kern = functools.partial(_gate_proj_kernel, heads=heads, chunk=MLSTM_CHUNK)
    return pl.pallas_call(
        kern,
        out_shape=(jax.ShapeDtypeStruct((batch, ng, seq), F32), jax.ShapeDtypeStruct((m, LANES), F32)),
        grid=(m // tm,),
        in_specs=[
            pl.BlockSpec((tm, kdim), lambda i: (i, 0)),
            pl.BlockSpec((kdim, LANES), lambda i: (0, 0)),
            pl.BlockSpec((1, LANES), lambda i: (0, 0)),
        ],
        out_specs=(pl.BlockSpec((None, ng, tm), lambda i: (i // tps, 0, i % tps)),
                   pl.BlockSpec((tm, LANES), lambda i: (i, 0))),
        compiler_params=_cparams("parallel"),
        name="mlstm_gate_proj",
    )(h, w_pad, b_pad)


def _mlstm_kernel(q_ref, k_ref, v_ref, gz_ref, rows_ref, cols_ref, u_ref, c_ref, m_ref, *,
                  chunk, heads, hp, dk, dv):
    g = pl.program_id(1)
    c = pl.program_id(2)
    L = chunk
    nt = (((1,), (1,)), ((), ()))

    @pl.when(c == 0)
    def _():
        c_ref[...] = jnp.zeros(c_ref.shape, F32)
        m_ref[...] = jnp.zeros(m_ref.shape, F32)

    t_id = lax.broadcasted_iota(jnp.int32, (L, L), 0)
    s_id = lax.broadcasted_iota(jnp.int32, (L, L), 1)
    causal = s_id <= t_id
    lane = lax.broadcasted_iota(jnp.int32, (L, LANES), 1)
    ones_blk = jnp.where(lane == 0, 1.0, 0.0).astype(BF16)
    ones_cols = jnp.ones((dv, LANES), BF16)
    cols = cols_ref[...]

    def col_of(idx):
        return jnp.sum(jnp.where(lane == idx, cols, 0.0), axis=1, keepdims=True)

    gate = []
    for hh in range(hp):
        h = g * hp + hh
        i_row = rows_ref[pl.ds(h, 1), :]
        b_row = rows_ref[pl.ds(heads + h, 1), :]
        i_col = col_of(h)
        b_col = col_of(heads + h)
        m_prev = m_ref[hh]
        log_d = jnp.where(causal, b_col - b_row + i_row, -jnp.inf)
        log_inter = b_col + m_prev
        m_t = jnp.maximum(log_inter, jnp.max(log_d, axis=1, keepdims=True))
        d_mat = jnp.exp(log_d - m_t)
        w_inter = jnp.exp(log_inter - m_t)
        b_last = b_row[:, L - 1:L]
        m_new = jnp.maximum(b_last + m_prev, jnp.max(b_last - b_row + i_row, axis=1, keepdims=True))
        decay = jnp.exp(b_last + m_prev - m_new)
        ws_col = jnp.exp(b_last - b_col + i_col - m_new)
        m_ref[hh] = m_new
        gate.append((m_t, d_mat, w_inter, decay, ws_col))

    s_qk = []
    for hh in range(hp):
        q = q_ref[:, hh * dk:(hh + 1) * dk]
        k = k_ref[:, hh * dk:(hh + 1) * dk]
        s_qk.append((lax.dot_general(q, k, nt, preferred_element_type=F32) * gate[hh][1]).astype(BF16))

    for hh in range(hp):
        m_t, _, w_inter, _, _ = gate[hh]
        q_scaled = q_ref[:, hh * dk:(hh + 1) * dk] * w_inter.astype(BF16)
        v_aug = jnp.concatenate([v_ref[:, hh * dv:(hh + 1) * dv], ones_blk], axis=1)
        lhs = jnp.concatenate([q_scaled, s_qk[hh]], axis=1)
        rhs = jnp.concatenate([c_ref[hh].astype(BF16), v_aug], axis=0)
        num = jnp.dot(lhs, rhs, preferred_element_type=F32)
        d = jnp.maximum(jnp.abs(num[:, dv:dv + 1]), jnp.exp(-m_t))
        nh = num[:, :dv]
        sq_sum = jnp.dot((nh * nh).astype(BF16), ones_cols, preferred_element_type=F32)[:, 0:1]
        r = lax.rsqrt(sq_sum * (1.0 / dv) + NORM_EPS * (d * d))
        u_ref[:, hh * dv:(hh + 1) * dv] = (nh * r * gz_ref[:, hh * dv:(hh + 1) * dv].astype(F32)).astype(BF16)

    for hh in range(hp):
        _, _, _, decay, ws_col = gate[hh]
        k_scaled = k_ref[:, hh * dk:(hh + 1) * dk] * ws_col.astype(BF16)
        v_aug = jnp.concatenate([v_ref[:, hh * dv:(hh + 1) * dv], ones_blk], axis=1)
        upd = lax.dot_general(k_scaled, v_aug, (((0,), (0,)), ((), ())), preferred_element_type=F32)
        c_ref[hh] = decay * c_ref[hh] + upd


def _mlstm_core(qk, v, gz, rows, cols, batch, seq, heads, dk, dv, hp=None):
    m = qk.shape[0]
    hp = heads if hp is None else hp
    L = MLSTM_CHUNK
    nc = seq // L
    groups = heads // hp
    v_width = heads * dv
    k_off = groups
    kern = functools.partial(_mlstm_kernel, chunk=L, heads=heads, hp=hp, dk=dk, dv=dv)
    return pl.pallas_call(
        kern,
        out_shape=jax.ShapeDtypeStruct((m, v_width), BF16),
        grid=(batch, groups, nc),
        in_specs=[
            pl.BlockSpec((L, hp * dk), lambda b, g, c: (b * nc + c, g)),
            pl.BlockSpec((L, hp * dk), lambda b, g, c: (b * nc + c, k_off + g)),
            pl.BlockSpec((L, hp * dv), lambda b, g, c: (b * nc + c, g)),
            pl.BlockSpec((L, hp * dv), lambda b, g, c: (b * nc + c, g)),
            pl.BlockSpec((None, 2 * heads, L), lambda b, g, c: (b, 0, c)),
            pl.BlockSpec((L, LANES), lambda b, g, c: (b * nc + c, 0)),
        ],
        out_specs=pl.BlockSpec((L, hp * dv), lambda b, g, c: (b * nc + c, g)),
        scratch_shapes=[pltpu.VMEM((hp, dk, dv + LANES), F32), pltpu.VMEM((hp, 1, 1), F32)],
        compiler_params=_cparams("parallel", "parallel", "arbitrary"),
        name="mlstm_core",
    )(qk, qk, v, gz, rows, cols)


def _moba_layer(xf, h, gate, w_in, q_norm, k_norm, w_out, batch, seq):
    hd = q_norm.shape[0]
    heads = w_in.shape[1] // 4 // hd
    width = heads * hd
    w_in = w_in.astype(BF16)
    qk = _attn_qk_proj(h, w_in, q_norm, k_norm, seq)
    v = _proj_act(h, w_in, 2 * width, width, None, "attn_v_proj")
    z_act = _proj_act(h, w_in, 3 * width, width, _silu, "attn_z_proj")
    u = _moba_attention(qk, v, z_act, batch, seq, heads, hd)
    return _outproj(u, w_out.astype(BF16), xf, gate, seq)


def _mlstm_layer(xf, h, gate, w_in, gate_b, conv_w, norm_w, w_out, batch, seq):
    heads = gate_b.shape[0] // 2
    v_width = w_out.shape[0]
    dv = v_width // heads
    qk_width = conv_w.shape[1] // 2
    dk = qk_width // heads
    n_main = 2 * qk_width + 3 * v_width
    w_main = w_in[:, :n_main].astype(BF16)
    qk = _mlstm_qk_proj(h, w_main, conv_w, qk_width, dk, seq)
    v = _proj_act(h, w_main, 2 * qk_width, v_width, None, "mlstm_v_proj")
    gz = _mlstm_gz_proj(h, w_main, norm_w, 2 * qk_width + v_width, 2 * qk_width + 2 * v_width, v_width)
    rows, cols = _gate_proj(h, w_in[:, n_main:], gate_b, batch, seq)
    u = _mlstm_core(qk, v, gz, rows, cols, batch, seq, heads, dk, dv)
    return _outproj(u, w_out.astype(BF16), xf, gate, seq)


def kernel(x, c, norm_w, ada_w, ada_b, att_w_in, att_q_norm, att_k_norm, att_w_out, mlstm_w_in,
           mlstm_gate_b, mlstm_conv_w, mlstm_norm_w, mlstm_w_out):
    batch, seq, d = x.shape
    depth = norm_w.shape[0]
    mod = _adaln(c, ada_w, ada_b)
    xf = x.reshape(batch * seq, d)
    for layer in range(depth):
        shift, scale, gate = mod[layer, :, :d], mod[layer, :, d:2 * d], mod[layer, :, 2 * d:]
        h = _norm_mod(xf, norm_w[layer], scale, shift, seq)
        j = layer // 2
        if layer % 2 == 0:
            xf = _moba_layer(xf, h, gate, att_w_in[j], att_q_norm[j], att_k_norm[j], att_w_out[j], batch, seq)
        else:
            xf = _mlstm_layer(xf, h, gate, mlstm_w_in[j], mlstm_gate_b[j], mlstm_conv_w[j],
                              mlstm_norm_w[j], mlstm_w_out[j], batch, seq)
    return xf.reshape(batch, seq, d)
```

```python
import functools

import jax
import jax.numpy as jnp
from jax import lax
from jax.experimental import pallas as pl
from jax.experimental.pallas import tpu as pltpu

F32 = jnp.float32
BF16 = jnp.bfloat16
HIGHEST = lax.Precision.HIGHEST

NORM_EPS = 1e-6
MOBA_BLOCK = 256
MOBA_TOPK = 3
ROPE_THETA = 10000.0
MLSTM_CHUNK = 256
MASK_BIAS = -1e30
LANES = 128
LOG2_E = 1.4426950408889634
ROW_SUB = 512

_VMEM_LIMIT = 48 * 1024 * 1024


def _sigmoid(x):
    return 1.0 / (1.0 + jnp.exp(-x))


def _cparams(*sem):
    return pltpu.CompilerParams(dimension_semantics=sem, vmem_limit_bytes=_VMEM_LIMIT)


def _adaln_kernel(c_ref, w_ref, b_ref, o_ref):
    c = c_ref[...]
    cond = c * _sigmoid(c)
    o_ref[...] = jnp.dot(cond, w_ref[...], preferred_element_type=F32, precision=HIGHEST) + b_ref[...]


def _adaln(c, ada_w, ada_b, tn=1536):
    depth, d, n3 = ada_w.shape
    b = c.shape[0]
    rows = 8
    c_pad = jnp.zeros((rows, d), F32).at[:b].set(c)
    out = pl.pallas_call(
        _adaln_kernel,
        out_shape=jax.ShapeDtypeStruct((depth, rows, n3), F32),
        grid=(depth, n3 // tn),
        in_specs=[
            pl.BlockSpec((rows, d), lambda l, n: (0, 0)),
            pl.BlockSpec((None, d, tn), lambda l, n: (l, 0, n)),
            pl.BlockSpec((None, 1, tn), lambda l, n: (l, 0, n)),
        ],
        out_specs=pl.BlockSpec((None, rows, tn), lambda l, n: (l, 0, n)),
        compiler_params=_cparams("parallel", "parallel"),
        name="adaln_mod",
    )(c_pad, ada_w, ada_b.reshape(depth, 1, n3))
    return out[:, :b]


def _norm_mod_kernel(x_ref, nw_ref, sc_ref, sh_ref, o_ref):
    x = x_ref[...]
    ms = jnp.mean(x * x, axis=-1, keepdims=True)
    y = x * lax.rsqrt(ms + NORM_EPS) * nw_ref[...]
    o_ref[...] = (y * (1.0 + sc_ref[...]) + sh_ref[...]).astype(BF16)


def _norm_mod(xf, nw, scale, shift, seq, ts=512):
    m, d = xf.shape
    b = m // seq
    tps = seq // ts
    return pl.pallas_call(
        _norm_mod_kernel,
        out_shape=jax.ShapeDtypeStruct((m, d), BF16),
        grid=(m // ts,),
        in_specs=[
            pl.BlockSpec((ts, d), lambda i: (i, 0)),
            pl.BlockSpec((1, d), lambda i: (0, 0)),
            pl.BlockSpec((None, 1, d), lambda i: (i // tps, 0, 0)),
            pl.BlockSpec((None, 1, d), lambda i: (i // tps, 0, 0)),
        ],
        out_specs=pl.BlockSpec((ts, d), lambda i: (i, 0)),
        compiler_params=_cparams("parallel"),
        name="norm_mod",
    )(xf, nw.reshape(1, d), scale.reshape(b, 1, d), shift.reshape(b, 1, d))


def _row_tiles(a_ref):
    return [slice(r, r + ROW_SUB) for r in range(0, a_ref.shape[0], ROW_SUB)]


def _silu(x):
    return x * _sigmoid(x)


def _mm(a, w, w_t):
    dims = (((1,), (1,)), ((), ())) if w_t else (((1,), (0,)), ((), ()))
    return lax.dot_general(a, w, dims, preferred_element_type=F32)


def _w_spec(kdim, tn, off, w_t):
    if w_t:
        return pl.BlockSpec((tn, kdim), lambda n, i: (off + n, 0))
    return pl.BlockSpec((kdim, tn), lambda n, i: (0, off + n))


def _proj_act_kernel(h_ref, w_ref, o_ref, *, act, w_t):
    for rows in _row_tiles(h_ref):
        acc = _mm(h_ref[rows, :], w_ref[...], w_t)
        o_ref[rows, :] = (acc if act is None else act(acc)).astype(BF16)


def _proj_act(h, w, col0, width, act, name, w_t=False, tm=1024, tn=1024):
    m, kdim = h.shape
    return pl.pallas_call(
        functools.partial(_proj_act_kernel, act=act, w_t=w_t),
        out_shape=jax.ShapeDtypeStruct((m, width), BF16),
        grid=(width // tn, m // tm),
        in_specs=[
            pl.BlockSpec((tm, kdim), lambda n, i: (i, 0)),
            _w_spec(kdim, tn, col0 // tn, w_t),
        ],
        out_specs=pl.BlockSpec((tm, tn), lambda n, i: (i, n)),
        compiler_params=_cparams("parallel", "parallel"),
        name=name,
    )(h, w)


def _staged(tiles, issue, finish):
    issue(0)
    for r in range(len(tiles)):
        if r + 1 < len(tiles):
            issue(r + 1)
        finish(r)


def _attn_qk_kernel(h_ref, w_ref, cos_ref, sin_ref, nw_ref, o_ref, acc_ref, *, head_dim):
    tiles = _row_tiles(h_ref)

    def issue(r):
        acc_ref[r % 2] = jnp.dot(h_ref[tiles[r], :], w_ref[...], preferred_element_type=F32)

    def finish(r):
        rows = tiles[r]
        cos = cos_ref[rows, :]
        sin = sin_ref[rows, :]
        for hh in range(acc_ref.shape[2] // head_dim):
            sl = slice(hh * head_dim, (hh + 1) * head_dim)
            xh = acc_ref[r % 2, :, sl]
            ms = jnp.mean(xh * xh, axis=-1, keepdims=True)
            y = xh * lax.rsqrt(ms + NORM_EPS) * nw_ref[:, sl]
            o_ref[rows, sl] = (y * cos + pltpu.roll(y, head_dim // 2, axis=1) * sin).astype(BF16)

    _staged(tiles, issue, finish)


def _attn_qk_proj(h, w_in, q_norm, k_norm, seq, tm=2048, tn=1024):
    m, kdim = h.shape
    width = w_in.shape[1] // 4
    hd = q_norm.shape[0]
    heads = width // hd
    assert seq % tm == 0, "row tiles must not straddle sequences"
    tps = seq // tm
    half = hd // 2
    inv = ROPE_THETA ** (-jnp.arange(half, dtype=F32) / half)
    ang = jnp.arange(seq, dtype=F32)[:, None] * inv[None, :]
    cos = jnp.cos(ang)
    sin = jnp.sin(ang)
    cos_t = jnp.concatenate([cos, cos], axis=-1)
    sin_t = jnp.concatenate([-sin, sin], axis=-1)
    q_scale = float(hd) ** -0.5 * LOG2_E
    nw = jnp.concatenate([jnp.tile(q_norm * q_scale, heads), jnp.tile(k_norm, heads)]).reshape(1, 2 * width)
    return pl.pallas_call(
        functools.partial(_attn_qk_kernel, head_dim=hd),
        out_shape=jax.ShapeDtypeStruct((m, 2 * width), BF16),
        grid=(2 * width // tn, m // tm),
        in_specs=[
            pl.BlockSpec((tm, kdim), lambda n, i: (i, 0)),
            pl.BlockSpec((kdim, tn), lambda n, i: (0, n)),
            pl.BlockSpec((tm, hd), lambda n, i: (i % tps, 0)),
            pl.BlockSpec((tm, hd), lambda n, i: (i % tps, 0)),
            pl.BlockSpec((1, tn), lambda n, i: (0, n)),
        ],
        out_specs=pl.BlockSpec((tm, tn), lambda n, i: (i, n)),
        scratch_shapes=[pltpu.VMEM((2, ROW_SUB, tn), F32)],
        compiler_params=_cparams("parallel", "parallel"),
        name="attn_qk_proj",
    )(h, w_in, cos_t, sin_t, nw)


def _moba_kernel(q_ref, k_ref, v_ref, z_ref, e_ref, u_ref,
                 qaugt_ref, kaug_ref, vt_ref, m_ref, acc_ref, sbuf_ref, cbuf_ref, abuf_ref, pbuf_ref, *,
                 bs, nb, topk, hd, hp, pw, pair_list):
    seq = nb * bs
    nt = (((1,), (1,)), ((), ()))

    for h in range(hp):
        hs = slice(h * hd, (h + 1) * hd)
        kf = k_ref[:, hs].astype(F32)
        kmean = jnp.sum(kf.reshape(nb, bs, hd), axis=1) * (1.0 / bs)
        g_t = lax.dot_general(kmean, q_ref[:, hs].astype(F32), nt, preferred_element_type=F32,
                              precision=HIGHEST)
        jj = lax.broadcasted_iota(jnp.int32, (nb, seq), 0)
        qb = lax.broadcasted_iota(jnp.int32, (nb, seq), 1) // bs
        rank = jnp.zeros((nb, seq), jnp.int32)
        for jp in range(nb):
            row = g_t[jp:jp + 1, :]
            beats = (row > g_t) | ((row == g_t) & (jp < jj))
            rank = rank + jnp.where(beats & (jp < qb), 1, 0)
        allowed = ((jj < qb) & (rank < topk)) | (jj == qb)
        bias_t = jnp.where(allowed, 0.0, MASK_BIAS).astype(BF16)
        qaugt_ref[h, hd:hd + nb, :] = bias_t
        qaugt_ref[h, hd + nb:hd + LANES, :] = jnp.zeros((LANES - nb, seq), BF16)
        kaug_ref[h, :, 0:hd] = k_ref[:, hs]
        kaug_ref[h, :, hd:hd + LANES] = e_ref[...]
        for jb in range(nb):
            rs = slice(jb * bs, (jb + 1) * bs)
            qaugt_ref[h, 0:hd, rs] = q_ref[rs, hs].astype(F32).T.astype(BF16)
            vt_ref[h, 0:hd, rs] = v_ref[rs, hs].astype(F32).T.astype(BF16)
        ones_row = lax.broadcasted_iota(jnp.int32, (8, seq), 0) == 0
        vt_ref[h, hd:hd + 8, :] = jnp.where(ones_row, 1.0, 0.0).astype(BF16)
    m_ref[...] = jnp.full(m_ref.shape, -jnp.inf, F32)
    acc_ref[...] = jnp.zeros(acc_ref.shape, F32)

    key_id = lax.broadcasted_iota(jnp.int32, (bs, bs), 0)
    qry_id = lax.broadcasted_iota(jnp.int32, (bs, bs), 1)
    causal = key_id <= qry_id
    steps = [[(h, i, j) for (i, j) in pair_list[t:t + pw] for h in range(hp)]
             for t in range(0, len(pair_list), pw)]

    def stage_scores(us, slot):
        for n, (h, i, j) in enumerate(us):
            s = jnp.dot(kaug_ref[h, j * bs:(j + 1) * bs, :], qaugt_ref[h, :, i * bs:(i + 1) * bs],
                        preferred_element_type=F32)
            if i == j:
                s = jnp.where(causal, s, -jnp.inf)
            sbuf_ref[slot, n] = s
            cbuf_ref[slot, n] = jnp.max(s, axis=0, keepdims=True)

    def stage_softmax(us, slot):
        for n, (h, i, _) in enumerate(us):
            m_old = m_ref[h, i]
            m_new = jnp.maximum(m_old, cbuf_ref[slot, n])
            abuf_ref[slot, n] = jnp.exp2(m_old - m_new)
            m_ref[h, i] = m_new
            pbuf_ref[slot, n] = jnp.exp2(sbuf_ref[slot, n] - m_new).astype(BF16)

    def stage_pv(us, slot):
        for n, (h, i, j) in enumerate(us):
            pv = jnp.dot(vt_ref[h, :, j * bs:(j + 1) * bs], pbuf_ref[slot, n], preferred_element_type=F32)
            acc_ref[h, i] = abuf_ref[slot, n] * acc_ref[h, i] + pv

    def finalize(h, i):
        rows, hs = slice(i * bs, (i + 1) * bs), slice(h * hd, (h + 1) * hd)
        acc = acc_ref[h, i]
        o = (acc[0:hd, :] / acc[hd:hd + 1, :]).T
        u_ref[rows, hs] = (o * z_ref[rows, hs].astype(F32)).astype(BF16)

    last_step = {(h, i): t for t, us in enumerate(steps) for (h, i, _) in us}
    for t in range(len(steps) + 2):
        if t < len(steps):
            stage_scores(steps[t], t % 2)
        if 1 <= t <= len(steps):
            stage_softmax(steps[t - 1], (t - 1) % 2)
        if t >= 2:
            stage_pv(steps[t - 2], t % 2)
            for (h, i), t_last in last_step.items():
                if t_last == t - 2:
                    finalize(h, i)


def _moba_attention(qk, v, z_act, batch, seq, heads, hd, hp=1):
    m = qk.shape[0]
    bs = MOBA_BLOCK
    nb = seq // bs
    width = heads * hd
    groups = heads // hp
    e_tab = (jnp.arange(seq)[:, None] // bs == jnp.arange(LANES)[None, :]).astype(BF16)
    pair_list = [(i, i) for i in range(nb)] + [(i, j) for j in range(nb) for i in range(j + 1, nb)]
    pw = 16
    kern = functools.partial(_moba_kernel, bs=bs, nb=nb, topk=min(MOBA_TOPK, nb), hd=hd, hp=hp, pw=pw,
                             pair_list=tuple(pair_list))
    blk = (seq, hp * hd)
    return pl.pallas_call(
        kern,
        out_shape=jax.ShapeDtypeStruct((m, width), BF16),
        grid=(batch, groups),
        in_specs=[
            pl.BlockSpec(blk, lambda b, g: (b, g)),
            pl.BlockSpec(blk, lambda b, g: (b, groups + g)),
            pl.BlockSpec(blk, lambda b, g: (b, g)),
            pl.BlockSpec(blk, lambda b, g: (b, g)),
            pl.BlockSpec((seq, LANES), lambda b, g: (0, 0)),
        ],
        out_specs=pl.BlockSpec(blk, lambda b, g: (b, g)),
        scratch_shapes=[
            pltpu.VMEM((hp, hd + LANES, seq), BF16),
            pltpu.VMEM((hp, seq, hd + LANES), BF16),
            pltpu.VMEM((hp, hd + 8, seq), BF16),
            pltpu.VMEM((hp, nb, 1, bs), F32),
            pltpu.VMEM((hp, nb, hd + 8, bs), F32),
            pltpu.VMEM((2, pw * hp, bs, bs), F32),
            pltpu.VMEM((2, pw * hp, 1, bs), F32),
            pltpu.VMEM((2, pw * hp, 1, bs), F32),
            pltpu.VMEM((2, pw * hp, bs, bs), BF16),
        ],
        compiler_params=_cparams("parallel", "parallel"),
        name="moba_attn",
    )(qk, qk, v, z_act, e_tab)


def _outproj_kernel(u_ref, w_ref, x_ref, g_ref, o_ref):
    acc = jnp.dot(u_ref[...], w_ref[...], preferred_element_type=F32)
    o_ref[...] = x_ref[...] + g_ref[...] * acc


def _outproj(u, w_out, xf, gate, seq, tm=512, tn=1024):
    m, kdim = u.shape
    d = w_out.shape[1]
    b = m // seq
    assert seq % tm == 0, "row tiles must not straddle sequences"
    tps = seq // tm
    return pl.pallas_call(
        _outproj_kernel,
        out_shape=jax.ShapeDtypeStruct((m, d), F32),
        grid=(d // tn, m // tm),
        in_specs=[
            pl.BlockSpec((tm, kdim), lambda n, i: (i, 0)),
            pl.BlockSpec((kdim, tn), lambda n, i: (0, n)),
            pl.BlockSpec((tm, tn), lambda n, i: (i, n)),
            pl.BlockSpec((None, 1, tn), lambda n, i: (i // tps, 0, n)),
        ],
        out_specs=pl.BlockSpec((tm, tn), lambda n, i: (i, n)),
        compiler_params=_cparams("parallel", "parallel"),
        name="outproj",
    )(u, w_out, xf, gate.reshape(b, 1, d))


def _mlstm_qk_kernel(h_ref, w_ref, cw_ref, sc_ref, o_ref, ext_ref, *, tiles_per_seq, conv_k):
    i = pl.program_id(1)
    sub = ROW_SUB
    tiles = _row_tiles(h_ref)

    @pl.when(i % tiles_per_seq == 0)
    def _():
        ext_ref[0, 0:8, :] = jnp.zeros((8, ext_ref.shape[2]), F32)

    def issue(r):
        ext_ref[r % 2, 8:sub + 8, :] = _mm(h_ref[tiles[r], :], w_ref[...], False)

    def finish(r):
        cur = r % 2
        y = cw_ref[conv_k - 1:conv_k, :] * ext_ref[cur, 8:sub + 8, :]
        for back in range(1, conv_k):
            y = y + cw_ref[conv_k - 1 - back:conv_k - back, :] * ext_ref[cur, 8 - back:8 - back + sub, :]
        o_ref[tiles[r], :] = (_silu(y) * sc_ref[...]).astype(BF16)
        ext_ref[1 - cur, 0:8, :] = ext_ref[cur, sub:sub + 8, :]

    _staged(tiles, issue, finish)


def _mlstm_qk_proj(h, w_in, conv_w, qk_width, dk, seq, tm=2048, tn=1024):
    m, kdim = h.shape
    conv_k = conv_w.shape[0]
    assert seq % tm == 0, "row tiles must not straddle sequences"
    tps = seq // tm
    sc = jnp.concatenate([jnp.ones((qk_width,), F32),
                          jnp.full((qk_width,), float(dk) ** -0.5, F32)]).reshape(1, 2 * qk_width)
    return pl.pallas_call(
        functools.partial(_mlstm_qk_kernel, tiles_per_seq=tps, conv_k=conv_k),
        out_shape=jax.ShapeDtypeStruct((m, 2 * qk_width), BF16),
        grid=(2 * qk_width // tn, m // tm),
        in_specs=[
            pl.BlockSpec((tm, kdim), lambda n, i: (i, 0)),
            _w_spec(kdim, tn, 0, False),
            pl.BlockSpec((conv_k, tn), lambda n, i: (0, n)),
            pl.BlockSpec((1, tn), lambda n, i: (0, n)),
        ],
        out_specs=pl.BlockSpec((tm, tn), lambda n, i: (i, n)),
        scratch_shapes=[pltpu.VMEM((2, ROW_SUB + 8, tn), F32)],
        compiler_params=_cparams("arbitrary", "arbitrary"),
        name="mlstm_qk_proj",
    )(h, w_in, conv_w, sc)


def _mlstm_gz_kernel(h_ref, wo_ref, wz_ref, nw_ref, o_ref):
    for rows in _row_tiles(h_ref):
        a = h_ref[rows, :]
        og = _mm(a, wo_ref[...], False)
        z = _mm(a, wz_ref[...], False)
        o_ref[rows, :] = (_sigmoid(og) * _silu(z) * nw_ref[...]).astype(BF16)


def _mlstm_gz_proj(h, w_in, norm_w, o_col0, z_col0, width, tm=1024, tn=1024):
    m, kdim = h.shape
    return pl.pallas_call(
        _mlstm_gz_kernel,
        out_shape=jax.ShapeDtypeStruct((m, width), BF16),
        grid=(width // tn, m // tm),
        in_specs=[
            pl.BlockSpec((tm, kdim), lambda n, i: (i, 0)),
            _w_spec(kdim, tn, o_col0 // tn, False),
            _w_spec(kdim, tn, z_col0 // tn, False),
            pl.BlockSpec((1, tn), lambda n, i: (0, n)),
        ],
        out_specs=pl.BlockSpec((tm, tn), lambda n, i: (i, n)),
        compiler_params=_cparams("parallel", "parallel"),
        name="mlstm_gz_proj",
    )(h, w_in, w_in, norm_w.reshape(1, width))


def _gate_proj_kernel(h_ref, w_ref, b_ref, rows_ref, cols_ref, *, heads, chunk):
    g = jnp.dot(h_ref[...], w_ref[...].astype(BF16), preferred_element_type=F32) + b_ref[...]
    tm = g.shape[0]
    g_t = g.T
    i_rows = g_t[0:heads, :]
    f_rows = g_t[heads:2 * heads, :]
    b = jnp.minimum(f_rows, 0.0) - jnp.log1p(jnp.exp(-jnp.abs(f_rows)))
    pos = lax.broadcasted_iota(jnp.int32, b.shape, 1) % chunk
    shift = 1
    while shift < chunk:
        b = b + jnp.where(pos >= shift, pltpu.roll(b, shift, axis=1), 0.0)
        shift *= 2
    rows = jnp.concatenate([i_rows, b], axis=0)
    rows_ref[...] = rows
    cols_ref[...] = jnp.concatenate([rows, jnp.zeros((LANES - 2 * heads, tm), F32)], axis=0).T


def _gate_proj(h, w_gate, gate_b, batch, seq, tm=1024):
    m, kdim = h.shape
    ng = w_gate.shape[1]
    heads = ng // 2
    assert seq % tm == 0, "row tiles must not straddle sequences"
    tps = seq // tm
    w_pad = jnp.zeros((kdim, LANES), F32).at[:, :ng].set(w_gate)
    b_pad = jnp.zeros((1, LANES), F32).at[0, :ng].set(gate_b)
    kern = functools.partial(_gate_proj_kernel, heads=heads, chunk=MLSTM_CHUNK)
    return pl.pallas_call(
        kern,
        out_shape=(jax.ShapeDtypeStruct((batch, ng, seq), F32), jax.ShapeDtypeStruct((m, LANES), F32)),
        grid=(m // tm,),
        in_specs=[
            pl.BlockSpec((tm, kdim), lambda i: (i, 0)),
            pl.BlockSpec((kdim, LANES), lambda i: (0, 0)),
            pl.BlockSpec((1, LANES), lambda i: (0, 0)),
        ],
        out_specs=(pl.BlockSpec((None, ng, tm), lambda i: (i // tps, 0, i % tps)),
                   pl.BlockSpec((tm, LANES), lambda i: (i, 0))),
        compiler_params=_cparams("parallel"),
        name="mlstm_gate_proj",
    )(h, w_pad, b_pad)


def _mlstm_kernel(q_ref, k_ref, v_ref, gz_ref, rows_ref, cols_ref, u_ref, c_ref, m_ref, *,
                  chunk, heads, hp, dk, dv):
    g = pl.program_id(1)
    c = pl.program_id(2)
    L = chunk
    nt = (((1,), (1,)), ((), ()))

    @pl.when(c == 0)
    def _():
        c_ref[...] = jnp.zeros(c_ref.shape, F32)
        m_ref[...] = jnp.zeros(m_ref.shape, F32)

    t_id = lax.broadcasted_iota(jnp.int32, (L, L), 0)
    s_id = lax.broadcasted_iota(jnp.int32, (L, L), 1)
    causal = s_id <= t_id
    lane = lax.broadcasted_iota(jnp.int32, (L, LANES), 1)
    ones_blk = jnp.where(lane == 0, 1.0, 0.0).astype(BF16)
    ones_cols = jnp.ones((dv, LANES), BF16)
    cols = cols_ref[...]

    def col_of(idx):
        return jnp.sum(jnp.where(lane == idx, cols, 0.0), axis=1, keepdims=True)

    gate = []
    for hh in range(hp):
        h = g * hp + hh
        i_row = rows_ref[pl.ds(h, 1), :]
        b_row = rows_ref[pl.ds(heads + h, 1), :]
        i_col = col_of(h)
        b_col = col_of(heads + h)
        m_prev = m_ref[hh]
        log_d = jnp.where(causal, b_col - b_row + i_row, -jnp.inf)
        log_inter = b_col + m_prev
        m_t = jnp.maximum(log_inter, jnp.max(log_d, axis=1, keepdims=True))
        d_mat = jnp.exp(log_d - m_t)
        w_inter = jnp.exp(log_inter - m_t)
        b_last = b_row[:, L - 1:L]
        m_new = jnp.maximum(b_last + m_prev, jnp.max(b_last - b_row + i_row, axis=1, keepdims=True))
        decay = jnp.exp(b_last + m_prev - m_new)
        ws_col = jnp.exp(b_last - b_col + i_col - m_new)
        m_ref[hh] = m_new
        gate.append((m_t, d_mat, w_inter, decay, ws_col))

    s_qk = []
    for hh in range(hp):
        q = q_ref[:, hh * dk:(hh + 1) * dk]
        k = k_ref[:, hh * dk:(hh + 1) * dk]
        s_qk.append((lax.dot_general(q, k, nt, preferred_element_type=F32) * gate[hh][1]).astype(BF16))

    for hh in range(hp):
        m_t, _, w_inter, _, _ = gate[hh]
        q_scaled = q_ref[:, hh * dk:(hh + 1) * dk] * w_inter.astype(BF16)
        v_aug = jnp.concatenate([v_ref[:, hh * dv:(hh + 1) * dv], ones_blk], axis=1)
        lhs = jnp.concatenate([q_scaled, s_qk[hh]], axis=1)
        rhs = jnp.concatenate([c_ref[hh].astype(BF16), v_aug], axis=0)
        num = jnp.dot(lhs, rhs, preferred_element_type=F32)
        d = jnp.maximum(jnp.abs(num[:, dv:dv + 1]), jnp.exp(-m_t))
        nh = num[:, :dv]
        sq_sum = jnp.dot((nh * nh).astype(BF16), ones_cols, preferred_element_type=F32)[:, 0:1]
        r = lax.rsqrt(sq_sum * (1.0 / dv) + NORM_EPS * (d * d))
        u_ref[:, hh * dv:(hh + 1) * dv] = (nh * r * gz_ref[:, hh * dv:(hh + 1) * dv].astype(F32)).astype(BF16)

    for hh in range(hp):
        _, _, _, decay, ws_col = gate[hh]
        k_scaled = k_ref[:, hh * dk:(hh + 1) * dk] * ws_col.astype(BF16)
        v_aug = jnp.concatenate([v_ref[:, hh * dv:(hh + 1) * dv], ones_blk], axis=1)
        upd = lax.dot_general(k_scaled, v_aug, (((0,), (0,)), ((), ())), preferred_element_type=F32)
        c_ref[hh] = decay * c_ref[hh] + upd


def _mlstm_core(qk, v, gz, rows, cols, batch, seq, heads, dk, dv, hp=None):
    m = qk.shape[0]
    hp = heads if hp is None else hp
    L = MLSTM_CHUNK
    nc = seq // L
    groups = heads // hp
    v_width = heads * dv
    k_off = groups
    kern = functools.partial(_mlstm_kernel, chunk=L, heads=heads, hp=hp, dk=dk, dv=dv)
    return pl.pallas_call(
        kern,
        out_shape=jax.ShapeDtypeStruct((m, v_width), BF16),
        grid=(batch, groups, nc),
        in_specs=[
            pl.BlockSpec((L, hp * dk), lambda b, g, c: (b * nc + c, g)),
            pl.BlockSpec((L, hp * dk), lambda b, g, c: (b * nc + c, k_off + g)),
            pl.BlockSpec((L, hp * dv), lambda b, g, c: (b * nc + c, g)),
            pl.BlockSpec((L, hp * dv), lambda b, g, c: (b * nc + c, g)),
            pl.BlockSpec((None, 2 * heads, L), lambda b, g, c: (b, 0, c)),
            pl.BlockSpec((L, LANES), lambda b, g, c: (b * nc + c, 0)),
        ],
        out_specs=pl.BlockSpec((L, hp * dv), lambda b, g, c: (b * nc + c, g)),
        scratch_shapes=[pltpu.VMEM((hp, dk, dv + LANES), F32), pltpu.VMEM((hp, 1, 1), F32)],
        compiler_params=_cparams("parallel", "parallel", "arbitrary"),
        name="mlstm_core",
    )(qk, qk, v, gz, rows, cols)


def _moba_layer(xf, h, gate, w_in, q_norm, k_norm, w_out, batch, seq):
    hd = q_norm.shape[0]
    heads = w_in.shape[1] // 4 // hd
    width = heads * hd
    w_in = w_in.astype(BF16)
    qk = _attn_qk_proj(h, w_in, q_norm, k_norm, seq)
    v = _proj_act(h, w_in, 2 * width, width, None, "attn_v_proj")
    z_act = _proj_act(h, w_in, 3 * width, width, _silu, "attn_z_proj")
    u = _moba_attention(qk, v, z_act, batch, seq, heads, hd)
    return _outproj(u, w_out.astype(BF16), xf, gate, seq, tn=w_out.shape[1])


def _mlstm_layer(xf, h, gate, w_in, gate_b, conv_w, norm_w, w_out, batch, seq):
    heads = gate_b.shape[0] // 2
    v_width = w_out.shape[0]
    dv = v_width // heads
    qk_width = conv_w.shape[1] // 2
    dk = qk_width // heads
    n_main = 2 * qk_width + 3 * v_width
    w_main = w_in[:, :n_main].astype(BF16)
    qk = _mlstm_qk_proj(h, w_main, conv_w, qk_width, dk, seq)
    v = _proj_act(h, w_main, 2 * qk_width, v_width, None, "mlstm_v_proj")
    gz = _mlstm_gz_proj(h, w_main, norm_w, 2 * qk_width + v_width, 2 * qk_width + 2 * v_width, v_width)
    rows, cols = _gate_proj(h, w_in[:, n_main:], gate_b, batch, seq)
    u = _mlstm_core(qk, v, gz, rows, cols, batch, seq, heads, dk, dv)
    return _outproj(u, w_out.astype(BF16), xf, gate, seq)


def kernel(x, c, norm_w, ada_w, ada_b, att_w_in, att_q_norm, att_k_norm, att_w_out, mlstm_w_in,
           mlstm_gate_b, mlstm_conv_w, mlstm_norm_w, mlstm_w_out):
    batch, seq, d = x.shape
    depth = norm_w.shape[0]
    mod = _adaln(c, ada_w, ada_b)
    xf = x.reshape(batch * seq, d)
    for layer in range(depth):
        shift, scale, gate = mod[layer, :, :d], mod[layer, :, d:2 * d], mod[layer, :, 2 * d:]
        h = _norm_mod(xf, norm_w[layer], scale, shift, seq)
        j = layer // 2
        if layer % 2 == 0:
            xf = _moba_layer(xf, h, gate, att_w_in[j], att_q_norm[j], att_k_norm[j], att_w_out[j], batch, seq)
        else:
            xf = _mlstm_layer(xf, h, gate, mlstm_w_in[j], mlstm_gate_b[j], mlstm_conv_w[j],
                              mlstm_norm_w[j], mlstm_w_out[j], batch, seq)
    return xf.reshape(batch, seq, d)
```

```python
import functools

import jax
import jax.numpy as jnp
from jax import lax
from jax.experimental import pallas as pl
from jax.experimental.pallas import tpu as pltpu

F32 = jnp.float32
BF16 = jnp.bfloat16
HIGHEST = lax.Precision.HIGHEST

NORM_EPS = 1e-6
MOBA_BLOCK = 256
MOBA_TOPK = 3
ROPE_THETA = 10000.0
MLSTM_CHUNK = 256
MASK_BIAS = -1e30
LANES = 128
LOG2_E = 1.4426950408889634
ROW_SUB = 512

_VMEM_LIMIT = 48 * 1024 * 1024


def _sigmoid(x):
    return 1.0 / (1.0 + jnp.exp(-x))


def _cparams(*sem):
    return pltpu.CompilerParams(dimension_semantics=sem, vmem_limit_bytes=_VMEM_LIMIT)


def _adaln_kernel(c_ref, w_ref, b_ref, o_ref):
    c = c_ref[...]
    cond = c * _sigmoid(c)
    o_ref[...] = jnp.dot(cond, w_ref[...], preferred_element_type=F32, precision=HIGHEST) + b_ref[...]


def _adaln(c, ada_w, ada_b, tn=1536):
    depth, d, n3 = ada_w.shape
    b = c.shape[0]
    rows = 8
    c_pad = jnp.zeros((rows, d), F32).at[:b].set(c)
    out = pl.pallas_call(
        _adaln_kernel,
        out_shape=jax.ShapeDtypeStruct((depth, rows, n3), F32),
        grid=(depth, n3 // tn),
        in_specs=[
            pl.BlockSpec((rows, d), lambda l, n: (0, 0)),
            pl.BlockSpec((None, d, tn), lambda l, n: (l, 0, n)),
            pl.BlockSpec((None, 1, tn), lambda l, n: (l, 0, n)),
        ],
        out_specs=pl.BlockSpec((None, rows, tn), lambda l, n: (l, 0, n)),
        compiler_params=_cparams("parallel", "parallel"),
        name="adaln_mod",
    )(c_pad, ada_w, ada_b.reshape(depth, 1, n3))
    return out[:, :b]


def _norm_mod_kernel(x_ref, nw_ref, sc_ref, sh_ref, o_ref):
    x = x_ref[...]
    ms = jnp.mean(x * x, axis=-1, keepdims=True)
    y = x * lax.rsqrt(ms + NORM_EPS) * nw_ref[...]
    o_ref[...] = (y * (1.0 + sc_ref[...]) + sh_ref[...]).astype(BF16)


def _norm_mod(xf, nw, scale, shift, seq, ts=512):
    m, d = xf.shape
    b = m // seq
    tps = seq // ts
    return pl.pallas_call(
        _norm_mod_kernel,
        out_shape=jax.ShapeDtypeStruct((m, d), BF16),
        grid=(m // ts,),
        in_specs=[
            pl.BlockSpec((ts, d), lambda i: (i, 0)),
            pl.BlockSpec((1, d), lambda i: (0, 0)),
            pl.BlockSpec((None, 1, d), lambda i: (i // tps, 0, 0)),
            pl.BlockSpec((None, 1, d), lambda i: (i // tps, 0, 0)),
        ],
        out_specs=pl.BlockSpec((ts, d), lambda i: (i, 0)),
        compiler_params=_cparams("parallel"),
        name="norm_mod",
    )(xf, nw.reshape(1, d), scale.reshape(b, 1, d), shift.reshape(b, 1, d))


def _row_tiles(a_ref):
    return [slice(r, r + ROW_SUB) for r in range(0, a_ref.shape[0], ROW_SUB)]


def _silu(x):
    return x * _sigmoid(x)


def _mm(a, w, w_t):
    dims = (((1,), (1,)), ((), ())) if w_t else (((1,), (0,)), ((), ()))
    return lax.dot_general(a, w, dims, preferred_element_type=F32)


def _w_spec(kdim, tn, off, w_t):
    if w_t:
        return pl.BlockSpec((tn, kdim), lambda n, i: (off + n, 0))
    return pl.BlockSpec((kdim, tn), lambda n, i: (0, off + n))


def _proj_act_kernel(h_ref, w_ref, o_ref, *, act, w_t):
    for rows in _row_tiles(h_ref):
        acc = _mm(h_ref[rows, :], w_ref[...], w_t)
        o_ref[rows, :] = (acc if act is None else act(acc)).astype(BF16)


def _proj_act(h, w, col0, width, act, name, w_t=False, tm=1024, tn=1024):
    m, kdim = h.shape
    return pl.pallas_call(
        functools.partial(_proj_act_kernel, act=act, w_t=w_t),
        out_shape=jax.ShapeDtypeStruct((m, width), BF16),
        grid=(width // tn, m // tm),
        in_specs=[
            pl.BlockSpec((tm, kdim), lambda n, i: (i, 0)),
            _w_spec(kdim, tn, col0 // tn, w_t),
        ],
        out_specs=pl.BlockSpec((tm, tn), lambda n, i: (i, n)),
        compiler_params=_cparams("parallel", "parallel"),
        name=name,
    )(h, w)


def _staged(tiles, issue, finish):
    issue(0)
    for r in range(len(tiles)):
        if r + 1 < len(tiles):
            issue(r + 1)
        finish(r)


def _attn_qk_kernel(h_ref, w_ref, cos_ref, sin_ref, nw_ref, o_ref, acc_ref, *, head_dim):
    tiles = _row_tiles(h_ref)

    def issue(r):
        acc_ref[r % 2] = jnp.dot(h_ref[tiles[r], :], w_ref[...], preferred_element_type=F32)

    def finish(r):
        rows = tiles[r]
        cos = cos_ref[rows, :]
        sin = sin_ref[rows, :]
        for hh in range(acc_ref.shape[2] // head_dim):
            sl = slice(hh * head_dim, (hh + 1) * head_dim)
            xh = acc_ref[r % 2, :, sl]
            ms = jnp.mean(xh * xh, axis=-1, keepdims=True)
            y = xh * lax.rsqrt(ms + NORM_EPS) * nw_ref[:, sl]
            o_ref[rows, sl] = (y * cos + pltpu.roll(y, head_dim // 2, axis=1) * sin).astype(BF16)

    _staged(tiles, issue, finish)


def _attn_qk_proj(h, w_in, q_norm, k_norm, seq, tm=2048, tn=1024):
    m, kdim = h.shape
    width = w_in.shape[1] // 4
    hd = q_norm.shape[0]
    heads = width // hd
    assert seq % tm == 0, "row tiles must not straddle sequences"
    tps = seq // tm
    half = hd // 2
    inv = ROPE_THETA ** (-jnp.arange(half, dtype=F32) / half)
    ang = jnp.arange(seq, dtype=F32)[:, None] * inv[None, :]
    cos = jnp.cos(ang)
    sin = jnp.sin(ang)
    cos_t = jnp.concatenate([cos, cos], axis=-1)
    sin_t = jnp.concatenate([-sin, sin], axis=-1)
    q_scale = float(hd) ** -0.5 * LOG2_E
    nw = jnp.concatenate([jnp.tile(q_norm * q_scale, heads), jnp.tile(k_norm, heads)]).reshape(1, 2 * width)
    return pl.pallas_call(
        functools.partial(_attn_qk_kernel, head_dim=hd),
        out_shape=jax.ShapeDtypeStruct((m, 2 * width), BF16),
        grid=(2 * width // tn, m // tm),
        in_specs=[
            pl.BlockSpec((tm, kdim), lambda n, i: (i, 0)),
            pl.BlockSpec((kdim, tn), lambda n, i: (0, n)),
            pl.BlockSpec((tm, hd), lambda n, i: (i % tps, 0)),
            pl.BlockSpec((tm, hd), lambda n, i: (i % tps, 0)),
            pl.BlockSpec((1, tn), lambda n, i: (0, n)),
        ],
        out_specs=pl.BlockSpec((tm, tn), lambda n, i: (i, n)),
        scratch_shapes=[pltpu.VMEM((2, ROW_SUB, tn), F32)],
        compiler_params=_cparams("parallel", "parallel"),
        name="attn_qk_proj",
    )(h, w_in, cos_t, sin_t, nw)


def _moba_kernel(q_ref, k_ref, v_ref, z_ref, e_ref, u_ref,
                 qaugt_ref, kaug_ref, vt_ref, m_ref, acc_ref, sbuf_ref, cbuf_ref, abuf_ref, pbuf_ref, *,
                 bs, nb, topk, hd, hp, pw, pair_list):
    seq = nb * bs
    nt = (((1,), (1,)), ((), ()))

    for h in range(hp):
        hs = slice(h * hd, (h + 1) * hd)
        kf = k_ref[:, hs].astype(F32)
        kmean = jnp.sum(kf.reshape(nb, bs, hd), axis=1) * (1.0 / bs)
        g_t = lax.dot_general(kmean, q_ref[:, hs].astype(F32), nt, preferred_element_type=F32,
                              precision=HIGHEST)
        jj = lax.broadcasted_iota(jnp.int32, (nb, seq), 0)
        qb = lax.broadcasted_iota(jnp.int32, (nb, seq), 1) // bs
        rank = jnp.zeros((nb, seq), jnp.int32)
        for jp in range(nb):
            row = g_t[jp:jp + 1, :]
            beats = (row > g_t) | ((row == g_t) & (jp < jj))
            rank = rank + jnp.where(beats & (jp < qb), 1, 0)
        allowed = ((jj < qb) & (rank < topk)) | (jj == qb)
        bias_t = jnp.where(allowed, 0.0, MASK_BIAS).astype(BF16)
        qaugt_ref[h, hd:hd + nb, :] = bias_t
        qaugt_ref[h, hd + nb:hd + LANES, :] = jnp.zeros((LANES - nb, seq), BF16)
        kaug_ref[h, :, 0:hd] = k_ref[:, hs]
        kaug_ref[h, :, hd:hd + LANES] = e_ref[...]
        for jb in range(nb):
            rs = slice(jb * bs, (jb + 1) * bs)
            qaugt_ref[h, 0:hd, rs] = q_ref[rs, hs].astype(F32).T.astype(BF16)
            vt_ref[h, 0:hd, rs] = v_ref[rs, hs].astype(F32).T.astype(BF16)
        ones_row = lax.broadcasted_iota(jnp.int32, (8, seq), 0) == 0
        vt_ref[h, hd:hd + 8, :] = jnp.where(ones_row, 1.0, 0.0).astype(BF16)
    m_ref[...] = jnp.full(m_ref.shape, -jnp.inf, F32)
    acc_ref[...] = jnp.zeros(acc_ref.shape, F32)

    key_id = lax.broadcasted_iota(jnp.int32, (bs, bs), 0)
    qry_id = lax.broadcasted_iota(jnp.int32, (bs, bs), 1)
    causal = key_id <= qry_id
    steps = [[(h, i, j) for (i, j) in pair_list[t:t + pw] for h in range(hp)]
             for t in range(0, len(pair_list), pw)]

    def stage_scores(us, slot):
        for n, (h, i, j) in enumerate(us):
            s = jnp.dot(kaug_ref[h, j * bs:(j + 1) * bs, :], qaugt_ref[h, :, i * bs:(i + 1) * bs],
                        preferred_element_type=F32)
            if i == j:
                s = jnp.where(causal, s, -jnp.inf)
            sbuf_ref[slot, n] = s
            cbuf_ref[slot, n] = jnp.max(s, axis=0, keepdims=True)

    def stage_softmax(us, slot):
        for n, (h, i, _) in enumerate(us):
            m_old = m_ref[h, i]
            m_new = jnp.maximum(m_old, cbuf_ref[slot, n])
            abuf_ref[slot, n] = jnp.exp2(m_old - m_new)
            m_ref[h, i] = m_new
            pbuf_ref[slot, n] = jnp.exp2(sbuf_ref[slot, n] - m_new).astype(BF16)

    def stage_pv(us, slot):
        for n, (h, i, j) in enumerate(us):
            pv = jnp.dot(vt_ref[h, :, j * bs:(j + 1) * bs], pbuf_ref[slot, n], preferred_element_type=F32)
            acc_ref[h, i] = abuf_ref[slot, n] * acc_ref[h, i] + pv

    def finalize(h, i):
        rows, hs = slice(i * bs, (i + 1) * bs), slice(h * hd, (h + 1) * hd)
        acc = acc_ref[h, i]
        o = (acc[0:hd, :] / acc[hd:hd + 1, :]).T
        u_ref[rows, hs] = (o * z_ref[rows, hs].astype(F32)).astype(BF16)

    last_step = {(h, i): t for t, us in enumerate(steps) for (h, i, _) in us}
    for t in range(len(steps) + 2):
        if t < len(steps):
            stage_scores(steps[t], t % 2)
        if 1 <= t <= len(steps):
            stage_softmax(steps[t - 1], (t - 1) % 2)
        if t >= 2:
            stage_pv(steps[t - 2], t % 2)
            for (h, i), t_last in last_step.items():
                if t_last == t - 2:
                    finalize(h, i)


def _moba_attention(qk, v, z_act, batch, seq, heads, hd, hp=1):
    m = qk.shape[0]
    bs = MOBA_BLOCK
    nb = seq // bs
    width = heads * hd
    groups = heads // hp
    e_tab = (jnp.arange(seq)[:, None] // bs == jnp.arange(LANES)[None, :]).astype(BF16)
    pair_list = [(i, i) for i in range(nb)] + [(i, j) for j in range(nb) for i in range(j + 1, nb)]
    pw = 16
    kern = functools.partial(_moba_kernel, bs=bs, nb=nb, topk=min(MOBA_TOPK, nb), hd=hd, hp=hp, pw=pw,
                             pair_list=tuple(pair_list))
    blk = (seq, hp * hd)
    return pl.pallas_call(
        kern,
        out_shape=jax.ShapeDtypeStruct((m, width), BF16),
        grid=(batch, groups),
        in_specs=[
            pl.BlockSpec(blk, lambda b, g: (b, g)),
            pl.BlockSpec(blk, lambda b, g: (b, groups + g)),
            pl.BlockSpec(blk, lambda b, g: (b, g)),
            pl.BlockSpec(blk, lambda b, g: (b, g)),
            pl.BlockSpec((seq, LANES), lambda b, g: (0, 0)),
        ],
        out_specs=pl.BlockSpec(blk, lambda b, g: (b, g)),
        scratch_shapes=[
            pltpu.VMEM((hp, hd + LANES, seq), BF16),
            pltpu.VMEM((hp, seq, hd + LANES), BF16),
            pltpu.VMEM((hp, hd + 8, seq), BF16),
            pltpu.VMEM((hp, nb, 1, bs), F32),
            pltpu.VMEM((hp, nb, hd + 8, bs), F32),
            pltpu.VMEM((2, pw * hp, bs, bs), F32),
            pltpu.VMEM((2, pw * hp, 1, bs), F32),
            pltpu.VMEM((2, pw * hp, 1, bs), F32),
            pltpu.VMEM((2, pw * hp, bs, bs), BF16),
        ],
        compiler_params=_cparams("parallel", "parallel"),
        name="moba_attn",
    )(qk, qk, v, z_act, e_tab)


def _outproj_kernel(u_ref, w_ref, x_ref, g_ref, o_ref):
    acc = jnp.dot(u_ref[...], w_ref[...], preferred_element_type=F32)
    o_ref[...] = x_ref[...] + g_ref[...] * acc


def _outproj(u, w_out, xf, gate, seq, tm=512, tn=1024):
    m, kdim = u.shape
    d = w_out.shape[1]
    b = m // seq
    assert seq % tm == 0, "row tiles must not straddle sequences"
    tps = seq // tm
    return pl.pallas_call(
        _outproj_kernel,
        out_shape=jax.ShapeDtypeStruct((m, d), F32),
        grid=(d // tn, m // tm),
        in_specs=[
            pl.BlockSpec((tm, kdim), lambda n, i: (i, 0)),
            pl.BlockSpec((kdim, tn), lambda n, i: (0, n)),
            pl.BlockSpec((tm, tn), lambda n, i: (i, n)),
            pl.BlockSpec((None, 1, tn), lambda n, i: (i // tps, 0, n)),
        ],
        out_specs=pl.BlockSpec((tm, tn), lambda n, i: (i, n)),
        compiler_params=_cparams("parallel", "parallel"),
        name="outproj",
    )(u, w_out, xf, gate.reshape(b, 1, d))


def _mlstm_qk_kernel(h_ref, w_ref, cw_ref, sc_ref, o_ref, ext_ref, *, tiles_per_seq, conv_k):
    i = pl.program_id(1)
    sub = ROW_SUB
    tiles = _row_tiles(h_ref)

    @pl.when(i % tiles_per_seq == 0)
    def _():
        ext_ref[0, 0:8, :] = jnp.zeros((8, ext_ref.shape[2]), F32)

    def issue(r):
        ext_ref[r % 2, 8:sub + 8, :] = _mm(h_ref[tiles[r], :], w_ref[...], False)

    def finish(r):
        cur = r % 2
        y = cw_ref[conv_k - 1:conv_k, :] * ext_ref[cur, 8:sub + 8, :]
        for back in range(1, conv_k):
            y = y + cw_ref[conv_k - 1 - back:conv_k - back, :] * ext_ref[cur, 8 - back:8 - back + sub, :]
        o_ref[tiles[r], :] = (_silu(y) * sc_ref[...]).astype(BF16)
        ext_ref[1 - cur, 0:8, :] = ext_ref[cur, sub:sub + 8, :]

    _staged(tiles, issue, finish)


def _mlstm_qk_proj(h, w_in, conv_w, qk_width, dk, seq, tm=2048, tn=1024):
    m, kdim = h.shape
    conv_k = conv_w.shape[0]
    assert seq % tm == 0, "row tiles must not straddle sequences"
    tps = seq // tm
    sc = jnp.concatenate([jnp.ones((qk_width,), F32),
                          jnp.full((qk_width,), float(dk) ** -0.5, F32)]).reshape(1, 2 * qk_width)
    return pl.pallas_call(
        functools.partial(_mlstm_qk_kernel, tiles_per_seq=tps, conv_k=conv_k),
        out_shape=jax.ShapeDtypeStruct((m, 2 * qk_width), BF16),
        grid=(2 * qk_width // tn, m // tm),
        in_specs=[
            pl.BlockSpec((tm, kdim), lambda n, i: (i, 0)),
            _w_spec(kdim, tn, 0, False),
            pl.BlockSpec((conv_k, tn), lambda n, i: (0, n)),
            pl.BlockSpec((1, tn), lambda n, i: (0, n)),
        ],
        out_specs=pl.BlockSpec((tm, tn), lambda n, i: (i, n)),
        scratch_shapes=[pltpu.VMEM((2, ROW_SUB + 8, tn), F32)],
        compiler_params=_cparams("arbitrary", "arbitrary"),
        name="mlstm_qk_proj",
    )(h, w_in, conv_w, sc)


def _mlstm_gz_kernel(h_ref, wo_ref, wz_ref, nw_ref, o_ref):
    for rows in _row_tiles(h_ref):
        a = h_ref[rows, :]
        og = _mm(a, wo_ref[...], False)
        z = _mm(a, wz_ref[...], False)
        o_ref[rows, :] = (_sigmoid(og) * _silu(z) * nw_ref[...]).astype(BF16)


def _mlstm_gz_proj(h, w_in, norm_w, o_col0, z_col0, width, tm=1024, tn=1024):
    m, kdim = h.shape
    return pl.pallas_call(
        _mlstm_gz_kernel,
        out_shape=jax.ShapeDtypeStruct((m, width), BF16),
        grid=(width // tn, m // tm),
        in_specs=[
            pl.BlockSpec((tm, kdim), lambda n, i: (i, 0)),
            _w_spec(kdim, tn, o_col0 // tn, False),
            _w_spec(kdim, tn, z_col0 // tn, False),
            pl.BlockSpec((1, tn), lambda n, i: (0, n)),
        ],
        out_specs=pl.BlockSpec((tm, tn), lambda n, i: (i, n)),
        compiler_params=_cparams("parallel", "parallel"),
        name="mlstm_gz_proj",
    )(h, w_in, w_in, norm_w.reshape(1, width))


def _gate_proj_kernel(h_ref, w_ref, b_ref, rows_ref, cols_ref, *, heads, chunk):
    g = jnp.dot(h_ref[...], w_ref[...].astype(BF16), preferred_element_type=F32) + b_ref[...]
    tm = g.shape[0]
    g_t = g.T
    i_rows = g_t[0:heads, :]
    f_rows = g_t[heads:2 * heads, :]
    b = jnp.minimum(f_rows, 0.0) - jnp.log1p(jnp.exp(-jnp.abs(f_rows)))
    pos = lax.broadcasted_iota(jnp.int32, b.shape, 1) % chunk
    shift = 1
    while shift < chunk:
        b = b + jnp.where(pos >= shift, pltpu.roll(b, shift, axis=1), 0.0)
        shift *= 2
    rows = jnp.concatenate([i_rows, b], axis=0)
    rows_ref[...] = rows
    cols_ref[...] = jnp.concatenate([rows, jnp.zeros((LANES - 2 * heads, tm), F32)], axis=0).T


def _gate_proj(h, w_gate, gate_b, batch, seq, tm=1024):
    m, kdim = h.shape
    ng = w_gate.shape[1]
    heads = ng // 2
    assert seq % tm == 0, "row tiles must not straddle sequences"
    tps = seq // tm
    w_pad = jnp.zeros((kdim, LANES), F32).at[:, :ng].set(w_gate)
    b_pad = jnp.zeros((1, LANES), F32).at[0, :ng].set(gate_b)
    kern = functools.partial(_gate_proj_kernel, heads=heads, chunk=MLSTM_CHUNK)
    return pl.pallas_call(
        kern,
        out_shape=(jax.ShapeDtypeStruct((batch, ng, seq), F32), jax.ShapeDtypeStruct((m, LANES), F32)),
        grid=(m // tm,),
        in_specs=[
            pl.BlockSpec((tm, kdim), lambda i: (i, 0)),
            pl.BlockSpec((kdim, LANES), lambda i: (0, 0)),
            pl.BlockSpec((1, LANES), lambda i: (0, 0)),
        ],
        out_specs=(pl.BlockSpec((None, ng, tm), lambda i: (i // tps, 0, i % tps)),
                   pl.BlockSpec((tm, LANES), lambda i: (i, 0))),
        compiler_params=_cparams("parallel"),
        name="mlstm_gate_proj",
    )(h, w_pad, b_pad)


def _mlstm_kernel(q_ref, k_ref, v_ref, gz_ref, rows_ref, cols_ref, u_ref, c_ref, m_ref, *,
                  chunk, heads, hp, dk, dv):
    g = pl.program_id(1)
    c = pl.program_id(2)
    L = chunk
    nt = (((1,), (1,)), ((), ()))

    @pl.when(c == 0)
    def _():
        c_ref[...] = jnp.zeros(c_ref.shape, F32)
        m_ref[...] = jnp.zeros(m_ref.shape, F32)

    t_id = lax.broadcasted_iota(jnp.int32, (L, L), 0)
    s_id = lax.broadcasted_iota(jnp.int32, (L, L), 1)
    causal = s_id <= t_id
    lane = lax.broadcasted_iota(jnp.int32, (L, LANES), 1)
    ones_blk = jnp.where(lane == 0, 1.0, 0.0).astype(BF16)
    ones_cols = jnp.ones((dv, LANES), BF16)
    cols = cols_ref[...]

    def col_of(idx):
        if isinstance(idx, int):
            return cols[:, idx:idx + 1]
        return jnp.sum(jnp.where(lane == idx, cols, 0.0), axis=1, keepdims=True)

    gate = []
    for hh in range(hp):
        h = hh if hp == heads else g * hp + hh
        i_row = rows_ref[pl.ds(h, 1), :]
        b_row = rows_ref[pl.ds(heads + h, 1), :]
        i_col = col_of(h)
        b_col = col_of(heads + h)
        m_prev = m_ref[hh]
        log_d = jnp.where(causal, b_col - b_row + i_row, -jnp.inf)
        log_inter = b_col + m_prev
        m_t = jnp.maximum(log_inter, jnp.max(log_d, axis=1, keepdims=True))
        d_mat = jnp.exp(log_d - m_t)
        w_inter = jnp.exp(log_inter - m_t)
        b_last = b_row[:, L - 1:L]
        m_new = jnp.maximum(b_last + m_prev, jnp.max(b_last - b_row + i_row, axis=1, keepdims=True))
        decay = jnp.exp(b_last + m_prev - m_new)
        ws_col = jnp.exp(b_last - b_col + i_col - m_new)
        m_ref[hh] = m_new
        gate.append((m_t, d_mat, w_inter, decay, ws_col))

    s_qk = []
    for hh in range(hp):
        q = q_ref[:, hh * dk:(hh + 1) * dk]
        k = k_ref[:, hh * dk:(hh + 1) * dk]
        s_qk.append((lax.dot_general(q, k, nt, preferred_element_type=F32) * gate[hh][1]).astype(BF16))

    for hh in range(hp):
        m_t, _, w_inter, _, _ = gate[hh]
        q_scaled = q_ref[:, hh * dk:(hh + 1) * dk] * w_inter.astype(BF16)
        v_aug = jnp.concatenate([v_ref[:, hh * dv:(hh + 1) * dv], ones_blk], axis=1)
        lhs = jnp.concatenate([q_scaled, s_qk[hh]], axis=1)
        rhs = jnp.concatenate([c_ref[hh].astype(BF16), v_aug], axis=0)
        num = jnp.dot(lhs, rhs, preferred_element_type=F32)
        d = jnp.maximum(jnp.abs(num[:, dv:dv + 1]), jnp.exp(-m_t))
        nh = num[:, :dv]
        sq_sum = jnp.dot((nh * nh).astype(BF16), ones_cols, preferred_element_type=F32)[:, 0:1]
        r = lax.rsqrt(sq_sum * (1.0 / dv) + NORM_EPS * (d * d))
        u_ref[:, hh * dv:(hh + 1) * dv] = (nh * r * gz_ref[:, hh * dv:(hh + 1) * dv].astype(F32)).astype(BF16)

    for hh in range(hp):
        _, _, _, decay, ws_col = gate[hh]
        k_scaled = k_ref[:, hh * dk:(hh + 1) * dk] * ws_col.astype(BF16)
        v_aug = jnp.concatenate([v_ref[:, hh * dv:(hh + 1) * dv], ones_blk], axis=1)
        upd = lax.dot_general(k_scaled, v_aug, (((0,), (0,)), ((), ())), preferred_element_type=F32)
        c_ref[hh] = decay * c_ref[hh] + upd


def _mlstm_core(qk, v, gz, rows, cols, batch, seq, heads, dk, dv, hp=None):
    m = qk.shape[0]
    hp = heads if hp is None else hp
    L = MLSTM_CHUNK
    nc = seq // L
    groups = heads // hp
    v_width = heads * dv
    k_off = groups
    kern = functools.partial(_mlstm_kernel, chunk=L, heads=heads, hp=hp, dk=dk, dv=dv)
    return pl.pallas_call(
        kern,
        out_shape=jax.ShapeDtypeStruct((m, v_width), BF16),
        grid=(batch, groups, nc),
        in_specs=[
            pl.BlockSpec((L, hp * dk), lambda b, g, c: (b * nc + c, g)),
            pl.BlockSpec((L, hp * dk), lambda b, g, c: (b * nc + c, k_off + g)),
            pl.BlockSpec((L, hp * dv), lambda b, g, c: (b * nc + c, g)),
            pl.BlockSpec((L, hp * dv), lambda b, g, c: (b * nc + c, g)),
            pl.BlockSpec((None, 2 * heads, L), lambda b, g, c: (b, 0, c)),
            pl.BlockSpec((L, LANES), lambda b, g, c: (b * nc + c, 0)),
        ],
        out_specs=pl.BlockSpec((L, hp * dv), lambda b, g, c: (b * nc + c, g)),
        scratch_shapes=[pltpu.VMEM((hp, dk, dv + LANES), F32), pltpu.VMEM((hp, 1, 1), F32)],
        compiler_params=_cparams("parallel", "parallel", "arbitrary"),
        name="mlstm_core",
    )(qk, qk, v, gz, rows, cols)


def _moba_layer(xf, h, gate, w_in, q_norm, k_norm, w_out, batch, seq):
    hd = q_norm.shape[0]
    heads = w_in.shape[1] // 4 // hd
    width = heads * hd
    w_in = w_in.astype(BF16)
    qk = _attn_qk_proj(h, w_in, q_norm, k_norm, seq)
    v = _proj_act(h, w_in, 2 * width, width, None, "attn_v_proj")
    z_act = _proj_act(h, w_in, 3 * width, width, _silu, "attn_z_proj")
    u = _moba_attention(qk, v, z_act, batch, seq, heads, hd)
    return _outproj(u, w_out.astype(BF16), xf, gate, seq, tn=w_out.shape[1])


def _mlstm_layer(xf, h, gate, w_in, gate_b, conv_w, norm_w, w_out, batch, seq):
    heads = gate_b.shape[0] // 2
    v_width = w_out.shape[0]
    dv = v_width // heads
    qk_width = conv_w.shape[1] // 2
    dk = qk_width // heads
    n_main = 2 * qk_width + 3 * v_width
    w_main = w_in[:, :n_main].astype(BF16)
    qk = _mlstm_qk_proj(h, w_main, conv_w, qk_width, dk, seq)
    v = _proj_act(h, w_main, 2 * qk_width, v_width, None, "mlstm_v_proj")
    gz = _mlstm_gz_proj(h, w_main, norm_w, 2 * qk_width + v_width, 2 * qk_width + 2 * v_width, v_width)
    rows, cols = _gate_proj(h, w_in[:, n_main:], gate_b, batch, seq)
    u = _mlstm_core(qk, v, gz, rows, cols, batch, seq, heads, dk, dv)
    return _outproj(u, w_out.astype(BF16), xf, gate, seq)


def kernel(x, c, norm_w, ada_w, ada_b, att_w_in, att_q_norm, att_k_norm, att_w_out, mlstm_w_in,
           mlstm_gate_b, mlstm_conv_w, mlstm_norm_w, mlstm_w_out):
    batch, seq, d = x.shape
    depth = norm_w.shape[0]
    mod = _adaln(c, ada_w, ada_b)
    xf = x.reshape(batch * seq, d)
    for layer in range(depth):
        shift, scale, gate = mod[layer, :, :d], mod[layer, :, d:2 * d], mod[layer, :, 2 * d:]
        h = _norm_mod(xf, norm_w[layer], scale, shift, seq)
        j = layer // 2
        if layer % 2 == 0:
            xf = _moba_layer(xf, h, gate, att_w_in[j], att_q_norm[j], att_k_norm[j], att_w_out[j], batch, seq)
        else:
            xf = _mlstm_layer(xf, h, gate, mlstm_w_in[j], mlstm_gate_b[j], mlstm_conv_w[j],
                              mlstm_norm_w[j], mlstm_w_out[j], batch, seq)
    return xf.reshape(batch, seq, d)
```

```python
import functools

import jax
import jax.numpy as jnp
from jax import lax
from jax.experimental import pallas as pl
from jax.experimental.pallas import tpu as pltpu

F32 = jnp.float32
BF16 = jnp.bfloat16
HIGHEST = lax.Precision.HIGHEST

NORM_EPS = 1e-6
MOBA_BLOCK = 256
MOBA_TOPK = 3
ROPE_THETA = 10000.0
MLSTM_CHUNK = 256
MASK_BIAS = -1e30
LANES = 128
LOG2_E = 1.4426950408889634
ROW_SUB = 512

_VMEM_LIMIT = 48 * 1024 * 1024


def _sigmoid(x):
    return 1.0 / (1.0 + jnp.exp(-x))


def _cparams(*sem):
    return pltpu.CompilerParams(dimension_semantics=sem, vmem_limit_bytes=_VMEM_LIMIT)


def _adaln_kernel(c_ref, w_ref, b_ref, o_ref):
    c = c_ref[...]
    cond = c * _sigmoid(c)
    o_ref[...] = jnp.dot(cond, w_ref[...], preferred_element_type=F32, precision=HIGHEST) + b_ref[...]


def _adaln(c, ada_w, ada_b, tn=1536):
    depth, d, n3 = ada_w.shape
    b = c.shape[0]
    rows = 8
    c_pad = jnp.zeros((rows, d), F32).at[:b].set(c)
    out = pl.pallas_call(
        _adaln_kernel,
        out_shape=jax.ShapeDtypeStruct((depth, rows, n3), F32),
        grid=(depth, n3 // tn),
        in_specs=[
            pl.BlockSpec((rows, d), lambda l, n: (0, 0)),
            pl.BlockSpec((None, d, tn), lambda l, n: (l, 0, n)),
            pl.BlockSpec((None, 1, tn), lambda l, n: (l, 0, n)),
        ],
        out_specs=pl.BlockSpec((None, rows, tn), lambda l, n: (l, 0, n)),
        compiler_params=_cparams("parallel", "parallel"),
        name="adaln_mod",
    )(c_pad, ada_w, ada_b.reshape(depth, 1, n3))
    return out[:, :b]


def _norm_mod_kernel(x_ref, nw_ref, sc_ref, sh_ref, o_ref):
    x = x_ref[...]
    ms = jnp.mean(x * x, axis=-1, keepdims=True)
    y = x * lax.rsqrt(ms + NORM_EPS) * nw_ref[...]
    o_ref[...] = (y * (1.0 + sc_ref[...]) + sh_ref[...]).astype(BF16)


def _norm_mod(xf, nw, scale, shift, seq, ts=1024):
    m, d = xf.shape
    b = m // seq
    tps = seq // ts
    return pl.pallas_call(
        _norm_mod_kernel,
        out_shape=jax.ShapeDtypeStruct((m, d), BF16),
        grid=(m // ts,),
        in_specs=[
            pl.BlockSpec((ts, d), lambda i: (i, 0)),
            pl.BlockSpec((1, d), lambda i: (0, 0)),
            pl.BlockSpec((None, 1, d), lambda i: (i // tps, 0, 0)),
            pl.BlockSpec((None, 1, d), lambda i: (i // tps, 0, 0)),
        ],
        out_specs=pl.BlockSpec((ts, d), lambda i: (i, 0)),
        compiler_params=_cparams("parallel"),
        name="norm_mod",
    )(xf, nw.reshape(1, d), scale.reshape(b, 1, d), shift.reshape(b, 1, d))


def _row_tiles(a_ref):
    return [slice(r, r + ROW_SUB) for r in range(0, a_ref.shape[0], ROW_SUB)]


def _silu(x):
    return x * _sigmoid(x)


def _mm(a, w, w_t):
    dims = (((1,), (1,)), ((), ())) if w_t else (((1,), (0,)), ((), ()))
    return lax.dot_general(a, w, dims, preferred_element_type=F32)


def _w_spec(kdim, tn, off, w_t):
    if w_t:
        return pl.BlockSpec((tn, kdim), lambda n, i: (off + n, 0))
    return pl.BlockSpec((kdim, tn), lambda n, i: (0, off + n))


def _proj_act_kernel(h_ref, w_ref, o_ref, *, act, w_t):
    for rows in _row_tiles(h_ref):
        acc = _mm(h_ref[rows, :], w_ref[...], w_t)
        o_ref[rows, :] = (acc if act is None else act(acc)).astype(BF16)


def _proj_act(h, w, col0, width, act, name, w_t=False, tm=1024, tn=1024):
    m, kdim = h.shape
    return pl.pallas_call(
        functools.partial(_proj_act_kernel, act=act, w_t=w_t),
        out_shape=jax.ShapeDtypeStruct((m, width), BF16),
        grid=(width // tn, m // tm),
        in_specs=[
            pl.BlockSpec((tm, kdim), lambda n, i: (i, 0)),
            _w_spec(kdim, tn, col0 // tn, w_t),
        ],
        out_specs=pl.BlockSpec((tm, tn), lambda n, i: (i, n)),
        compiler_params=_cparams("parallel", "parallel"),
        name=name,
    )(h, w)


def _staged(tiles, issue, finish):
    issue(0)
    for r in range(len(tiles)):
        if r + 1 < len(tiles):
            issue(r + 1)
        finish(r)


def _attn_qk_kernel(h_ref, w_ref, cos_ref, sin_ref, nw_ref, o_ref, acc_ref, *, head_dim):
    tiles = _row_tiles(h_ref)

    def issue(r):
        acc_ref[r % 2] = jnp.dot(h_ref[tiles[r], :], w_ref[...], preferred_element_type=F32)

    def finish(r):
        rows = tiles[r]
        cos = cos_ref[rows, :]
        sin = sin_ref[rows, :]
        for hh in range(acc_ref.shape[2] // head_dim):
            sl = slice(hh * head_dim, (hh + 1) * head_dim)
            xh = acc_ref[r % 2, :, sl]
            ms = jnp.mean(xh * xh, axis=-1, keepdims=True)
            y = xh * lax.rsqrt(ms + NORM_EPS) * nw_ref[:, sl]
            o_ref[rows, sl] = (y * cos + pltpu.roll(y, head_dim // 2, axis=1) * sin).astype(BF16)

    _staged(tiles, issue, finish)


def _attn_qk_proj(h, w_in, q_norm, k_norm, seq, tm=2048, tn=1024):
    m, kdim = h.shape
    width = w_in.shape[1] // 4
    hd = q_norm.shape[0]
    heads = width // hd
    assert seq % tm == 0, "row tiles must not straddle sequences"
    tps = seq // tm
    half = hd // 2
    inv = ROPE_THETA ** (-jnp.arange(half, dtype=F32) / half)
    ang = jnp.arange(seq, dtype=F32)[:, None] * inv[None, :]
    cos = jnp.cos(ang)
    sin = jnp.sin(ang)
    cos_t = jnp.concatenate([cos, cos], axis=-1)
    sin_t = jnp.concatenate([-sin, sin], axis=-1)
    q_scale = float(hd) ** -0.5 * LOG2_E
    nw = jnp.concatenate([jnp.tile(q_norm * q_scale, heads), jnp.tile(k_norm, heads)]).reshape(1, 2 * width)
    return pl.pallas_call(
        functools.partial(_attn_qk_kernel, head_dim=hd),
        out_shape=jax.ShapeDtypeStruct((m, 2 * width), BF16),
        grid=(2 * width // tn, m // tm),
        in_specs=[
            pl.BlockSpec((tm, kdim), lambda n, i: (i, 0)),
            pl.BlockSpec((kdim, tn), lambda n, i: (0, n)),
            pl.BlockSpec((tm, hd), lambda n, i: (i % tps, 0)),
            pl.BlockSpec((tm, hd), lambda n, i: (i % tps, 0)),
            pl.BlockSpec((1, tn), lambda n, i: (0, n)),
        ],
        out_specs=pl.BlockSpec((tm, tn), lambda n, i: (i, n)),
        scratch_shapes=[pltpu.VMEM((2, ROW_SUB, tn), F32)],
        compiler_params=_cparams("parallel", "parallel"),
        name="attn_qk_proj",
    )(h, w_in, cos_t, sin_t, nw)


def _moba_kernel(q_ref, k_ref, v_ref, z_ref, e_ref, u_ref,
                 qaugt_ref, kaug_ref, vt_ref, m_ref, acc_ref, sbuf_ref, cbuf_ref, abuf_ref, pbuf_ref, *,
                 bs, nb, topk, hd, hp, pw, pair_list):
    seq = nb * bs
    nt = (((1,), (1,)), ((), ()))

    for h in range(hp):
        hs = slice(h * hd, (h + 1) * hd)
        kf = k_ref[:, hs].astype(F32)
        kmean = jnp.sum(kf.reshape(nb, bs, hd), axis=1) * (1.0 / bs)
        g_t = lax.dot_general(kmean, q_ref[:, hs].astype(F32), nt, preferred_element_type=F32,
                              precision=HIGHEST)
        jj = lax.broadcasted_iota(jnp.int32, (nb, seq), 0)
        qb = lax.broadcasted_iota(jnp.int32, (nb, seq), 1) // bs
        rank = jnp.zeros((nb, seq), jnp.int32)
        for jp in range(nb):
            row = g_t[jp:jp + 1, :]
            beats = (row > g_t) | ((row == g_t) & (jp < jj))
            rank = rank + jnp.where(beats & (jp < qb), 1, 0)
        allowed = ((jj < qb) & (rank < topk)) | (jj == qb)
        bias_t = jnp.where(allowed, 0.0, MASK_BIAS).astype(BF16)
        qaugt_ref[h, hd:hd + nb, :] = bias_t
        qaugt_ref[h, hd + nb:hd + LANES, :] = jnp.zeros((LANES - nb, seq), BF16)
        kaug_ref[h, :, 0:hd] = k_ref[:, hs]
        kaug_ref[h, :, hd:hd + LANES] = e_ref[...]
        for jb in range(nb):
            rs = slice(jb * bs, (jb + 1) * bs)
            qaugt_ref[h, 0:hd, rs] = q_ref[rs, hs].astype(F32).T.astype(BF16)
            vt_ref[h, 0:hd, rs] = v_ref[rs, hs].astype(F32).T.astype(BF16)
        ones_row = lax.broadcasted_iota(jnp.int32, (8, seq), 0) == 0
        vt_ref[h, hd:hd + 8, :] = jnp.where(ones_row, 1.0, 0.0).astype(BF16)
    m_ref[...] = jnp.full(m_ref.shape, -jnp.inf, F32)
    acc_ref[...] = jnp.zeros(acc_ref.shape, F32)

    key_id = lax.broadcasted_iota(jnp.int32, (bs, bs), 0)
    qry_id = lax.broadcasted_iota(jnp.int32, (bs, bs), 1)
    causal = key_id <= qry_id
    steps = [[(h, i, j) for (i, j) in pair_list[t:t + pw] for h in range(hp)]
             for t in range(0, len(pair_list), pw)]

    def stage_scores(us, slot):
        for n, (h, i, j) in enumerate(us):
            s = jnp.dot(kaug_ref[h, j * bs:(j + 1) * bs, :], qaugt_ref[h, :, i * bs:(i + 1) * bs],
                        preferred_element_type=F32)
            if i == j:
                s = jnp.where(causal, s, -jnp.inf)
            sbuf_ref[slot, n] = s
            cbuf_ref[slot, n] = jnp.max(s, axis=0, keepdims=True)

    def stage_softmax(us, slot):
        for n, (h, i, _) in enumerate(us):
            m_old = m_ref[h, i]
            m_new = jnp.maximum(m_old, cbuf_ref[slot, n])
            abuf_ref[slot, n] = jnp.exp2(m_old - m_new)
            m_ref[h, i] = m_new
            pbuf_ref[slot, n] = jnp.exp2(sbuf_ref[slot, n] - m_new).astype(BF16)

    def stage_pv(us, slot):
        for n, (h, i, j) in enumerate(us):
            pv = jnp.dot(vt_ref[h, :, j * bs:(j + 1) * bs], pbuf_ref[slot, n], preferred_element_type=F32)
            acc_ref[h, i] = abuf_ref[slot, n] * acc_ref[h, i] + pv

    def finalize(h, i):
        rows, hs = slice(i * bs, (i + 1) * bs), slice(h * hd, (h + 1) * hd)
        acc = acc_ref[h, i]
        o = (acc[0:hd, :] / acc[hd:hd + 1, :]).T
        u_ref[rows, hs] = (o * z_ref[rows, hs].astype(F32)).astype(BF16)

    last_step = {(h, i): t for t, us in enumerate(steps) for (h, i, _) in us}
    for t in range(len(steps) + 2):
        if t < len(steps):
            stage_scores(steps[t], t % 2)
        if 1 <= t <= len(steps):
            stage_softmax(steps[t - 1], (t - 1) % 2)
        if t >= 2:
            stage_pv(steps[t - 2], t % 2)
            for (h, i), t_last in last_step.items():
                if t_last == t - 2:
                    finalize(h, i)


def _moba_attention(qk, v, z_act, batch, seq, heads, hd, hp=1):
    m = qk.shape[0]
    bs = MOBA_BLOCK
    nb = seq // bs
    width = heads * hd
    groups = heads // hp
    e_tab = (jnp.arange(seq)[:, None] // bs == jnp.arange(LANES)[None, :]).astype(BF16)
    pair_list = [(i, i) for i in range(nb)] + [(i, j) for j in range(nb) for i in range(j + 1, nb)]
    pw = 16
    kern = functools.partial(_moba_kernel, bs=bs, nb=nb, topk=min(MOBA_TOPK, nb), hd=hd, hp=hp, pw=pw,
                             pair_list=tuple(pair_list))
    blk = (seq, hp * hd)
    return pl.pallas_call(
        kern,
        out_shape=jax.ShapeDtypeStruct((m, width), BF16),
        grid=(batch, groups),
        in_specs=[
            pl.BlockSpec(blk, lambda b, g: (b, g)),
            pl.BlockSpec(blk, lambda b, g: (b, groups + g)),
            pl.BlockSpec(blk, lambda b, g: (b, g)),
            pl.BlockSpec(blk, lambda b, g: (b, g)),
            pl.BlockSpec((seq, LANES), lambda b, g: (0, 0)),
        ],
        out_specs=pl.BlockSpec(blk, lambda b, g: (b, g)),
        scratch_shapes=[
            pltpu.VMEM((hp, hd + LANES, seq), BF16),
            pltpu.VMEM((hp, seq, hd + LANES), BF16),
            pltpu.VMEM((hp, hd + 8, seq), BF16),
            pltpu.VMEM((hp, nb, 1, bs), F32),
            pltpu.VMEM((hp, nb, hd + 8, bs), F32),
            pltpu.VMEM((2, pw * hp, bs, bs), F32),
            pltpu.VMEM((2, pw * hp, 1, bs), F32),
            pltpu.VMEM((2, pw * hp, 1, bs), F32),
            pltpu.VMEM((2, pw * hp, bs, bs), BF16),
        ],
        compiler_params=_cparams("parallel", "parallel"),
        name="moba_attn",
    )(qk, qk, v, z_act, e_tab)


def _outproj_kernel(u_ref, w_ref, x_ref, g_ref, o_ref):
    acc = jnp.dot(u_ref[...], w_ref[...], preferred_element_type=F32)
    o_ref[...] = x_ref[...] + g_ref[...] * acc


def _outproj(u, w_out, xf, gate, seq, tm=512, tn=1024):
    m, kdim = u.shape
    d = w_out.shape[1]
    b = m // seq
    assert seq % tm == 0, "row tiles must not straddle sequences"
    tps = seq // tm
    return pl.pallas_call(
        _outproj_kernel,
        out_shape=jax.ShapeDtypeStruct((m, d), F32),
        grid=(d // tn, m // tm),
        in_specs=[
            pl.BlockSpec((tm, kdim), lambda n, i: (i, 0)),
            pl.BlockSpec((kdim, tn), lambda n, i: (0, n)),
            pl.BlockSpec((tm, tn), lambda n, i: (i, n)),
            pl.BlockSpec((None, 1, tn), lambda n, i: (i // tps, 0, n)),
        ],
        out_specs=pl.BlockSpec((tm, tn), lambda n, i: (i, n)),
        compiler_params=_cparams("parallel", "parallel"),
        name="outproj",
    )(u, w_out, xf, gate.reshape(b, 1, d))


def _mlstm_qk_kernel(h_ref, w_ref, cw_ref, sc_ref, o_ref, ext_ref, *, tiles_per_seq, conv_k):
    i = pl.program_id(1)
    sub = ROW_SUB
    tiles = _row_tiles(h_ref)

    @pl.when(i % tiles_per_seq == 0)
    def _():
        ext_ref[0, 0:8, :] = jnp.zeros((8, ext_ref.shape[2]), F32)

    def issue(r):
        ext_ref[r % 2, 8:sub + 8, :] = _mm(h_ref[tiles[r], :], w_ref[...], False)

    def finish(r):
        cur = r % 2
        y = cw_ref[conv_k - 1:conv_k, :] * ext_ref[cur, 8:sub + 8, :]
        for back in range(1, conv_k):
            y = y + cw_ref[conv_k - 1 - back:conv_k - back, :] * ext_ref[cur, 8 - back:8 - back + sub, :]
        o_ref[tiles[r], :] = (_silu(y) * sc_ref[...]).astype(BF16)
        ext_ref[1 - cur, 0:8, :] = ext_ref[cur, sub:sub + 8, :]

    _staged(tiles, issue, finish)


def _mlstm_qk_proj(h, w_in, conv_w, qk_width, dk, seq, tm=2048, tn=1024):
    m, kdim = h.shape
    conv_k = conv_w.shape[0]
    assert seq % tm == 0, "row tiles must not straddle sequences"
    tps = seq // tm
    sc = jnp.concatenate([jnp.ones((qk_width,), F32),
                          jnp.full((qk_width,), float(dk) ** -0.5, F32)]).reshape(1, 2 * qk_width)
    return pl.pallas_call(
        functools.partial(_mlstm_qk_kernel, tiles_per_seq=tps, conv_k=conv_k),
        out_shape=jax.ShapeDtypeStruct((m, 2 * qk_width), BF16),
        grid=(2 * qk_width // tn, m // tm),
        in_specs=[
            pl.BlockSpec((tm, kdim), lambda n, i: (i, 0)),
            _w_spec(kdim, tn, 0, False),
            pl.BlockSpec((conv_k, tn), lambda n, i: (0, n)),
            pl.BlockSpec((1, tn), lambda n, i: (0, n)),
        ],
        out_specs=pl.BlockSpec((tm, tn), lambda n, i: (i, n)),
        scratch_shapes=[pltpu.VMEM((2, ROW_SUB + 8, tn), F32)],
        compiler_params=_cparams("arbitrary", "arbitrary"),
        name="mlstm_qk_proj",
    )(h, w_in, conv_w, sc)


def _mlstm_gz_kernel(h_ref, wo_ref, wz_ref, nw_ref, o_ref):
    for rows in _row_tiles(h_ref):
        a = h_ref[rows, :]
        og = _mm(a, wo_ref[...], False)
        z = _mm(a, wz_ref[...], False)
        o_ref[rows, :] = (_sigmoid(og) * _silu(z) * nw_ref[...]).astype(BF16)


def _mlstm_gz_proj(h, w_in, norm_w, o_col0, z_col0, width, tm=1024, tn=1024):
    m, kdim = h.shape
    return pl.pallas_call(
        _mlstm_gz_kernel,
        out_shape=jax.ShapeDtypeStruct((m, width), BF16),
        grid=(width // tn, m // tm),
        in_specs=[
            pl.BlockSpec((tm, kdim), lambda n, i: (i, 0)),
            _w_spec(kdim, tn, o_col0 // tn, False),
            _w_spec(kdim, tn, z_col0 // tn, False),
            pl.BlockSpec((1, tn), lambda n, i: (0, n)),
        ],
        out_specs=pl.BlockSpec((tm, tn), lambda n, i: (i, n)),
        compiler_params=_cparams("parallel", "parallel"),
        name="mlstm_gz_proj",
    )(h, w_in, w_in, norm_w.reshape(1, width))


def _gate_proj_kernel(h_ref, w_ref, b_ref, rows_ref, cols_ref, *, heads, chunk):
    g = jnp.dot(h_ref[...], w_ref[...].astype(BF16), preferred_element_type=F32) + b_ref[...]
    tm = g.shape[0]
    g_t = g.T
    i_rows = g_t[0:heads, :]
    f_rows = g_t[heads:2 * heads, :]
    b = jnp.minimum(f_rows, 0.0) - jnp.log1p(jnp.exp(-jnp.abs(f_rows)))
    pos = lax.broadcasted_iota(jnp.int32, b.shape, 1) % chunk
    shift = 1
    while shift < chunk:
        b = b + jnp.where(pos >= shift, pltpu.roll(b, shift, axis=1), 0.0)
        shift *= 2
    rows = jnp.concatenate([i_rows, b], axis=0)
    rows_ref[...] = rows
    cols_ref[...] = jnp.concatenate([rows, jnp.zeros((LANES - 2 * heads, tm), F32)], axis=0).T


def _gate_proj(h, w_gate, gate_b, batch, seq, tm=1024):
    m, kdim = h.shape
    ng = w_gate.shape[1]
    heads = ng // 2
    assert seq % tm == 0, "row tiles must not straddle sequences"
    tps = seq // tm
    w_pad = jnp.zeros((kdim, LANES), F32).at[:, :ng].set(w_gate)
    b_pad = jnp.zeros((1, LANES), F32).at[0, :ng].set(gate_b)
    kern = functools.partial(_gate_proj_kernel, heads=heads, chunk=MLSTM_CHUNK)
    return pl.pallas_call(
        kern,
        out_shape=(jax.ShapeDtypeStruct((batch, ng, seq), F32), jax.ShapeDtypeStruct((m, LANES), F32)),
        grid=(m // tm,),
        in_specs=[
            pl.BlockSpec((tm, kdim), lambda i: (i, 0)),
            pl.BlockSpec((kdim, LANES), lambda i: (0, 0)),
            pl.BlockSpec((1, LANES), lambda i: (0, 0)),
        ],
        out_specs=(pl.BlockSpec((None, ng, tm), lambda i: (i // tps, 0, i % tps)),
                   pl.BlockSpec((tm, LANES), lambda i: (i, 0))),
        compiler_params=_cparams("parallel"),
        name="mlstm_gate_proj",
    )(h, w_pad, b_pad)


def _mlstm_kernel(q_ref, k_ref, v_ref, gz_ref, rows_ref, cols_ref, u_ref, c_ref, m_ref, *,
                  chunk, heads, hp, dk, dv):
    g = pl.program_id(1)
    c = pl.program_id(2)
    L = chunk
    nt = (((1,), (1,)), ((), ()))

    @pl.when(c == 0)
    def _():
        c_ref[...] = jnp.zeros(c_ref.shape, F32)
        m_ref[...] = jnp.zeros(m_ref.shape, F32)

    t_id = lax.broadcasted_iota(jnp.int32, (L, L), 0)
    s_id = lax.broadcasted_iota(jnp.int32, (L, L), 1)
    causal = s_id <= t_id
    lane = lax.broadcasted_iota(jnp.int32, (L, LANES), 1)
    ones_blk = jnp.where(lane == 0, 1.0, 0.0).astype(BF16)
    ones_cols = jnp.ones((dv, LANES), BF16)
    cols = cols_ref[...]

    def col_of(idx):
        if isinstance(idx, int):
            return cols[:, idx:idx + 1]
        return jnp.sum(jnp.where(lane == idx, cols, 0.0), axis=1, keepdims=True)

    gate = []
    for hh in range(hp):
        h = hh if hp == heads else g * hp + hh
        i_row = rows_ref[pl.ds(h, 1), :]
        b_row = rows_ref[pl.ds(heads + h, 1), :]
        i_col = col_of(h)
        b_col = col_of(heads + h)
        m_prev = m_ref[hh]
        log_d = jnp.where(causal, b_col - b_row + i_row, -jnp.inf)
        log_inter = b_col + m_prev
        m_t = jnp.maximum(log_inter, jnp.max(log_d, axis=1, keepdims=True))
        d_mat = jnp.exp(log_d - m_t)
        w_inter = jnp.exp(log_inter - m_t)
        b_last = b_row[:, L - 1:L]
        m_new = jnp.maximum(b_last + m_prev, jnp.max(b_last - b_row + i_row, axis=1, keepdims=True))
        decay = jnp.exp(b_last + m_prev - m_new)
        ws_col = jnp.exp(b_last - b_col + i_col - m_new)
        m_ref[hh] = m_new
        gate.append((m_t, d_mat, w_inter, decay, ws_col))

    s_qk = []
    for hh in range(hp):
        q = q_ref[:, hh * dk:(hh + 1) * dk]
        k = k_ref[:, hh * dk:(hh + 1) * dk]
        s_qk.append((lax.dot_general(q, k, nt, preferred_element_type=F32) * gate[hh][1]).astype(BF16))

    for hh in range(hp):
        m_t, _, w_inter, _, _ = gate[hh]
        q_scaled = q_ref[:, hh * dk:(hh + 1) * dk] * w_inter.astype(BF16)
        v_aug = jnp.concatenate([v_ref[:, hh * dv:(hh + 1) * dv], ones_blk], axis=1)
        lhs = jnp.concatenate([q_scaled, s_qk[hh]], axis=1)
        rhs = jnp.concatenate([c_ref[hh].astype(BF16), v_aug], axis=0)
        num = jnp.dot(lhs, rhs, preferred_element_type=F32)
        d = jnp.maximum(jnp.abs(num[:, dv:dv + 1]), jnp.exp(-m_t))
        nh = num[:, :dv]
        sq_sum = jnp.dot((nh * nh).astype(BF16), ones_cols, preferred_element_type=F32)[:, 0:1]
        r = lax.rsqrt(sq_sum * (1.0 / dv) + NORM_EPS * (d * d))
        u_ref[:, hh * dv:(hh + 1) * dv] = (nh * r * gz_ref[:, hh * dv:(hh + 1) * dv].astype(F32)).astype(BF16)

    for hh in range(hp):
        _, _, _, decay, ws_col = gate[hh]
        k_scaled = k_ref[:, hh * dk:(hh + 1) * dk] * ws_col.astype(BF16)
        v_aug = jnp.concatenate([v_ref[:, hh * dv:(hh + 1) * dv], ones_blk], axis=1)
        upd = lax.dot_general(k_scaled, v_aug, (((0,), (0,)), ((), ())), preferred_element_type=F32)
        c_ref[hh] = decay * c_ref[hh] + upd


def _mlstm_core(qk, v, gz, rows, cols, batch, seq, heads, dk, dv, hp=None):
    m = qk.shape[0]
    hp = heads if hp is None else hp
    L = MLSTM_CHUNK
    nc = seq // L
    groups = heads // hp
    v_width = heads * dv
    k_off = groups
    kern = functools.partial(_mlstm_kernel, chunk=L, heads=heads, hp=hp, dk=dk, dv=dv)
    return pl.pallas_call(
        kern,
        out_shape=jax.ShapeDtypeStruct((m, v_width), BF16),
        grid=(batch, groups, nc),
        in_specs=[
            pl.BlockSpec((L, hp * dk), lambda b, g, c: (b * nc + c, g)),
            pl.BlockSpec((L, hp * dk), lambda b, g, c: (b * nc + c, k_off + g)),
            pl.BlockSpec((L, hp * dv), lambda b, g, c: (b * nc + c, g)),
            pl.BlockSpec((L, hp * dv), lambda b, g, c: (b * nc + c, g)),
            pl.BlockSpec((None, 2 * heads, L), lambda b, g, c: (b, 0, c)),
            pl.BlockSpec((L, LANES), lambda b, g, c: (b * nc + c, 0)),
        ],
        out_specs=pl.BlockSpec((L, hp * dv), lambda b, g, c: (b * nc + c, g)),
        scratch_shapes=[pltpu.VMEM((hp, dk, dv + LANES), F32), pltpu.VMEM((hp, 1, 1), F32)],
        compiler_params=_cparams("parallel", "parallel", "arbitrary"),
        name="mlstm_core",
    )(qk, qk, v, gz, rows, cols)


def _moba_layer(xf, h, gate, w_in, q_norm, k_norm, w_out, batch, seq):
    hd = q_norm.shape[0]
    heads = w_in.shape[1] // 4 // hd
    width = heads * hd
    w_in = w_in.astype(BF16)
    qk = _attn_qk_proj(h, w_in, q_norm, k_norm, seq)
    v = _proj_act(h, w_in, 2 * width, width, None, "attn_v_proj")
    z_act = _proj_act(h, w_in, 3 * width, width, _silu, "attn_z_proj")
    u = _moba_attention(qk, v, z_act, batch, seq, heads, hd)
    return _outproj(u, w_out.astype(BF16), xf, gate, seq, tn=w_out.shape[1])


def _mlstm_layer(xf, h, gate, w_in, gate_b, conv_w, norm_w, w_out, batch, seq):
    heads = gate_b.shape[0] // 2
    v_width = w_out.shape[0]
    dv = v_width // heads
    qk_width = conv_w.shape[1] // 2
    dk = qk_width // heads
    n_main = 2 * qk_width + 3 * v_width
    w_main = w_in[:, :n_main].astype(BF16)
    qk = _mlstm_qk_proj(h, w_main, conv_w, qk_width, dk, seq)
    v = _proj_act(h, w_main, 2 * qk_width, v_width, None, "mlstm_v_proj")
    gz = _mlstm_gz_proj(h, w_main, norm_w, 2 * qk_width + v_width, 2 * qk_width + 2 * v_width, v_width)
    rows, cols = _gate_proj(h, w_in[:, n_main:], gate_b, batch, seq)
    u = _mlstm_core(qk, v, gz, rows, cols, batch, seq, heads, dk, dv)
    return _outproj(u, w_out.astype(BF16), xf, gate, seq)


def kernel(x, c, norm_w, ada_w, ada_b, att_w_in, att_q_norm, att_k_norm, att_w_out, mlstm_w_in,
           mlstm_gate_b, mlstm_conv_w, mlstm_norm_w, mlstm_w_out):
    batch, seq, d = x.shape
    depth = norm_w.shape[0]
    mod = _adaln(c, ada_w, ada_b)
    xf = x.reshape(batch * seq, d)
    for layer in range(depth):
        shift, scale, gate = mod[layer, :, :d], mod[layer, :, d:2 * d], mod[layer, :, 2 * d:]
        h = _norm_mod(xf, norm_w[layer], scale, shift, seq)
        j = layer // 2
        if layer % 2 == 0:
            xf = _moba_layer(xf, h, gate, att_w_in[j], att_q_norm[j], att_k_norm[j], att_w_out[j], batch, seq)
        else:
            xf = _mlstm_layer(xf, h, gate, mlstm_w_in[j], mlstm_gate_b[j], mlstm_conv_w[j],
                              mlstm_norm_w[j], mlstm_w_out[j], batch, seq)
    return xf.reshape(batch, seq, d)
```
